```python
import math
import jax
import jax.numpy as jnp
from jax import lax
import numpy as np

D_MODEL = 1024
BATCH = 2
SEQ = 8192
DEPTH = 4
DEC_BATCH = 32
DEC_SEQ = 4
PAST_LEN = 8192
PAGE_SIZE = 128

N_A_LAYERS = DEPTH // 2
N_B_LAYERS = DEPTH - N_A_LAYERS
GLA_HEADS = 4
GLA_DK = D_MODEL // 2
GLA_DV = D_MODEL
GLA_DK_HEAD = GLA_DK // GLA_HEADS
GLA_DV_HEAD = GLA_DV // GLA_HEADS
GLA_GATE_RANK = 16
GLA_GATE_NORM = 16.0
GLA_CHUNK = 64
ATT_HEADS = 8
HEAD_DIM = D_MODEL // ATT_HEADS
MOBA_BLOCK = 256
MOBA_TOPK = 3
MOBA_SLOTS = MOBA_TOPK + 1
MOBA_QBLOCK = 64
N_BUCKETS = 32
MAX_DISTANCE = 128
D_FF = 4 * D_MODEL
EPS = 1e-6

kernel_name = 'yoco_gla_moba_adaln_decoder_step'


def _rms(x, g):
    xf = x.astype(jnp.float32)
    y = xf * lax.rsqrt(jnp.mean(xf * xf, axis=-1, keepdims=True) + EPS)
    return (y * g.astype(jnp.float32)).astype(x.dtype)


def _ada(c, w, b):
    return jax.nn.silu(c) @ w + b


def _pre(x, g, shift, scale):
    return _rms(x, g) * (1.0 + scale[:, None, :]) + shift[:, None, :]


def _mlp(h, w_up, w_down):
    return jnp.square(jax.nn.relu(h @ w_up)) @ w_down


def _gla_recurrence(q, k, v, log_a, s0):
    B, L, H, DK = q.shape
    DV = v.shape[-1]
    C = math.gcd(GLA_CHUNK, L)
    n = L // C
    f32 = jnp.float32

    def chunks(t):
        return jnp.moveaxis(t.astype(f32).reshape(B, n, C, H, t.shape[-1]), 1, 0)

    causal = jnp.tril(jnp.ones((C, C), dtype=bool))[None, :, :, None, None]

    def step(S, inp):
        qc, kc, vc, ac = inp
        b = jnp.cumsum(ac, axis=1)
        decay = jnp.exp(jnp.where(causal, b[:, :, None] - b[:, None, :], -jnp.inf))
        A = jnp.einsum('bthd,bshd,btshd->bhts', qc, kc, decay)
        o = jnp.einsum('bhts,bshv->bthv', A, vc) + jnp.einsum('bthd,bhdv->bthv', qc * jnp.exp(b), S)
        b_end = b[:, -1]
        S = jnp.exp(b_end)[..., None] * S + jnp.einsum('bshd,bshv->bhdv', kc * jnp.exp(b_end[:, None] - b), vc)
        return S, o

    S, o = lax.scan(step, s0.astype(f32), (chunks(q), chunks(k), chunks(v), chunks(log_a)))
    return jnp.moveaxis(o, 0, 1).reshape(B, L, H, DV), S


def _gla_mixer(h, w_in, w_gate2, b_gate, onorm, w_out, s0):
    B, L, _ = h.shape
    cuts = np.cumsum([GLA_DK, GLA_DK, GLA_DV, GLA_DV]).tolist()
    q, k, v, g, r = jnp.split(h @ w_in, cuts, axis=-1)
    log_a = jax.nn.log_sigmoid((r @ w_gate2 + b_gate).astype(jnp.float32)) / GLA_GATE_NORM
    shk = (B, L, GLA_HEADS, GLA_DK_HEAD)
    shv = (B, L, GLA_HEADS, GLA_DV_HEAD)
    o, s = _gla_recurrence(q.reshape(shk) * GLA_DK_HEAD ** -0.5, k.reshape(shk), v.reshape(shv),
                           log_a.reshape(shk), s0)
    o = _rms(o, onorm).astype(h.dtype).reshape(B, L, GLA_DV) * jax.nn.silu(g)
    return o @ w_out, s.astype(s0.dtype)


def _shared_kv(x, c, kv_ada_w, kv_ada_b, kv_norm, w_kv, k_norm):
    B, L, _ = x.shape
    shift, scale = jnp.split(_ada(c, kv_ada_w, kv_ada_b), 2, axis=-1)
    k, v = jnp.split(_pre(x, kv_norm, shift, scale) @ w_kv, 2, axis=-1)
    k = _rms(k.reshape(B, L, ATT_HEADS, HEAD_DIM), k_norm)
    return k, v.reshape(B, L, ATT_HEADS, HEAD_DIM)


def _rel_bias(dist, table):
    n = jnp.maximum(dist, 0)
    max_exact = N_BUCKETS // 2
    nf = jnp.maximum(n, max_exact).astype(jnp.float32)
    large = max_exact + (jnp.log(nf / max_exact) / math.log(MAX_DISTANCE / max_exact)
                         * (N_BUCKETS - max_exact)).astype(jnp.int32)
    bucket = jnp.where(n < max_exact, n, jnp.minimum(large, N_BUCKETS - 1))
    h_idx = jnp.arange(table.shape[1])[None, :, None, None]
    return table.T.astype(jnp.float32)[h_idx, bucket]


def _attend(q, qpos, k, v, kpos, ok, rel_table):
    f32 = jnp.float32
    logits = jnp.einsum('bhqd,bhqnd->bhqn', q.astype(f32), k.astype(f32)) * (HEAD_DIM ** -0.5)
    logits = logits + _rel_bias(qpos[:, None] - kpos, rel_table)
    p = jax.nn.softmax(jnp.where(ok, logits, -jnp.inf), axis=-1)
    return jnp.einsum('bhqn,bhqnd->bhqd', p, v.astype(f32))


def _select_blocks(q, qpos, means):
    B, H, Q, _ = q.shape
    nb = means.shape[2]
    own = qpos // MOBA_BLOCK
    scores = jnp.einsum('bhqd,bhnd->bhqn', q.astype(jnp.float32), means)
    scores = jnp.where(jnp.arange(nb)[None, :] < own[:, None], scores, -jnp.inf)
    if nb < MOBA_TOPK:
        scores = jnp.pad(scores, ((0, 0), (0, 0), (0, 0), (0, MOBA_TOPK - nb)), constant_values=-jnp.inf)
    top = jnp.minimum(lax.top_k(scores, MOBA_TOPK)[1], nb - 1)
    idx = jnp.concatenate([top, jnp.broadcast_to(own[:, None], (B, H, Q, 1))], axis=-1).astype(jnp.int32)
    valid_top = jnp.arange(MOBA_TOPK)[None, :] < own[:, None]
    valid = jnp.concatenate([jnp.broadcast_to(valid_top, (B, H, Q, MOBA_TOPK)),
                             jnp.ones((B, H, Q, 1), dtype=bool)], axis=-1)
    return idx, valid


def _prompt_context(k, v):
    B, L, H, D = k.shape
    nb = -(-L // MOBA_BLOCK)
    pad = nb * MOBA_BLOCK - L

    def blocks(t):
        t = jnp.pad(t, ((0, 0), (0, pad), (0, 0), (0, 0)))
        return t.reshape(B, nb, MOBA_BLOCK, H, D).transpose(0, 3, 1, 2, 4)

    kb = blocks(k)
    vb = blocks(v)
    means = jnp.mean(kb.astype(jnp.float32), axis=3)
    return kb, vb, means


def _moba_prompt(q, ctx, rel_table):
    kb, vb, means = ctx
    B, L, H, D = q.shape
    qb = math.gcd(MOBA_QBLOCK, L)
    nq = L // qb
    qs = q.reshape(B, nq, qb, H, D).transpose(1, 0, 3, 2, 4)
    b_idx = jnp.arange(B)[:, None, None, None]
    h_idx = jnp.arange(H)[None, :, None, None]
    offs = jnp.arange(MOBA_BLOCK)
    flat = (B, H, qb, MOBA_SLOTS * MOBA_BLOCK)

    def one(args):
        qi, start = args
        qpos = start + jnp.arange(qb)
        idx, valid = _select_blocks(qi, qpos, means)
        kpos = idx[..., None] * MOBA_BLOCK + offs
        ok = valid[..., None] & (kpos <= qpos[:, None, None])
        kg = kb[b_idx, h_idx, idx].reshape(*flat, D)
        vg = vb[b_idx, h_idx, idx].reshape(*flat, D)
        return _attend(qi, qpos, kg, vg, kpos.reshape(flat), ok.reshape(flat), rel_table)

    o = lax.map(one, (qs, jnp.arange(nq, dtype=jnp.int32) * qb))
    return o.transpose(1, 0, 3, 2, 4).reshape(B, L, H, D).astype(q.dtype)


def _sample_context(k, v, cache_k, page_table):
    DB, Lq, H, D = k.shape
    n_pages = page_table.shape[1]
    page = cache_k.shape[1]
    past = n_pages * page
    nb = -(-(past + Lq) // MOBA_BLOCK)
    pos = past + jnp.arange(Lq)
    page_sums = jnp.sum(cache_k.astype(jnp.float32), axis=1)[page_table]
    rows = jnp.concatenate([page_sums, k.astype(jnp.float32)], axis=1)
    seg = jnp.concatenate([jnp.arange(n_pages) * page // MOBA_BLOCK, pos // MOBA_BLOCK])
    sums = jax.ops.segment_sum(jnp.moveaxis(rows, 1, 0), seg, num_segments=nb)
    means = jnp.transpose(sums, (1, 2, 0, 3)) / MOBA_BLOCK
    return k, v, means


def _moba_sample(q, ctx, cache_k, cache_v, page_table, rel_table):
    k_new, v_new, means = ctx
    DB, Lq, H, D = q.shape
    n_pages = page_table.shape[1]
    page = cache_k.shape[1]
    ppb = MOBA_BLOCK // page
    qpos = n_pages * page + jnp.arange(Lq)
    qh = q.transpose(0, 2, 1, 3)
    idx, valid = _select_blocks(qh, qpos, means)
    lp = idx[..., None] * ppb + jnp.arange(ppb)
    page_ok = valid[..., None] & (lp < n_pages)
    phys = page_table[jnp.arange(DB)[:, None, None, None, None], jnp.minimum(lp, n_pages - 1)]
    h_idx = jnp.arange(H)[None, :, None, None, None]
    n_past = MOBA_SLOTS * ppb * page
    kg = cache_k[phys, :, h_idx, :].reshape(DB, H, Lq, n_past, D)
    vg = cache_v[phys, :, h_idx, :].reshape(DB, H, Lq, n_past, D)
    kpos_past = (lp[..., None] * page + jnp.arange(page)).reshape(DB, H, Lq, n_past)
    ok_past = jnp.broadcast_to(page_ok[..., None], (DB, H, Lq, MOBA_SLOTS, ppb, page)).reshape(DB, H, Lq, n_past)
    new_blk = qpos // MOBA_BLOCK
    in_sel = jnp.any((idx[..., None] == new_blk) & valid[..., None], axis=3)
    ok_new = in_sel & (qpos[None, :] <= qpos[:, None])
    kn = jnp.broadcast_to(k_new.transpose(0, 2, 1, 3)[:, :, None], (DB, H, Lq, Lq, D))
    vn = jnp.broadcast_to(v_new.transpose(0, 2, 1, 3)[:, :, None], (DB, H, Lq, Lq, D))
    kpos_new = jnp.broadcast_to(qpos, (DB, H, Lq, Lq))
    o = _attend(qh, qpos,
                jnp.concatenate([kg, kn], axis=3), jnp.concatenate([vg, vn], axis=3),
                jnp.concatenate([kpos_past, kpos_new], axis=3), jnp.concatenate([ok_past, ok_new], axis=3),
                rel_table)
    return o.transpose(0, 2, 1, 3).astype(q.dtype)


def _moba_mixer(h, w_q, q_norm, w_o, attend, ctx):
    B, L, _ = h.shape
    q = _rms((h @ w_q).reshape(B, L, ATT_HEADS, HEAD_DIM), q_norm)
    return attend(q, ctx).reshape(B, L, ATT_HEADS * HEAD_DIM) @ w_o


def _forward(x, c, gla_s0, prep, attend, P):
    states = []
    k = v = ctx = None
    for i in range(DEPTH):
        sh1, sc1, g1, sh2, sc2, g2 = jnp.split(_ada(c, P['ada_w'][i], P['ada_b'][i]), 6, axis=-1)
        if i == N_A_LAYERS:
            k, v = _shared_kv(x, c, P['kv_ada_w'], P['kv_ada_b'], P['kv_norm'], P['w_kv'], P['k_norm'])
            ctx = prep(k, v)
        h = _pre(x, P['norm_mix'][i], sh1, sc1)
        if i < N_A_LAYERS:
            mix, s = _gla_mixer(h, P['gla_w_in'][i], P['gla_w_gate2'][i], P['gla_b_gate'][i],
                                P['gla_onorm'][i], P['gla_w_out'][i], gla_s0[i])
            states.append(s)
        else:
            j = i - N_A_LAYERS
            mix = _moba_mixer(h, P['moba_w_q'][j], P['q_norm'][j], P['moba_w_o'][j], attend, ctx)
        x = x + g1[:, None, :] * mix
        x = x + g2[:, None, :] * _mlp(_pre(x, P['norm_mlp'][i], sh2, sc2), P['mlp_w_up'][i], P['mlp_w_down'][i])
    return x, jnp.stack(states), k, v


def setup_inputs(seed: int = 0) -> dict:
    key = jax.random.key(seed)
    ks = jax.random.split(key, 32)
    f32 = jnp.float32

    def nrm(i, shape, scale=1.0):
        return jax.random.normal(ks[i], shape, f32) * scale

    n_pages = PAST_LEN // PAGE_SIZE
    n_pool = (5 * DEC_BATCH * n_pages) // 4
    page_table = jax.random.permutation(ks[0], n_pool)[: DEC_BATCH * n_pages].reshape(DEC_BATCH, n_pages).astype(jnp.int32)
    gla_in = 2 * GLA_DK + 2 * GLA_DV + GLA_GATE_RANK
    return {
        'x_prompt': nrm(1, (BATCH, SEQ, D_MODEL)),
        'x_sample': nrm(2, (DEC_BATCH, DEC_SEQ, D_MODEL)),
        'state_gla': nrm(3, (N_A_LAYERS, DEC_BATCH, GLA_HEADS, GLA_DK_HEAD, GLA_DV_HEAD)),
        'cache_k': nrm(4, (n_pool, PAGE_SIZE, ATT_HEADS, HEAD_DIM)),
        'cache_v': nrm(5, (n_pool, PAGE_SIZE, ATT_HEADS, HEAD_DIM)),
        'page_table': page_table,
        'c_prompt': nrm(6, (BATCH, D_MODEL)),
        'c_sample': nrm(7, (DEC_BATCH, D_MODEL)),
        'ada_w': nrm(8, (DEPTH, D_MODEL, 6 * D_MODEL), 0.5 * D_MODEL ** -0.5),
        'ada_b': nrm(9, (DEPTH, 6 * D_MODEL), 0.02),
        'norm_mix': 1.0 + nrm(10, (DEPTH, D_MODEL), 0.02),
        'norm_mlp': 1.0 + nrm(11, (DEPTH, D_MODEL), 0.02),
        'gla_w_in': nrm(12, (N_A_LAYERS, D_MODEL, gla_in), D_MODEL ** -0.5),
        'gla_w_gate2': nrm(13, (N_A_LAYERS, GLA_GATE_RANK, GLA_DK), GLA_GATE_RANK ** -0.5),
        'gla_b_gate': nrm(14, (N_A_LAYERS, GLA_DK), 0.02),
        'gla_onorm': 1.0 + nrm(15, (N_A_LAYERS, GLA_DV_HEAD), 0.02),
        'gla_w_out': nrm(16, (N_A_LAYERS, GLA_DV, D_MODEL), GLA_DV ** -0.5),
        'kv_ada_w': nrm(17, (D_MODEL, 2 * D_MODEL), 0.5 * D_MODEL ** -0.5),
        'kv_ada_b': nrm(18, (2 * D_MODEL,), 0.02),
        'kv_norm': 1.0 + nrm(19, (D_MODEL,), 0.02),
        'w_kv': nrm(20, (D_MODEL, 2 * ATT_HEADS * HEAD_DIM), D_MODEL ** -0.5),
        'k_norm': 1.0 + nrm(21, (HEAD_DIM,), 0.02),
        'moba_w_q': nrm(22, (N_B_LAYERS, D_MODEL, ATT_HEADS * HEAD_DIM), D_MODEL ** -0.5),
        'q_norm': 1.0 + nrm(23, (N_B_LAYERS, HEAD_DIM), 0.02),
        'moba_w_o': nrm(24, (N_B_LAYERS, ATT_HEADS * HEAD_DIM, D_MODEL), (ATT_HEADS * HEAD_DIM) ** -0.5),
        'rel_bias': nrm(25, (N_BUCKETS, ATT_HEADS), 0.5),
        'mlp_w_up': nrm(26, (DEPTH, D_MODEL, D_FF), D_MODEL ** -0.5),
        'mlp_w_down': nrm(27, (DEPTH, D_FF, D_MODEL), D_FF ** -0.5),
    }


def reference(x_prompt, x_sample, state_gla, cache_k, cache_v, page_table, c_prompt, c_sample,
              ada_w, ada_b, norm_mix, norm_mlp, gla_w_in, gla_w_gate2, gla_b_gate, gla_onorm, gla_w_out,
              kv_ada_w, kv_ada_b, kv_norm, w_kv, k_norm, moba_w_q, q_norm, moba_w_o, rel_bias,
              mlp_w_up, mlp_w_down):
    P = {'ada_w': ada_w, 'ada_b': ada_b, 'norm_mix': norm_mix, 'norm_mlp': norm_mlp,
         'gla_w_in': gla_w_in, 'gla_w_gate2': gla_w_gate2, 'gla_b_gate': gla_b_gate,
         'gla_onorm': gla_onorm, 'gla_w_out': gla_w_out, 'kv_ada_w': kv_ada_w, 'kv_ada_b': kv_ada_b,
         'kv_norm': kv_norm, 'w_kv': w_kv, 'k_norm': k_norm, 'moba_w_q': moba_w_q, 'q_norm': q_norm,
         'moba_w_o': moba_w_o, 'mlp_w_up': mlp_w_up, 'mlp_w_down': mlp_w_down}
    prompt_s0 = jnp.zeros((N_A_LAYERS, x_prompt.shape[0], GLA_HEADS, GLA_DK_HEAD, GLA_DV_HEAD), state_gla.dtype)
    y_prompt, st_p, k_p, v_p = _forward(
        x_prompt, c_prompt, prompt_s0, _prompt_context,
        lambda q, ctx: _moba_prompt(q, ctx, rel_bias), P)
    y_sample, st_s, k_s, v_s = _forward(
        x_sample, c_sample, state_gla,
        lambda k, v: _sample_context(k, v, cache_k, page_table),
        lambda q, ctx: _moba_sample(q, ctx, cache_k, cache_v, page_table, rel_bias), P)
    return (y_prompt, y_sample, st_p, st_s, k_p, v_p, k_s, v_s)
```

```python
import functools
import math

import numpy as np
import jax
import jax.numpy as jnp
from jax import lax
from jax.experimental import pallas as pl
from jax.experimental.pallas import tpu as pltpu

F32 = jnp.float32
BF16 = jnp.bfloat16

GLA_HEADS = 4
GLA_GATE_NORM = 16.0
GLA_CHUNK = 64
ATT_HEADS = 8
MOBA_BLOCK = 256
MOBA_TOPK = 3
N_BUCKETS = 32
MAX_DISTANCE = 128
EPS = 1e-6

LANES = 128
VMEM_LIMIT = 48 * 1024 * 1024
NEG = -1e30
HIGHEST = lax.Precision.HIGHEST
NT = (((1,), (1,)), ((), ()))
TN = (((0,), (0,)), ((), ()))


def _params(*sem):
    return pltpu.CompilerParams(dimension_semantics=sem, vmem_limit_bytes=VMEM_LIMIT)


def _prenorm(x, g, shift, scale):
    ms = jnp.mean(x * x, axis=-1, keepdims=True)
    return (x * lax.rsqrt(ms + EPS)) * g * (1.0 + scale) + shift


def _head_rms(y, g):
    return y * lax.rsqrt(jnp.mean(y * y, axis=-1, keepdims=True) + EPS) * g


def _row_tile(rows, want):
    tm = min(rows, want)
    assert rows % tm == 0
    return tm


def _mod_spec(mod, rows, tm, ngrid):
    G, R, D = mod.shape
    tiles_per_group = rows // (G * tm)
    assert tiles_per_group * G * tm == rows and R in (1, tm)
    if ngrid == 1:
        return pl.BlockSpec((1, R, D), lambda i: (i // tiles_per_group, 0, 0))
    return pl.BlockSpec((1, R, D), lambda i, k: (i // tiles_per_group, 0, 0))


def _ada_kernel(c_ref, w_ref, b_ref, o_ref):
    c = c_ref[...]
    a = (c * jax.nn.sigmoid(c)).astype(BF16)
    o_ref[...] = jnp.dot(a, w_ref[...].astype(BF16), preferred_element_type=F32) + b_ref[...]


def _ada(c, w, b, tn=1024):
    NL, D, N = w.shape
    M = c.shape[0]
    return pl.pallas_call(
        _ada_kernel,
        out_shape=jax.ShapeDtypeStruct((NL, M, N), F32),
        grid=(NL, N // tn),
        in_specs=[pl.BlockSpec((M, D), lambda l, j: (0, 0)),
                  pl.BlockSpec((None, D, tn), lambda l, j: (l, 0, j)),
                  pl.BlockSpec((None, 1, tn), lambda l, j: (l, 0, j))],
        out_specs=pl.BlockSpec((None, M, tn), lambda l, j: (l, 0, j)),
        compiler_params=_params("parallel", "parallel"),
    )(c, w, b.reshape(NL, 1, N))


def _gla_in_kernel(x_ref, g_ref, sh_ref, sc_ref, w_ref, wr_ref, wg2_ref, bg_ref,
                   qk_ref, v_ref, gt_ref, la_ref, *, dk, dv, qscale):
    h = _prenorm(x_ref[...], g_ref[...], sh_ref[0], sc_ref[0]).astype(BF16)
    qk_ref[:, :dk] = jnp.dot(h, w_ref[:, :dk], preferred_element_type=F32) * qscale
    qk_ref[:, dk:] = jnp.dot(h, w_ref[:, dk:2 * dk], preferred_element_type=F32)
    v_ref[...] = jnp.dot(h, w_ref[:, 2 * dk:2 * dk + dv], preferred_element_type=F32)
    gt_ref[...] = jnp.dot(h, w_ref[:, 2 * dk + dv:], preferred_element_type=F32)
    r = jnp.dot(h, wr_ref[...], preferred_element_type=F32)
    z = jnp.dot(r, wg2_ref[...], precision=HIGHEST, preferred_element_type=F32) + bg_ref[...]
    log_sig = jnp.minimum(z, 0.0) - jnp.log1p(jnp.exp(-jnp.abs(z)))
    la_ref[...] = log_sig * (1.0 / GLA_GATE_NORM)


def _gla_in(x, g, shift, scale, w_main, w_r, w_g2, b_g, *, dk, dv, tm):
    rows, D = x.shape
    tm = _row_tile(rows, tm)
    kern = functools.partial(_gla_in_kernel, dk=dk, dv=dv, qscale=(dk // GLA_HEADS) ** -0.5)
    full = lambda a: pl.BlockSpec(a.shape, lambda i: (0,) * a.ndim)
    row = lambda n: pl.BlockSpec((tm, n), lambda i: (i, 0))
    return pl.pallas_call(
        kern,
        out_shape=(jax.ShapeDtypeStruct((rows, 2 * dk), F32), jax.ShapeDtypeStruct((rows, dv), F32),
                   jax.ShapeDtypeStruct((rows, dv), F32), jax.ShapeDtypeStruct((rows, dk), F32)),
        grid=(rows // tm,),
        in_specs=[row(D), full(g), _mod_spec(shift, rows, tm, 1), _mod_spec(scale, rows, tm, 1),
                  full(w_main), full(w_r), full(w_g2), full(b_g)],
        out_specs=(row(2 * dk), row(dv), row(dv), row(dk)),
        compiler_params=_params("parallel"),
    )(x, g, shift, scale, w_main, w_r, w_g2, b_g)


def _gla_rec_kernel(q_ref, k_ref, v_ref, la_ref, s0_ref, o_ref, s_ref, st_ref, *, chunk, n_inner):
    lt = pl.program_id(2)

    @pl.when(lt == 0)
    def _():
        st_ref[...] = s0_ref[0, 0].T

    r_io = lax.broadcasted_iota(jnp.int32, (chunk, chunk), 0)
    c_io = lax.broadcasted_iota(jnp.int32, (chunk, chunk), 1)
    lower = r_io >= c_io
    tri = lower.astype(F32)

    def body(c, carry):
        r0 = pl.multiple_of(c * chunk, chunk)
        q = q_ref[pl.ds(r0, chunk), :]
        k = k_ref[pl.ds(r0, chunk), :]
        v = v_ref[pl.ds(r0, chunk), :]
        la = la_ref[pl.ds(r0, chunk), :]
        b = jnp.dot(tri, la, precision=HIGHEST, preferred_element_type=F32)
        b_end = b[chunk - 1:chunk, :]
        qd = (q * jnp.exp(b)).astype(BF16)
        kd = (k * jnp.exp(-b)).astype(BF16)
        a = lax.dot_general(qd, kd, NT, preferred_element_type=F32)
        a = jnp.where(lower, a, 0.0).astype(BF16)
        st = st_ref[...]
        vb = v.astype(BF16)
        o = jnp.dot(a, vb, preferred_element_type=F32)
        o = o + lax.dot_general(qd, st.astype(BF16), NT, preferred_element_type=F32)
        o_ref[pl.ds(r0, chunk), :] = o
        ke = (k * jnp.exp(b_end - b)).astype(BF16)
        upd = lax.dot_general(vb, ke, TN, preferred_element_type=F32)
        st_ref[...] = st * jnp.exp(b_end) + upd
        return carry

    lax.fori_loop(0, n_inner, body, 0)

    @pl.when(lt == pl.num_programs(2) - 1)
    def _():
        s_ref[0, 0] = st_ref[...].T


def _gla_rec(qk, v, la, s0, *, seq, chunk, tl):
    B, H, dkh, dvh = s0.shape
    tl = min(tl, seq)
    assert seq % tl == 0 and tl % chunk == 0
    nlt = seq // tl
    kern = functools.partial(_gla_rec_kernel, chunk=chunk, n_inner=tl // chunk)
    return pl.pallas_call(
        kern,
        out_shape=(jax.ShapeDtypeStruct(v.shape, F32), jax.ShapeDtypeStruct(s0.shape, F32)),
        grid=(B, H, nlt),
        in_specs=[pl.BlockSpec((tl, dkh), lambda b, h, t: (b * nlt + t, h)),
                  pl.BlockSpec((tl, dkh), lambda b, h, t: (b * nlt + t, H + h)),
                  pl.BlockSpec((tl, dvh), lambda b, h, t: (b * nlt + t, h)),
                  pl.BlockSpec((tl, dkh), lambda b, h, t: (b * nlt + t, h)),
                  pl.BlockSpec((1, 1, dkh, dvh), lambda b, h, t: (b, h, 0, 0))],
        out_specs=(pl.BlockSpec((tl, dvh), lambda b, h, t: (b * nlt + t, h)),
                   pl.BlockSpec((1, 1, dkh, dvh), lambda b, h, t: (b, h, 0, 0))),
        scratch_shapes=[pltpu.VMEM((dvh, dkh), F32)],
        compiler_params=_params("parallel", "parallel", "arbitrary"),
    )(qk, qk, v, la, s0)


def _gla_out_kernel(o_ref, gt_ref, x_ref, on_ref, w_ref, gate_ref, out_ref, *, dvh):
    acc = jnp.zeros(out_ref.shape, F32)
    for hd in range(GLA_HEADS):
        sl = slice(hd * dvh, (hd + 1) * dvh)
        y = _head_rms(o_ref[:, sl], on_ref[...])
        gt = gt_ref[:, sl]
        y = y * (gt * jax.nn.sigmoid(gt))
        acc = acc + jnp.dot(y.astype(BF16), w_ref[sl, :], preferred_element_type=F32)
    out_ref[...] = x_ref[...] + gate_ref[0] * acc


def _gla_out(o, gt, x, onorm, w_out, gate, *, tm):
    rows, D = x.shape
    tm = _row_tile(rows, tm)
    row = lambda n: pl.BlockSpec((tm, n), lambda i: (i, 0))
    full = lambda a: pl.BlockSpec(a.shape, lambda i: (0,) * a.ndim)
    kern = functools.partial(_gla_out_kernel, dvh=o.shape[1] // GLA_HEADS)
    return pl.pallas_call(
        kern,
        out_shape=jax.ShapeDtypeStruct((rows, D), F32),
        grid=(rows // tm,),
        in_specs=[row(o.shape[1]), row(gt.shape[1]), row(D), full(onorm), full(w_out),
                  _mod_spec(gate, rows, tm, 1)],
        out_specs=row(D),
        compiler_params=_params("parallel"),
    )(o, gt, x, onorm, w_out, gate)


def _mlp_kernel(x_ref, g_ref, sh_ref, sc_ref, gate_ref, wu_ref, wd_ref, out_ref, h_ref):
    k = pl.program_id(1)

    @pl.when(k == 0)
    def _():
        h_ref[...] = _prenorm(x_ref[...], g_ref[...], sh_ref[0], sc_ref[0]).astype(BF16)
        out_ref[...] = jnp.zeros_like(out_ref)

    u = jnp.dot(h_ref[...], wu_ref[...], preferred_element_type=F32)
    u = jnp.square(jnp.maximum(u, 0.0)).astype(BF16)
    out_ref[...] += jnp.dot(u, wd_ref[...], preferred_element_type=F32)

    @pl.when(k == pl.num_programs(1) - 1)
    def _():
        out_ref[...] = x_ref[...] + gate_ref[0] * out_ref[...]


def _mlp(x, g, shift, scale, gate, w_up, w_down, *, tm, tf):
    rows, D = x.shape
    FF = w_up.shape[1]
    tm = _row_tile(rows, tm)
    return pl.pallas_call(
        _mlp_kernel,
        out_shape=jax.ShapeDtypeStruct((rows, D), F32),
        grid=(rows // tm, FF // tf),
        in_specs=[pl.BlockSpec((tm, D), lambda i, k: (i, 0)),
                  pl.BlockSpec(g.shape, lambda i, k: (0, 0)),
                  _mod_spec(shift, rows, tm, 2), _mod_spec(scale, rows, tm, 2),
                  _mod_spec(gate, rows, tm, 2),
                  pl.BlockSpec((D, tf), lambda i, k: (0, k)),
                  pl.BlockSpec((tf, D), lambda i, k: (k, 0))],
        out_specs=pl.BlockSpec((tm, D), lambda i, k: (i, 0)),
        scratch_shapes=[pltpu.VMEM((tm, D), BF16)],
        compiler_params=_params("parallel", "arbitrary"),
    )(x, g, shift, scale, gate, w_up, w_down)


def _kv_kernel(x_ref, g_ref, sh_ref, sc_ref, w_ref, kn_ref, k_ref, v_ref, *rest, hd, nsum):
    h = _prenorm(x_ref[...], g_ref[...], sh_ref[0], sc_ref[0]).astype(BF16)
    n = k_ref.shape[1]
    kk = jnp.dot(h, w_ref[:, :n], preferred_element_type=F32)
    vv = jnp.dot(h, w_ref[:, n:], preferred_element_type=F32)
    v_ref[...] = vv
    if nsum:
        kb_ref, vb_ref, ks_ref = rest
        vb_ref[...] = vv.astype(BF16)
    for a in range(n // hd):
        sl = slice(a * hd, (a + 1) * hd)
        kh = _head_rms(kk[:, sl], kn_ref[...])
        k_ref[:, sl] = kh
        if nsum:
            kb_ref[:, sl] = kh.astype(BF16)
            for r in range(nsum):
                ks_ref[r, :, sl] = jnp.sum(kh[r * MOBA_BLOCK:(r + 1) * MOBA_BLOCK], axis=0, keepdims=True)


def _kv(x, g, shift, scale, w_kv, k_norm, *, tm, block_sums):
    rows, D = x.shape
    n = w_kv.shape[1] // 2
    hd = k_norm.shape[1]
    tm = _row_tile(rows, tm)
    nsum = tm // MOBA_BLOCK if block_sums else 0
    assert not block_sums or tm % MOBA_BLOCK == 0
    row = lambda m: pl.BlockSpec((tm, m), lambda i: (i, 0))
    full = lambda a: pl.BlockSpec(a.shape, lambda i: (0,) * a.ndim)
    out_shape = [jax.ShapeDtypeStruct((rows, n), F32), jax.ShapeDtypeStruct((rows, n), F32)]
    out_specs = [row(n), row(n)]
    if block_sums:
        out_shape += [jax.ShapeDtypeStruct((rows, n), BF16), jax.ShapeDtypeStruct((rows, n), BF16),
                      jax.ShapeDtypeStruct((rows // MOBA_BLOCK, 1, n), F32)]
        out_specs += [row(n), row(n), pl.BlockSpec((nsum, 1, n), lambda i: (i, 0, 0))]
    return pl.pallas_call(
        functools.partial(_kv_kernel, hd=hd, nsum=nsum),
        out_shape=tuple(out_shape),
        grid=(rows // tm,),
        in_specs=[row(D), full(g), _mod_spec(shift, rows, tm, 1), _mod_spec(scale, rows, tm, 1),
                  full(w_kv), full(k_norm)],
        out_specs=tuple(out_specs),
        compiler_params=_params("parallel"),
    )(x, g, shift, scale, w_kv, k_norm)


def _q_kernel(x_ref, g_ref, sh_ref, sc_ref, w_ref, qn_ref, q_ref, *, hd):
    h = _prenorm(x_ref[...], g_ref[...], sh_ref[0], sc_ref[0]).astype(BF16)
    qq = jnp.dot(h, w_ref[...], preferred_element_type=F32)
    for a in range(q_ref.shape[1] // hd):
        sl = slice(a * hd, (a + 1) * hd)
        q_ref[:, sl] = _head_rms(qq[:, sl], qn_ref[...])


def _q_proj(x, g, shift, scale, w_q, q_norm, *, tm):
    rows, D = x.shape
    n = w_q.shape[1]
    tm = _row_tile(rows, tm)
    row = lambda m: pl.BlockSpec((tm, m), lambda i: (i, 0))
    full = lambda a: pl.BlockSpec(a.shape, lambda i: (0,) * a.ndim)
    return pl.pallas_call(
        functools.partial(_q_kernel, hd=q_norm.shape[1]),
        out_shape=jax.ShapeDtypeStruct((rows, n), F32),
        grid=(rows // tm,),
        in_specs=[row(D), full(g), _mod_spec(shift, rows, tm, 1), _mod_spec(scale, rows, tm, 1),
                  full(w_q), full(q_norm)],
        out_specs=row(n),
        compiler_params=_params("parallel"),
    )(x, g, shift, scale, w_q, q_norm)


def _proj_res_kernel(a_ref, x_ref, w_ref, gate_ref, out_ref):
    y = jnp.dot(a_ref[...].astype(BF16), w_ref[...], preferred_element_type=F32)
    out_ref[...] = x_ref[...] + gate_ref[0] * y


def _proj_res(a, x, w, gate, *, tm):
    rows, D = x.shape
    tm = _row_tile(rows, tm)
    row = lambda m: pl.BlockSpec((tm, m), lambda i: (i, 0))
    return pl.pallas_call(
        _proj_res_kernel,
        out_shape=jax.ShapeDtypeStruct((rows, D), F32),
        grid=(rows // tm,),
        in_specs=[row(a.shape[1]), row(D), pl.BlockSpec(w.shape, lambda i: (0, 0)),
                  _mod_spec(gate, rows, tm, 1)],
        out_specs=row(D),
        compiler_params=_params("parallel"),
    )(a, x, w, gate)


def _top_blocks(scores, n_past):
    nbk = scores.shape[1]
    col = lax.broadcasted_iota(jnp.int32, scores.shape, 1)
    sc = jnp.where(col < n_past, scores, -jnp.inf)
    picks = []
    for r in range(MOBA_TOPK):
        m = jnp.max(sc, axis=1, keepdims=True)
        idx = jnp.min(jnp.where(sc == m, col, nbk), axis=1, keepdims=True)
        picks.append((idx, (jnp.zeros_like(idx) + r) < n_past))
        sc = jnp.where(col == idx, -jnp.inf, sc)
    return picks


def _moba_prompt_kernel(q_ref, k_ref, v_ref, ks_ref, bias_ref, o_ref, sel_ref, m_ref, l_ref, acc_ref,
                        *, blk, scale):
    i = pl.program_id(2)
    q = q_ref[...]
    nbk = ks_ref.shape[1]
    means = ks_ref[0] * (1.0 / blk)
    scores = lax.dot_general(q, means, NT, precision=HIGHEST, preferred_element_type=F32)
    col = lax.broadcasted_iota(jnp.int32, scores.shape, 1)
    sel = jnp.zeros(scores.shape, F32)
    for idx, ok in _top_blocks(scores, i):
        sel = jnp.where((col == idx) & ok, 1.0, sel)
    sel_ref[...] = sel
    qs = (q * scale).astype(BF16)

    r0 = pl.multiple_of(i * blk, blk)
    s = lax.dot_general(qs, k_ref[pl.ds(r0, blk), :], NT, preferred_element_type=F32) + bias_ref[0, 0]
    t_io = lax.broadcasted_iota(jnp.int32, s.shape, 0)
    s_io = lax.broadcasted_iota(jnp.int32, s.shape, 1)
    s = jnp.where(s_io <= t_io, s, NEG)
    m0 = jnp.max(s, axis=1, keepdims=True)
    p = jnp.exp(s - m0)
    m_ref[...] = m0
    l_ref[...] = jnp.sum(p, axis=1, keepdims=True)
    acc_ref[...] = jnp.dot(p.astype(BF16), v_ref[pl.ds(r0, blk), :], preferred_element_type=F32)

    def body(j, carry):
        c0 = pl.multiple_of(j * blk, blk)
        s = lax.dot_general(qs, k_ref[pl.ds(c0, blk), :], NT, preferred_element_type=F32)
        s = s + bias_ref[0, jnp.where(j == i - 1, 1, 2)]
        chosen = jnp.sum(jnp.where(col == j, sel_ref[...], 0.0), axis=1, keepdims=True) > 0.0
        s = jnp.where(chosen, s, NEG)
        m_prev = m_ref[...]
        m_new = jnp.maximum(m_prev, jnp.max(s, axis=1, keepdims=True))
        alpha = jnp.exp(m_prev - m_new)
        p = jnp.exp(s - m_new)
        m_ref[...] = m_new
        l_ref[...] = alpha * l_ref[...] + jnp.sum(p, axis=1, keepdims=True)
        acc_ref[...] = alpha * acc_ref[...] + jnp.dot(p.astype(BF16), v_ref[pl.ds(c0, blk), :],
                                                       preferred_element_type=F32)
        return carry

    lax.fori_loop(0, i, body, 0)
    o_ref[...] = (acc_ref[...] / l_ref[...]).astype(o_ref.dtype)


def _moba_prompt(q, kb, vb, ksum, bias, *, batch, seq):
    rows, n = q.shape
    hd = n // ATT_HEADS
    blk = MOBA_BLOCK
    nq = seq // blk
    nbk = ksum.shape[1]
    kern = functools.partial(_moba_prompt_kernel, blk=blk, scale=hd ** -0.5)
    return pl.pallas_call(
        kern,
        out_shape=jax.ShapeDtypeStruct((rows, n), BF16),
        grid=(batch, ATT_HEADS, nq),
        in_specs=[pl.BlockSpec((blk, hd), lambda b, h, i: (b * nq + i, h)),
                  pl.BlockSpec((seq, hd), lambda b, h, i: (b, h)),
                  pl.BlockSpec((seq, hd), lambda b, h, i: (b, h)),
                  pl.BlockSpec((1, nbk, hd), lambda b, h, i: (b, 0, h)),
                  pl.BlockSpec((1, 3, blk, blk), lambda b, h, i: (h, 0, 0, 0))],
        out_specs=pl.BlockSpec((blk, hd), lambda b, h, i: (b * nq + i, h)),
        scratch_shapes=[pltpu.VMEM((blk, nbk), F32), pltpu.VMEM((blk, 1), F32),
                        pltpu.VMEM((blk, 1), F32), pltpu.VMEM((blk, hd), F32)],
        compiler_params=_params("parallel", "parallel", "arbitrary"),
    )(q, kb, vb, ksum, bias)


def _page_sum_kernel(pt_ref, *refs):
    o_ref = refs[-1]
    total = jnp.sum(refs[0][0], axis=0, keepdims=True)
    for r in refs[1:-1]:
        total = total + jnp.sum(r[0], axis=0, keepdims=True)
    o_ref[0] = total


def _sample_block_sums(cache_k2, pt_flat, *, dec_batch, n_pages):
    _, page, n = cache_k2.shape
    ppb = MOBA_BLOCK // page
    nbk = n_pages // ppb

    def page_spec(p):
        return pl.BlockSpec((1, page, n), lambda d, j, pt: (pt[d * n_pages + j * ppb + p], 0, 0))

    out = pl.pallas_call(
        _page_sum_kernel,
        out_shape=jax.ShapeDtypeStruct((dec_batch * nbk, 1, n), F32),
        grid_spec=pltpu.PrefetchScalarGridSpec(
            num_scalar_prefetch=1,
            grid=(dec_batch, nbk),
            in_specs=[page_spec(p) for p in range(ppb)],
            out_specs=pl.BlockSpec((1, 1, n), lambda d, j, pt: (d * nbk + j, 0, 0))),
        compiler_params=_params("parallel", "arbitrary"),
    )(pt_flat, *([cache_k2] * ppb))
    return out.reshape(dec_batch, nbk, n)


def _sample_select_kernel(q_ref, ks_ref, idx_ref, *, hd, blk):
    nbk = ks_ref.shape[1]
    lane = lax.broadcasted_iota(jnp.int32, idx_ref.shape[1:], 1)
    out = jnp.zeros(idx_ref.shape[1:], jnp.int32)
    for a in range(ATT_HEADS):
        sl = slice(a * hd, (a + 1) * hd)
        means = ks_ref[0, :, sl] * (1.0 / blk)
        scores = lax.dot_general(q_ref[0, :, sl], means, NT, precision=HIGHEST,
                                 preferred_element_type=F32)
        for r, (idx, _) in enumerate(_top_blocks(scores, nbk)):
            out = jnp.where(lane == a * MOBA_TOPK + r, idx, out)
    idx_ref[0] = out


def _sample_select(q8, ksum):
    DB, R, n = q8.shape
    nbk = ksum.shape[1]
    assert nbk >= MOBA_TOPK
    kern = functools.partial(_sample_select_kernel, hd=n // ATT_HEADS, blk=MOBA_BLOCK)
    return pl.pallas_call(
        kern,
        out_shape=jax.ShapeDtypeStruct((DB, R, LANES), jnp.int32),
        grid=(DB,),
        in_specs=[pl.BlockSpec((1, R, n), lambda d: (d, 0, 0)),
                  pl.BlockSpec((1, nbk, n), lambda d: (d, 0, 0))],
        out_specs=pl.BlockSpec((1, R, LANES), lambda d: (d, 0, 0)),
        compiler_params=_params("parallel"),
    )(q8, ksum)


def _sample_attn_kernel(pt_ref, sel_ref, q_ref, kn_ref, vn_ref, bias_ref, kp_ref, vp_ref, o_ref,
                        m_ref, l_ref, acc_ref, *, lq_n, ppb, n_pages, scale):
    d, a, s_id, p = (pl.program_id(0), pl.program_id(1), pl.program_id(2), pl.program_id(3))
    nslot = lq_n * MOBA_TOPK
    lq = s_id // MOBA_TOPK
    q = q_ref[0] * scale
    row = lax.broadcasted_iota(jnp.int32, (q.shape[0], 1), 0)

    @pl.when((s_id == 0) & (p == 0))
    def _():
        logits = []
        for s in range(lq_n):
            sc = jnp.sum(q * kn_ref[0, s:s + 1, :], axis=1, keepdims=True) + bias_ref[0, 8:16, s:s + 1]
            logits.append(jnp.where(s <= row, sc, NEG))
        m0 = functools.reduce(jnp.maximum, logits)
        l0 = jnp.zeros_like(m0)
        acc = jnp.zeros(acc_ref.shape, F32)
        for s in range(lq_n):
            e = jnp.exp(logits[s] - m0)
            l0 = l0 + e
            acc = acc + e * vn_ref[0, s:s + 1, :]
        m_ref[...] = m0
        l_ref[...] = l0
        acc_ref[...] = acc

    blk_idx = sel_ref[(d * ATT_HEADS + a) * nslot + s_id]
    lp = blk_idx * ppb + p
    kp = kp_ref[0].astype(BF16)
    s = lax.dot_general(q.astype(BF16), kp, NT, preferred_element_type=F32)
    is_last = (jnp.zeros(s.shape, jnp.int32) + lp) == n_pages - 1
    s = s + jnp.where(is_last, bias_ref[0, 0:8, :], bias_ref[0, 16:24, :])
    s = jnp.where(row == lq, s, NEG)
    m_prev = m_ref[...]
    m_new = jnp.maximum(m_prev, jnp.max(s, axis=1, keepdims=True))
    alpha = jnp.exp(m_prev - m_new)
    pr = jnp.exp(s - m_new)
    m_ref[...] = m_new
    l_ref[...] = alpha * l_ref[...] + jnp.sum(pr, axis=1, keepdims=True)
    acc_ref[...] = alpha * acc_ref[...] + jnp.dot(pr.astype(BF16), vp_ref[0].astype(BF16),
                                                   preferred_element_type=F32)

    @pl.when((s_id == nslot - 1) & (p == ppb - 1))
    def _():
        o_ref[0] = acc_ref[...] / l_ref[...]


def _sample_attn(q8, kn8, vn8, bias, cache_k2, cache_v2, pt_flat, sel_flat, *, lq_n, n_pages):
    DB, R, n = q8.shape
    hd = n // ATT_HEADS
    page = cache_k2.shape[1]
    ppb = MOBA_BLOCK // page
    nslot = lq_n * MOBA_TOPK
    kern = functools.partial(_sample_attn_kernel, lq_n=lq_n, ppb=ppb, n_pages=n_pages, scale=hd ** -0.5)
    head = pl.BlockSpec((1, R, hd), lambda d, a, s, p, pt, sel: (d, 0, a))

    def page_map(d, a, s, p, pt, sel):
        return (pt[d * n_pages + sel[(d * ATT_HEADS + a) * nslot + s] * ppb + p], 0, a)

    return pl.pallas_call(
        kern,
        out_shape=jax.ShapeDtypeStruct((DB, R, n), F32),
        grid_spec=pltpu.PrefetchScalarGridSpec(
            num_scalar_prefetch=2,
            grid=(DB, ATT_HEADS, nslot, ppb),
            in_specs=[head, head, head,
                      pl.BlockSpec((1, 24, LANES), lambda d, a, s, p, pt, sel: (a, 0, 0)),
                      pl.BlockSpec((1, page, hd), page_map),
                      pl.BlockSpec((1, page, hd), page_map)],
            out_specs=head,
            scratch_shapes=[pltpu.VMEM((R, 1), F32), pltpu.VMEM((R, 1), F32), pltpu.VMEM((R, hd), F32)]),
        compiler_params=_params("parallel", "parallel", "arbitrary", "arbitrary"),
    )(pt_flat, sel_flat, q8, kn8, vn8, bias, cache_k2, cache_v2)


def _bucket_of_distance():
    n = np.arange(MAX_DISTANCE)
    max_exact = N_BUCKETS // 2
    nf = np.maximum(n, max_exact).astype(np.float32)
    large = max_exact + (np.log(nf / np.float32(max_exact)) / np.float32(math.log(MAX_DISTANCE / max_exact))
                         * np.float32(N_BUCKETS - max_exact)).astype(np.int32)
    return np.where(n < max_exact, n, np.minimum(large, N_BUCKETS - 1)).astype(np.int32)


def _dist_table(rel_bias):
    return rel_bias.astype(F32)[_bucket_of_distance()].T


def _prompt_bias_tiles(table):
    blk = MOBA_BLOCK
    assert blk >= MAX_DISTANCE
    t = np.arange(blk)[:, None]
    s = np.arange(blk)[None, :]
    far = MAX_DISTANCE - 1
    own = np.clip(t - s, 0, far)
    prev = np.minimum(blk + t - s, far)
    return jnp.stack([table[:, own], table[:, prev], table[:, np.full((blk, blk), far)]], axis=1)


def _sample_bias_rows(table, *, page, lq_n):
    assert page == LANES and page >= MAX_DISTANCE and lq_n <= 8
    far = MAX_DISTANCE - 1
    r = np.arange(8)[:, None]
    c = np.arange(LANES)[None, :]
    last = np.minimum(page + r - c, far)
    new = np.clip(r - c, 0, far)
    return jnp.concatenate([table[:, last], table[:, new], table[:, np.full((8, LANES), far)]], axis=1)


def _pad_tokens(a, groups, per, to):
    return jnp.pad(a.reshape(groups, per, a.shape[-1]), ((0, 0), (0, to - per), (0, 0)))


def _trunk(x, mods, kvmod, s0, W, *, batch, seq, tm, attend, make_ctx):
    n_a = W['gla_w_main'].shape[0]
    depth = W['mlp_w_up'].shape[0]
    dk = W['gla_w_g2'].shape[2]
    dv = W['gla_w_out'].shape[1]
    states = []
    k = v = ctx = None
    for i in range(depth):
        sh1, sc1, g1, sh2, sc2, g2 = mods[i]
        if i == n_a:
            k, v, ctx = make_ctx(x, kvmod)
        if i < n_a:
            qk, vv, gt, la = _gla_in(x, W['norm_mix'][i], sh1, sc1, W['gla_w_main'][i], W['gla_w_r'][i],
                                     W['gla_w_g2'][i], W['gla_b_gate'][i], dk=dk, dv=dv, tm=tm)
            chunk = math.gcd(GLA_CHUNK, seq)
            if chunk < 16:
                padded = 16
                qk, vv, la = (_pad_tokens(t, batch, seq, padded).reshape(batch * padded, -1)
                              for t in (qk, vv, la))
                o, s = _gla_rec(qk, vv, la, s0[i], seq=padded, chunk=padded, tl=padded)
                o = o.reshape(batch, padded, dv)[:, :seq].reshape(batch * seq, dv)
            else:
                o, s = _gla_rec(qk, vv, la, s0[i], seq=seq, chunk=chunk, tl=512)
            states.append(s)
            x = _gla_out(o, gt, x, W['gla_onorm'][i], W['gla_w_out'][i], g1, tm=tm)
        else:
            j = i - n_a
            q = _q_proj(x, W['norm_mix'][i], sh1, sc1, W['moba_w_q'][j], W['q_norm'][j], tm=tm)
            x = _proj_res(attend(q, ctx), x, W['moba_w_o'][j], g1, tm=tm)
        x = _mlp(x, W['norm_mlp'][i], sh2, sc2, g2, W['mlp_w_up'][i], W['mlp_w_down'][i],
                 tm=min(tm * 2, 1024), tf=512)
    return x, jnp.stack(states), k, v


def kernel(x_prompt, x_sample, state_gla, cache_k, cache_v, page_table, c_prompt, c_sample,
           ada_w, ada_b, norm_mix, norm_mlp, gla_w_in, gla_w_gate2, gla_b_gate, gla_onorm, gla_w_out,
           kv_ada_w, kv_ada_b, kv_norm, w_kv, k_norm, moba_w_q, q_norm, moba_w_o, rel_bias,
           mlp_w_up, mlp_w_down):
    B, L, D = x_prompt.shape
    DB, LQ, _ = x_sample.shape
    depth = ada_w.shape[0]
    n_a = gla_w_in.shape[0]
    rank, dk = gla_w_gate2.shape[1:]
    dv = gla_w_out.shape[1]
    n_pool, page, H, hd = cache_k.shape
    n_pages = page_table.shape[1]
    past = n_pages * page
    assert H == ATT_HEADS and L % MOBA_BLOCK == 0 and MOBA_BLOCK % page == 0
    assert past % MOBA_BLOCK == 0 and LQ <= min(8, MOBA_BLOCK) and rank <= LANES

    W = {
        'norm_mix': norm_mix.reshape(depth, 1, D), 'norm_mlp': norm_mlp.reshape(depth, 1, D),
        'gla_w_main': gla_w_in[:, :, :2 * dk + 2 * dv].astype(BF16),
        'gla_w_r': jnp.pad(gla_w_in[:, :, 2 * dk + 2 * dv:], ((0, 0), (0, 0), (0, LANES - rank))).astype(BF16),
        'gla_w_g2': jnp.pad(gla_w_gate2, ((0, 0), (0, LANES - rank), (0, 0))),
        'gla_b_gate': gla_b_gate.reshape(n_a, 1, dk), 'gla_onorm': gla_onorm.reshape(n_a, 1, -1),
        'gla_w_out': gla_w_out.astype(BF16),
        'moba_w_q': moba_w_q.astype(BF16), 'q_norm': q_norm.reshape(-1, 1, hd),
        'moba_w_o': moba_w_o.astype(BF16),
        'mlp_w_up': mlp_w_up.astype(BF16), 'mlp_w_down': mlp_w_down.astype(BF16),
    }
    w_kv_b = w_kv.astype(BF16)
    kv_norm2 = kv_norm.reshape(1, D)
    k_norm2 = k_norm.reshape(1, hd)

    n_c = B + DB
    c_all = jnp.pad(jnp.concatenate([c_prompt, c_sample], axis=0), ((0, -n_c % 8), (0, 0)))
    mod = _ada(c_all, ada_w, ada_b)
    kvm = _ada(c_all, kv_ada_w[None], kv_ada_b[None])[0]

    def prompt_mod(m):
        return m[:B].reshape(B, 1, D)

    def sample_mod(m):
        return jnp.repeat(m[B:n_c], LQ, axis=0).reshape(1, DB * LQ, D)

    def split_mods(pick):
        layers = [[pick(mod[i, :, t * D:(t + 1) * D]) for t in range(6)] for i in range(depth)]
        return layers, [pick(kvm[:, t * D:(t + 1) * D]) for t in range(2)]

    table = _dist_table(rel_bias)

    p_mods, p_kvmod = split_mods(prompt_mod)
    p_bias = _prompt_bias_tiles(table)

    def prompt_ctx(x, kvmod):
        k, v, kb, vb, ksum = _kv(x, kv_norm2, kvmod[0], kvmod[1], w_kv_b, k_norm2, tm=512, block_sums=True)
        return k, v, (kb, vb, ksum.reshape(B, L // MOBA_BLOCK, H * hd))

    def prompt_attend(q, ctx):
        return _moba_prompt(q, ctx[0], ctx[1], ctx[2], p_bias, batch=B, seq=L)

    s0_p = jnp.zeros((n_a, B) + state_gla.shape[2:], F32)
    y_p, st_p, k_p, v_p = _trunk(x_prompt.reshape(B * L, D), p_mods, p_kvmod, s0_p, W, batch=B, seq=L,
                                 tm=512, attend=prompt_attend, make_ctx=prompt_ctx)

    s_mods, s_kvmod = split_mods(sample_mod)
    s_bias = _sample_bias_rows(table, page=page, lq_n=LQ)
    cache_k2 = cache_k.reshape(n_pool, page, H * hd)
    cache_v2 = cache_v.reshape(n_pool, page, H * hd)
    pt_flat = page_table.reshape(-1).astype(jnp.int32)
    rows_s = DB * LQ

    def sample_ctx(x, kvmod):
        k, v = _kv(x, kv_norm2, kvmod[0], kvmod[1], w_kv_b, k_norm2, tm=rows_s, block_sums=False)
        ksum = _sample_block_sums(cache_k2, pt_flat, dec_batch=DB, n_pages=n_pages)
        return k, v, (_pad_tokens(k, DB, LQ, 8), _pad_tokens(v, DB, LQ, 8), ksum)

    def sample_attend(q, ctx):
        kn8, vn8, ksum = ctx
        q8 = _pad_tokens(q, DB, LQ, 8)
        picks = _sample_select(q8, ksum)[:, :LQ, :ATT_HEADS * MOBA_TOPK]
        sel_flat = picks.reshape(DB, LQ, ATT_HEADS, MOBA_TOPK).transpose(0, 2, 1, 3).reshape(-1)
        o8 = _sample_attn(q8, kn8, vn8, s_bias, cache_k2, cache_v2, pt_flat, sel_flat,
                          lq_n=LQ, n_pages=n_pages)
        return o8[:, :LQ].reshape(rows_s, H * hd)

    y_s, st_s, k_s, v_s = _trunk(x_sample.reshape(rows_s, D), s_mods, s_kvmod, state_gla.astype(F32), W,
                                 batch=DB, seq=LQ, tm=rows_s, attend=sample_attend, make_ctx=sample_ctx)

    return (y_p.reshape(B, L, D), y_s.reshape(DB, LQ, D), st_p, st_s,
            k_p.reshape(B, L, H, hd), v_p.reshape(B, L, H, hd),
            k_s.reshape(DB, LQ, H, hd), v_s.reshape(DB, LQ, H, hd))
```

```python
import functools
import math

import numpy as np
import jax
import jax.numpy as jnp
from jax import lax
from jax.experimental import pallas as pl
from jax.experimental.pallas import tpu as pltpu

F32 = jnp.float32
BF16 = jnp.bfloat16

GLA_HEADS = 4
GLA_GATE_NORM = 16.0
GLA_CHUNK = 64
ATT_HEADS = 8
MOBA_BLOCK = 256
MOBA_TOPK = 3
N_BUCKETS = 32
MAX_DISTANCE = 128
EPS = 1e-6

LANES = 128
SUBLANES = 8
VMEM_LIMIT = 48 * 1024 * 1024
NEG = -1e30
HIGHEST = lax.Precision.HIGHEST
NT = (((1,), (1,)), ((), ()))
TN = (((0,), (0,)), ((), ()))


def _params(*sem):
    return pltpu.CompilerParams(dimension_semantics=sem, vmem_limit_bytes=VMEM_LIMIT)


def _prenorm(x, g, shift, scale):
    ms = jnp.mean(x * x, axis=-1, keepdims=True)
    return (x * lax.rsqrt(ms + EPS)) * g * (1.0 + scale) + shift


def _head_rms(y, g):
    return y * lax.rsqrt(jnp.mean(y * y, axis=-1, keepdims=True) + EPS) * g


def _row_tile(rows, want):
    tm = min(rows, want)
    assert rows % tm == 0
    return tm


def _mod_spec(mod, rows, tm, ngrid):
    G, R, D = mod.shape
    tiles_per_group = rows // (G * tm)
    assert tiles_per_group * G * tm == rows and R in (1, tm)
    if ngrid == 1:
        return pl.BlockSpec((1, R, D), lambda i: (i // tiles_per_group, 0, 0))
    return pl.BlockSpec((1, R, D), lambda i, k: (i // tiles_per_group, 0, 0))


def _ada_kernel(c_ref, w_ref, b_ref, o_ref):
    c = c_ref[...]
    a = (c * jax.nn.sigmoid(c)).astype(BF16)
    o_ref[...] = jnp.dot(a, w_ref[...].astype(BF16), preferred_element_type=F32) + b_ref[...]


def _ada(c, w, b, tn=1024):
    NL, D, N = w.shape
    M = c.shape[0]
    return pl.pallas_call(
        _ada_kernel,
        out_shape=jax.ShapeDtypeStruct((NL, M, N), F32),
        grid=(NL, N // tn),
        in_specs=[pl.BlockSpec((M, D), lambda l, j: (0, 0)),
                  pl.BlockSpec((None, D, tn), lambda l, j: (l, 0, j)),
                  pl.BlockSpec((None, 1, tn), lambda l, j: (l, 0, j))],
        out_specs=pl.BlockSpec((None, M, tn), lambda l, j: (l, 0, j)),
        compiler_params=_params("parallel", "parallel"),
        name="ada_modulation",
    )(c, w, b.reshape(NL, 1, N))


def _gla_in_kernel(x_ref, g_ref, sh_ref, sc_ref, w_ref, wr_ref, wg2_ref, bg_ref,
                   qk_ref, v_ref, gt_ref, la_ref, *, dk, dv, qscale):
    h = _prenorm(x_ref[...], g_ref[...], sh_ref[0], sc_ref[0]).astype(BF16)
    qk_ref[:, :dk] = jnp.dot(h, w_ref[:, :dk], preferred_element_type=F32) * qscale
    qk_ref[:, dk:] = jnp.dot(h, w_ref[:, dk:2 * dk], preferred_element_type=F32)
    v_ref[...] = jnp.dot(h, w_ref[:, 2 * dk:2 * dk + dv], preferred_element_type=F32)
    gt_ref[...] = jnp.dot(h, w_ref[:, 2 * dk + dv:], preferred_element_type=F32)
    r = jnp.dot(h, wr_ref[...], preferred_element_type=F32)
    z = jnp.dot(r, wg2_ref[...], precision=HIGHEST, preferred_element_type=F32) + bg_ref[...]
    log_sig = jnp.minimum(z, 0.0) - jnp.log1p(jnp.exp(-jnp.abs(z)))
    la_ref[...] = log_sig * (1.0 / GLA_GATE_NORM)


def _gla_in(x, g, shift, scale, w_main, w_r, w_g2, b_g, *, dk, dv, tm):
    rows, D = x.shape
    tm = _row_tile(rows, tm)
    kern = functools.partial(_gla_in_kernel, dk=dk, dv=dv, qscale=(dk // GLA_HEADS) ** -0.5)
    full = lambda a: pl.BlockSpec(a.shape, lambda i: (0,) * a.ndim)
    row = lambda n: pl.BlockSpec((tm, n), lambda i: (i, 0))
    return pl.pallas_call(
        kern,
        out_shape=(jax.ShapeDtypeStruct((rows, 2 * dk), F32), jax.ShapeDtypeStruct((rows, dv), F32),
                   jax.ShapeDtypeStruct((rows, dv), F32), jax.ShapeDtypeStruct((rows, dk), F32)),
        grid=(rows // tm,),
        in_specs=[row(D), full(g), _mod_spec(shift, rows, tm, 1), _mod_spec(scale, rows, tm, 1),
                  full(w_main), full(w_r), full(w_g2), full(b_g)],
        out_specs=(row(2 * dk), row(dv), row(dv), row(dk)),
        compiler_params=_params("parallel"),
        name="gla_in_proj",
    )(x, g, shift, scale, w_main, w_r, w_g2, b_g)


def _gla_rec_kernel(q_ref, k_ref, v_ref, la_ref, s0_ref, o_ref, s_ref, st_ref, *, chunk, n_inner):
    lt = pl.program_id(2)

    @pl.when(lt == 0)
    def _():
        st_ref[...] = s0_ref[0, 0].T

    r_io = lax.broadcasted_iota(jnp.int32, (chunk, chunk), 0)
    c_io = lax.broadcasted_iota(jnp.int32, (chunk, chunk), 1)
    lower = r_io >= c_io
    tri = lower.astype(F32)

    def body(c, carry):
        r0 = pl.multiple_of(c * chunk, chunk)
        q = q_ref[pl.ds(r0, chunk), :]
        k = k_ref[pl.ds(r0, chunk), :]
        v = v_ref[pl.ds(r0, chunk), :]
        la = la_ref[pl.ds(r0, chunk), :]
        b = jnp.dot(tri, la, precision=HIGHEST, preferred_element_type=F32)
        b_end = b[chunk - 1:chunk, :]
        qd = (q * jnp.exp(b)).astype(BF16)
        kd = (k * jnp.exp(-b)).astype(BF16)
        a = lax.dot_general(qd, kd, NT, preferred_element_type=F32)
        a = jnp.where(lower, a, 0.0).astype(BF16)
        st = st_ref[...]
        vb = v.astype(BF16)
        o = jnp.dot(a, vb, preferred_element_type=F32)
        o = o + lax.dot_general(qd, st.astype(BF16), NT, preferred_element_type=F32)
        o_ref[pl.ds(r0, chunk), :] = o
        ke = (k * jnp.exp(b_end - b)).astype(BF16)
        upd = lax.dot_general(vb, ke, TN, preferred_element_type=F32)
        st_ref[...] = st * jnp.exp(b_end) + upd
        return carry

    lax.fori_loop(0, n_inner, body, 0)

    @pl.when(lt == pl.num_programs(2) - 1)
    def _():
        s_ref[0, 0] = st_ref[...].T


def _gla_rec(qk, v, la, s0, *, seq, chunk, tl):
    B, H, dkh, dvh = s0.shape
    tl = min(tl, seq)
    assert seq % tl == 0 and tl % chunk == 0
    nlt = seq // tl
    kern = functools.partial(_gla_rec_kernel, chunk=chunk, n_inner=tl // chunk)
    return pl.pallas_call(
        kern,
        out_shape=(jax.ShapeDtypeStruct(v.shape, F32), jax.ShapeDtypeStruct(s0.shape, F32)),
        grid=(B, H, nlt),
        in_specs=[pl.BlockSpec((tl, dkh), lambda b, h, t: (b * nlt + t, h)),
                  pl.BlockSpec((tl, dkh), lambda b, h, t: (b * nlt + t, H + h)),
                  pl.BlockSpec((tl, dvh), lambda b, h, t: (b * nlt + t, h)),
                  pl.BlockSpec((tl, dkh), lambda b, h, t: (b * nlt + t, h)),
                  pl.BlockSpec((1, 1, dkh, dvh), lambda b, h, t: (b, h, 0, 0))],
        out_specs=(pl.BlockSpec((tl, dvh), lambda b, h, t: (b * nlt + t, h)),
                   pl.BlockSpec((1, 1, dkh, dvh), lambda b, h, t: (b, h, 0, 0))),
        scratch_shapes=[pltpu.VMEM((dvh, dkh), F32)],
        compiler_params=_params("parallel", "parallel", "arbitrary"),
        name="gla_recurrence",
    )(qk, qk, v, la, s0)


def _gla_out_kernel(o_ref, gt_ref, x_ref, on_ref, w_ref, gate_ref, out_ref, *, dvh):
    acc = jnp.zeros(out_ref.shape, F32)
    for hd in range(GLA_HEADS):
        sl = slice(hd * dvh, (hd + 1) * dvh)
        y = _head_rms(o_ref[:, sl], on_ref[...])
        gt = gt_ref[:, sl]
        y = y * (gt * jax.nn.sigmoid(gt))
        acc = acc + jnp.dot(y.astype(BF16), w_ref[sl, :], preferred_element_type=F32)
    out_ref[...] = x_ref[...] + gate_ref[0] * acc


def _gla_out(o, gt, x, onorm, w_out, gate, *, tm):
    rows, D = x.shape
    tm = _row_tile(rows, tm)
    row = lambda n: pl.BlockSpec((tm, n), lambda i: (i, 0))
    full = lambda a: pl.BlockSpec(a.shape, lambda i: (0,) * a.ndim)
    kern = functools.partial(_gla_out_kernel, dvh=o.shape[1] // GLA_HEADS)
    return pl.pallas_call(
        kern,
        out_shape=jax.ShapeDtypeStruct((rows, D), F32),
        grid=(rows // tm,),
        in_specs=[row(o.shape[1]), row(gt.shape[1]), row(D), full(onorm), full(w_out),
                  _mod_spec(gate, rows, tm, 1)],
        out_specs=row(D),
        compiler_params=_params("parallel"),
        name="gla_out_proj",
    )(o, gt, x, onorm, w_out, gate)


def _mlp_kernel(x_ref, g_ref, sh_ref, sc_ref, gate_ref, wu_ref, wd_ref, out_ref, h_ref):
    k = pl.program_id(1)

    @pl.when(k == 0)
    def _():
        h_ref[...] = _prenorm(x_ref[...], g_ref[...], sh_ref[0], sc_ref[0]).astype(BF16)
        out_ref[...] = jnp.zeros_like(out_ref)

    u = jnp.dot(h_ref[...], wu_ref[...], preferred_element_type=F32)
    u = jnp.square(jnp.maximum(u, 0.0)).astype(BF16)
    out_ref[...] += jnp.dot(u, wd_ref[...], preferred_element_type=F32)

    @pl.when(k == pl.num_programs(1) - 1)
    def _():
        out_ref[...] = x_ref[...] + gate_ref[0] * out_ref[...]


def _mlp(x, g, shift, scale, gate, w_up, w_down, *, tm, tf):
    rows, D = x.shape
    FF = w_up.shape[1]
    tm = _row_tile(rows, tm)
    return pl.pallas_call(
        _mlp_kernel,
        out_shape=jax.ShapeDtypeStruct((rows, D), F32),
        grid=(rows // tm, FF // tf),
        in_specs=[pl.BlockSpec((tm, D), lambda i, k: (i, 0)),
                  pl.BlockSpec(g.shape, lambda i, k: (0, 0)),
                  _mod_spec(shift, rows, tm, 2), _mod_spec(scale, rows, tm, 2),
                  _mod_spec(gate, rows, tm, 2),
                  pl.BlockSpec((D, tf), lambda i, k: (0, k)),
                  pl.BlockSpec((tf, D), lambda i, k: (k, 0))],
        out_specs=pl.BlockSpec((tm, D), lambda i, k: (i, 0)),
        scratch_shapes=[pltpu.VMEM((tm, D), BF16)],
        compiler_params=_params("parallel", "arbitrary"),
        name="mlp",
    )(x, g, shift, scale, gate, w_up, w_down)


def _kv_kernel(x_ref, g_ref, sh_ref, sc_ref, w_ref, kn_ref, k_ref, v_ref, *rest, hd, nsum, seq):
    h = _prenorm(x_ref[...], g_ref[...], sh_ref[0], sc_ref[0]).astype(BF16)
    n = k_ref.shape[1]
    tm = k_ref.shape[0]
    kk = jnp.dot(h, w_ref[:, :n], preferred_element_type=F32)
    vv = jnp.dot(h, w_ref[:, n:], preferred_element_type=F32)
    v_ref[...] = vv
    if nsum:
        ka_ref, va_ref, ks_ref = rest
        pos = (pl.program_id(0) * tm) % seq + lax.broadcasted_iota(jnp.int32, (tm, hd), 0)
        lane = lax.broadcasted_iota(jnp.int32, (tm, hd), 1)
        block_onehot = jnp.where(lane == pos // MOBA_BLOCK, 1.0, 0.0).astype(BF16)
        ones = jnp.ones((tm, hd), BF16)
    for a in range(n // hd):
        sl = slice(a * hd, (a + 1) * hd)
        kh = _head_rms(kk[:, sl], kn_ref[...])
        k_ref[:, sl] = kh
        if nsum:
            ka_ref[:, 2 * a * hd:(2 * a + 1) * hd] = kh.astype(BF16)
            ka_ref[:, (2 * a + 1) * hd:(2 * a + 2) * hd] = block_onehot
            va_ref[:, 2 * a * hd:(2 * a + 1) * hd] = vv[:, sl].astype(BF16)
            va_ref[:, (2 * a + 1) * hd:(2 * a + 2) * hd] = ones
            for r in range(nsum):
                ks_ref[r, :, sl] = jnp.sum(kh[r * MOBA_BLOCK:(r + 1) * MOBA_BLOCK], axis=0, keepdims=True)


def _kv(x, g, shift, scale, w_kv, k_norm, *, tm, seq, for_prompt):
    rows, D = x.shape
    n = w_kv.shape[1] // 2
    hd = k_norm.shape[1]
    tm = _row_tile(rows, tm)
    nsum = tm // MOBA_BLOCK if for_prompt else 0
    assert not for_prompt or (tm % MOBA_BLOCK == 0 and seq // MOBA_BLOCK <= hd)
    row = lambda m: pl.BlockSpec((tm, m), lambda i: (i, 0))
    full = lambda a: pl.BlockSpec(a.shape, lambda i: (0,) * a.ndim)
    out_shape = [jax.ShapeDtypeStruct((rows, n), F32), jax.ShapeDtypeStruct((rows, n), F32)]
    out_specs = [row(n), row(n)]
    if for_prompt:
        out_shape += [jax.ShapeDtypeStruct((rows, 2 * n), BF16), jax.ShapeDtypeStruct((rows, 2 * n), BF16),
                      jax.ShapeDtypeStruct((rows // MOBA_BLOCK, 1, n), F32)]
        out_specs += [row(2 * n), row(2 * n), pl.BlockSpec((nsum, 1, n), lambda i: (i, 0, 0))]
    return pl.pallas_call(
        functools.partial(_kv_kernel, hd=hd, nsum=nsum, seq=seq),
        out_shape=tuple(out_shape),
        grid=(rows // tm,),
        in_specs=[row(D), full(g), _mod_spec(shift, rows, tm, 1), _mod_spec(scale, rows, tm, 1),
                  full(w_kv), full(k_norm)],
        out_specs=tuple(out_specs),
        compiler_params=_params("parallel"),
        name="shared_kv",
    )(x, g, shift, scale, w_kv, k_norm)


def _q_kernel(x_ref, g_ref, sh_ref, sc_ref, w_ref, qn_ref, q_ref, *, hd):
    h = _prenorm(x_ref[...], g_ref[...], sh_ref[0], sc_ref[0]).astype(BF16)
    qq = jnp.dot(h, w_ref[...], preferred_element_type=F32)
    for a in range(q_ref.shape[1] // hd):
        sl = slice(a * hd, (a + 1) * hd)
        q_ref[:, sl] = _head_rms(qq[:, sl], qn_ref[...])


def _q_proj(x, g, shift, scale, w_q, q_norm, *, tm):
    rows, D = x.shape
    n = w_q.shape[1]
    tm = _row_tile(rows, tm)
    row = lambda m: pl.BlockSpec((tm, m), lambda i: (i, 0))
    full = lambda a: pl.BlockSpec(a.shape, lambda i: (0,) * a.ndim)
    return pl.pallas_call(
        functools.partial(_q_kernel, hd=q_norm.shape[1]),
        out_shape=jax.ShapeDtypeStruct((rows, n), F32),
        grid=(rows // tm,),
        in_specs=[row(D), full(g), _mod_spec(shift, rows, tm, 1), _mod_spec(scale, rows, tm, 1),
                  full(w_q), full(q_norm)],
        out_specs=row(n),
        compiler_params=_params("parallel"),
        name="moba_q_proj",
    )(x, g, shift, scale, w_q, q_norm)


def _proj_res_kernel(a_ref, x_ref, w_ref, gate_ref, out_ref):
    y = jnp.dot(a_ref[...].astype(BF16), w_ref[...], preferred_element_type=F32)
    out_ref[...] = x_ref[...] + gate_ref[0] * y


def _proj_res(a, x, w, gate, *, tm):
    rows, D = x.shape
    tm = _row_tile(rows, tm)
    row = lambda m: pl.BlockSpec((tm, m), lambda i: (i, 0))
    return pl.pallas_call(
        _proj_res_kernel,
        out_shape=jax.ShapeDtypeStruct((rows, D), F32),
        grid=(rows // tm,),
        in_specs=[row(a.shape[1]), row(D), pl.BlockSpec(w.shape, lambda i: (0, 0)),
                  _mod_spec(gate, rows, tm, 1)],
        out_specs=row(D),
        compiler_params=_params("parallel"),
        name="moba_out_proj",
    )(a, x, w, gate)


def _top_blocks(scores, n_past, axis):
    nbk = scores.shape[axis]
    blk_id = lax.broadcasted_iota(jnp.int32, scores.shape, axis)
    sc = jnp.where(blk_id < n_past, scores, -jnp.inf)
    picks = []
    for r in range(MOBA_TOPK):
        m = jnp.max(sc, axis=axis, keepdims=True)
        idx = jnp.min(jnp.where(sc == m, blk_id, nbk), axis=axis, keepdims=True)
        picks.append((idx, (jnp.zeros_like(idx) + r) < n_past))
        sc = jnp.where(blk_id == idx, -jnp.inf, sc)
    return picks


def _moba_prompt_kernel(q_ref, ka_ref, va_ref, ks_ref, bias_ref, o_ref, qa_ref, m_ref, acc_ref,
                        sa_ref, sb_ref, *, blk, scale, group):
    i = pl.program_id(2)
    hd = q_ref.shape[1]
    q = q_ref[...]
    nbk = ks_ref.shape[1]
    scores = lax.dot_general(ks_ref[0] * (1.0 / blk), q, NT, precision=HIGHEST,
                             preferred_element_type=F32)
    blk_id = lax.broadcasted_iota(jnp.int32, scores.shape, 0)
    chosen = blk_id < 0
    for idx, ok in _top_blocks(scores, i, 0):
        chosen = chosen | ((blk_id == idx) & ok)
    mask_t = jnp.concatenate([jnp.where(chosen, 0.0, NEG), jnp.zeros((hd - nbk, blk), F32)], axis=0)
    mask = mask_t.T
    col = lax.broadcasted_iota(jnp.int32, mask.shape, 1)
    qs = q * scale
    qb = qs.astype(BF16)
    qa_ref[0] = jnp.concatenate([qs, mask], axis=1).astype(BF16)
    qa_ref[1] = jnp.concatenate([qs, jnp.where(col == i - 1, NEG, mask)], axis=1).astype(BF16)

    r_own = pl.multiple_of(i * blk, blk)
    r_prev = pl.multiple_of(jnp.maximum(i - 1, 0) * blk, blk)
    s_own = lax.dot_general(qb, ka_ref[pl.ds(r_own, blk), :hd], NT, preferred_element_type=F32)
    t_io = lax.broadcasted_iota(jnp.int32, s_own.shape, 0)
    s_io = lax.broadcasted_iota(jnp.int32, s_own.shape, 1)
    s_own = jnp.where(s_io <= t_io, s_own + bias_ref[0, 0], NEG)
    s_prev = lax.dot_general(qa_ref[0], ka_ref[pl.ds(r_prev, blk), :], NT, preferred_element_type=F32)
    s_prev = s_prev + bias_ref[0, 1]
    m0 = jnp.maximum(jnp.max(s_own, axis=1, keepdims=True), jnp.max(s_prev, axis=1, keepdims=True))
    acc = jnp.dot(jnp.exp(s_own - m0).astype(BF16), va_ref[pl.ds(r_own, blk), :], preferred_element_type=F32)
    acc = acc + jnp.dot(jnp.exp(s_prev - m0).astype(BF16), va_ref[pl.ds(r_prev, blk), :],
                        preferred_element_type=F32)
    m_ref[...] = m0
    acc_ref[...] = acc

    last_group = nbk // group - 1

    def rows_of(g):
        return pl.ds(pl.multiple_of(g * (group * blk), group * blk), group * blk)

    def scores_of(g):
        return lax.dot_general(qa_ref[1], ka_ref[rows_of(g), :], NT, preferred_element_type=F32)

    def absorb(s, g):
        m_prev = m_ref[...]
        m_new = jnp.maximum(m_prev, jnp.max(s, axis=1, keepdims=True))
        p = jnp.exp(s - m_new).astype(BF16)
        acc_ref[...] = jnp.exp(m_prev - m_new) * acc_ref[...] + jnp.dot(
            p, va_ref[rows_of(g), :], preferred_element_type=F32)
        m_ref[...] = m_new

    n_pairs = ((i - 2 + group) // group + 1) // 2

    @pl.when(n_pairs > 0)
    def _():
        sa_ref[...] = scores_of(0)

    def body(t, carry):
        sb_ref[...] = scores_of(2 * t + 1)
        absorb(sa_ref[...], 2 * t)
        sa_ref[...] = scores_of(jnp.minimum(2 * t + 2, last_group))
        absorb(sb_ref[...], 2 * t + 1)
        return carry

    lax.fori_loop(0, n_pairs, body, 0)
    acc = acc_ref[...]
    o_ref[...] = (acc[:, :hd] / acc[:, hd:]).astype(o_ref.dtype)


def _moba_prompt(q, ka, va, ksum, bias, *, batch, seq, group=4):
    rows, n = q.shape
    hd = n // ATT_HEADS
    blk = MOBA_BLOCK
    nq = seq // blk
    nbk = ksum.shape[1]
    assert nbk <= hd and nbk % (2 * group) == 0 and group >= 2 and nbk % SUBLANES == 0
    kern = functools.partial(_moba_prompt_kernel, blk=blk, scale=hd ** -0.5, group=group)
    return pl.pallas_call(
        kern,
        out_shape=jax.ShapeDtypeStruct((rows, n), BF16),
        grid=(batch, ATT_HEADS, nq),
        in_specs=[pl.BlockSpec((blk, hd), lambda b, h, i: (b * nq + i, h)),
                  pl.BlockSpec((seq, 2 * hd), lambda b, h, i: (b, h)),
                  pl.BlockSpec((seq, 2 * hd), lambda b, h, i: (b, h)),
                  pl.BlockSpec((1, nbk, hd), lambda b, h, i: (b, 0, h)),
                  pl.BlockSpec((1, 2, blk, blk), lambda b, h, i: (h, 0, 0, 0))],
        out_specs=pl.BlockSpec((blk, hd), lambda b, h, i: (b * nq + i, h)),
        scratch_shapes=[pltpu.VMEM((2, blk, 2 * hd), BF16), pltpu.VMEM((blk, 1), F32),
                        pltpu.VMEM((blk, 2 * hd), F32),
                        pltpu.VMEM((blk, group * blk), F32), pltpu.VMEM((blk, group * blk), F32)],
        compiler_params=_params("parallel", "parallel", "arbitrary"),
        name="moba_prompt_attention",
    )(q, ka, va, ksum, bias)


def _page_sum_kernel(pt_ref, *refs, ppb):
    o_ref = refs[-1]
    pages = refs[:-1]
    for r in range(len(pages) // ppb):
        total = jnp.sum(pages[r * ppb][0], axis=0)
        for p in range(1, ppb):
            total = total + jnp.sum(pages[r * ppb + p][0], axis=0)
        o_ref[0, r] = total


def _sample_block_sums(cache_k, pt_flat, *, dec_batch, n_pages, blocks_per_step=4):
    _, page, H, hd = cache_k.shape
    ppb = MOBA_BLOCK // page
    nbk = n_pages // ppb
    bps = math.gcd(blocks_per_step, nbk)
    nper = bps * ppb

    def page_spec(p):
        return pl.BlockSpec((1, page, H, hd), lambda d, j, pt: (pt[d * n_pages + j * nper + p], 0, 0, 0))

    return pl.pallas_call(
        functools.partial(_page_sum_kernel, ppb=ppb),
        out_shape=jax.ShapeDtypeStruct((dec_batch, nbk, H, hd), F32),
        grid_spec=pltpu.PrefetchScalarGridSpec(
            num_scalar_prefetch=1,
            grid=(dec_batch, nbk // bps),
            in_specs=[page_spec(p) for p in range(nper)],
            out_specs=pl.BlockSpec((1, bps, H, hd), lambda d, j, pt: (d, j, 0, 0))),
        compiler_params=_params("parallel", "arbitrary"),
        name="sample_block_sums",
    )(pt_flat, *([cache_k] * nper))


def _sample_select_kernel(q_ref, ks_ref, idx_ref, *, hd, blk):
    nbk = ks_ref.shape[1]
    lane = lax.broadcasted_iota(jnp.int32, idx_ref.shape[1:], 1)
    out = jnp.zeros(idx_ref.shape[1:], jnp.int32)
    for a in range(ATT_HEADS):
        means = ks_ref[0, :, a, :] * (1.0 / blk)
        scores = lax.dot_general(q_ref[0, :, a * hd:(a + 1) * hd], means, NT, precision=HIGHEST,
                                 preferred_element_type=F32)
        for r, (idx, _) in enumerate(_top_blocks(scores, nbk, 1)):
            out = jnp.where(lane == a * MOBA_TOPK + r, idx, out)
    idx_ref[0] = out


def _sample_select(q8, ksum):
    DB, R, n = q8.shape
    _, nbk, H, hd = ksum.shape
    assert nbk >= MOBA_TOPK and H == ATT_HEADS
    kern = functools.partial(_sample_select_kernel, hd=hd, blk=MOBA_BLOCK)
    return pl.pallas_call(
        kern,
        out_shape=jax.ShapeDtypeStruct((DB, R, LANES), jnp.int32),
        grid=(DB,),
        in_specs=[pl.BlockSpec((1, R, n), lambda d: (d, 0, 0)),
                  pl.BlockSpec((1, nbk, H, hd), lambda d: (d, 0, 0, 0))],
        out_specs=pl.BlockSpec((1, R, LANES), lambda d: (d, 0, 0)),
        compiler_params=_params("parallel"),
        name="sample_select",
    )(q8, ksum)


def _sample_attn_kernel(pt_ref, sel_ref, q_ref, kn_ref, vn_ref, bias_ref, ck_ref, cv_ref, o_ref,
                        kbuf, vbuf, sem, *, lq_n, ppb, n_pages, page, scale):
    g = pl.program_id(0)
    nslot = lq_n * MOBA_TOPK
    nsl = nslot * ppb

    def copies(step, slot):
        d = step // ATT_HEADS
        a = step % ATT_HEADS
        out = []
        for j in range(nsl):
            phys = pt_ref[d * n_pages + sel_ref[step * nslot + j // ppb] * ppb + j % ppb]
            rows = pl.ds(j * page, page)
            out.append(pltpu.make_async_copy(ck_ref.at[phys, :, a, :], kbuf.at[slot, rows, :], sem.at[slot, 0, j]))
            out.append(pltpu.make_async_copy(cv_ref.at[phys, :, a, :], vbuf.at[slot, rows, :], sem.at[slot, 1, j]))
        return out

    @pl.when(g == 0)
    def _():
        for c in copies(0, 0):
            c.start()

    @pl.when(g + 1 < pl.num_programs(0))
    def _():
        for c in copies(g + 1, (g + 1) % 2):
            c.start()

    slot = g % 2
    q = q_ref[0] * scale
    row = lax.broadcasted_iota(jnp.int32, (q.shape[0], 1), 0)

    new_logits = []
    for s in range(lq_n):
        sc = jnp.sum(q * kn_ref[0, s:s + 1, :], axis=1, keepdims=True) + bias_ref[0, 8:16, s:s + 1]
        new_logits.append(jnp.where(s <= row, sc, NEG))

    for c in copies(g, slot):
        c.wait()

    s_all = lax.dot_general(q.astype(BF16), kbuf[slot].astype(BF16), NT, preferred_element_type=F32)
    parts = []
    for j in range(nsl):
        lp = sel_ref[g * nslot + j // ppb] * ppb + j % ppb
        is_last = (jnp.zeros((q.shape[0], page), jnp.int32) + lp) == n_pages - 1
        sj = s_all[:, j * page:(j + 1) * page] + jnp.where(is_last, bias_ref[0, 0:8, :], 0.0)
        parts.append(jnp.where(row == j // (MOBA_TOPK * ppb), sj, NEG))
    m = jnp.max(functools.reduce(jnp.maximum, parts), axis=1, keepdims=True)
    m = functools.reduce(jnp.maximum, new_logits, m)
    probs = [jnp.exp(p - m) for p in parts]
    den = jnp.sum(functools.reduce(jnp.add, probs), axis=1, keepdims=True)
    pcat = jnp.concatenate([p.astype(BF16) for p in probs], axis=1)
    acc = jnp.dot(pcat, vbuf[slot].astype(BF16), preferred_element_type=F32)
    for s in range(lq_n):
        e = jnp.exp(new_logits[s] - m)
        den = den + e
        acc = acc + e * vn_ref[0, s:s + 1, :]
    o_ref[0] = acc / den


def _sample_attn(q8, kn8, vn8, bias, cache_k, cache_v, pt_flat, sel_flat, *, lq_n, n_pages):
    DB, R, n = q8.shape
    _, page, H, hd = cache_k.shape
    ppb = MOBA_BLOCK // page
    nsl = lq_n * MOBA_TOPK * ppb
    assert page == LANES and R == SUBLANES and H == ATT_HEADS
    kern = functools.partial(_sample_attn_kernel, lq_n=lq_n, ppb=ppb, n_pages=n_pages, page=page,
                             scale=hd ** -0.5)
    head = pl.BlockSpec((1, R, hd), lambda g, pt, sel: (g // ATT_HEADS, 0, g % ATT_HEADS))
    return pl.pallas_call(
        kern,
        out_shape=jax.ShapeDtypeStruct((DB, R, n), F32),
        grid_spec=pltpu.PrefetchScalarGridSpec(
            num_scalar_prefetch=2,
            grid=(DB * ATT_HEADS,),
            in_specs=[head, head, head,
                      pl.BlockSpec((1, 2 * SUBLANES, LANES), lambda g, pt, sel: (g % ATT_HEADS, 0, 0)),
                      pl.BlockSpec(memory_space=pl.ANY), pl.BlockSpec(memory_space=pl.ANY)],
            out_specs=head,
            scratch_shapes=[pltpu.VMEM((2, nsl * page, hd), F32), pltpu.VMEM((2, nsl * page, hd), F32),
                            pltpu.SemaphoreType.DMA((2, 2, nsl))]),
        compiler_params=_params("arbitrary"),
        name="sample_attention",
    )(pt_flat, sel_flat, q8, kn8, vn8, bias, cache_k, cache_v)


def _bucket_of_distance():
    n = np.arange(MAX_DISTANCE)
    max_exact = N_BUCKETS // 2
    nf = np.maximum(n, max_exact).astype(np.float32)
    large = max_exact + (np.log(nf / np.float32(max_exact)) / np.float32(math.log(MAX_DISTANCE / max_exact))
                         * np.float32(N_BUCKETS - max_exact)).astype(np.int32)
    return np.where(n < max_exact, n, np.minimum(large, N_BUCKETS - 1)).astype(np.int32)


def _dist_table(rel_bias, width):
    t = rel_bias.astype(F32)[_bucket_of_distance()].T
    t = t - t[:, MAX_DISTANCE - 1:]
    return jnp.pad(t, ((0, 0), (0, width - MAX_DISTANCE)))


def _toeplitz(first_row_wrapped, n):
    H = first_row_wrapped.shape[0]
    x = jnp.broadcast_to(first_row_wrapped[:, None, :], (H, n, 2 * n)).reshape(H, 2 * n * n)
    return x[:, :n * (2 * n - 1)].reshape(H, n, 2 * n - 1)[:, :, :n]


def _prompt_bias_tiles(rel_bias):
    blk = MOBA_BLOCK
    assert blk >= MAX_DISTANCE
    t = _dist_table(rel_bias, blk)
    rev = jnp.flip(t[:, 1:], axis=1)
    zeros = jnp.zeros_like(t)
    own = _toeplitz(jnp.concatenate([t[:, :1], zeros, rev], axis=1), blk)
    prev = _toeplitz(jnp.concatenate([zeros[:, :1], rev, zeros], axis=1), blk)
    return jnp.stack([own, prev], axis=1)


def _sample_bias_rows(rel_bias, *, page, lq_n):
    assert page == LANES and page >= MAX_DISTANCE and lq_n <= SUBLANES
    t = _dist_table(rel_bias, 2 * page)
    r = np.arange(SUBLANES)[:, None]
    c = np.arange(LANES)[None, :]
    return jnp.concatenate([t[:, page + r - c], t[:, np.clip(r - c, 0, None)]], axis=1)


def _pad_tokens(a, groups, per, to):
    return jnp.pad(a.reshape(groups, per, a.shape[-1]), ((0, 0), (0, to - per), (0, 0)))


def _trunk(x, mods, kvmod, s0, W, *, batch, seq, tm, attend, make_ctx):
    n_a = W['gla_w_main'].shape[0]
    depth = W['mlp_w_up'].shape[0]
    dk = W['gla_w_g2'].shape[2]
    dv = W['gla_w_out'].shape[1]
    states = []
    k = v = ctx = None
    for i in range(depth):
        sh1, sc1, g1, sh2, sc2, g2 = mods[i]
        if i == n_a:
            k, v, ctx = make_ctx(x, kvmod)
        if i < n_a:
            qk, vv, gt, la = _gla_in(x, W['norm_mix'][i], sh1, sc1, W['gla_w_main'][i], W['gla_w_r'][i],
                                     W['gla_w_g2'][i], W['gla_b_gate'][i], dk=dk, dv=dv, tm=tm)
            chunk = math.gcd(GLA_CHUNK, seq)
            if chunk < 16:
                padded = 16
                qk, vv, la = (_pad_tokens(t, batch, seq, padded).reshape(batch * padded, -1)
                              for t in (qk, vv, la))
                o, s = _gla_rec(qk, vv, la, s0[i], seq=padded, chunk=padded, tl=padded)
                o = o.reshape(batch, padded, dv)[:, :seq].reshape(batch * seq, dv)
            else:
                o, s = _gla_rec(qk, vv, la, s0[i], seq=seq, chunk=chunk, tl=512)
            states.append(s)
            x = _gla_out(o, gt, x, W['gla_onorm'][i], W['gla_w_out'][i], g1, tm=tm)
        else:
            j = i - n_a
            q = _q_proj(x, W['norm_mix'][i], sh1, sc1, W['moba_w_q'][j], W['q_norm'][j], tm=tm)
            x = _proj_res(attend(q, ctx), x, W['moba_w_o'][j], g1, tm=tm)
        x = _mlp(x, W['norm_mlp'][i], sh2, sc2, g2, W['mlp_w_up'][i], W['mlp_w_down'][i],
                 tm=min(tm * 2, 1024), tf=512)
    return x, jnp.stack(states), k, v


def kernel(x_prompt, x_sample, state_gla, cache_k, cache_v, page_table, c_prompt, c_sample,
           ada_w, ada_b, norm_mix, norm_mlp, gla_w_in, gla_w_gate2, gla_b_gate, gla_onorm, gla_w_out,
           kv_ada_w, kv_ada_b, kv_norm, w_kv, k_norm, moba_w_q, q_norm, moba_w_o, rel_bias,
           mlp_w_up, mlp_w_down):
    B, L, D = x_prompt.shape
    DB, LQ, _ = x_sample.shape
    depth = ada_w.shape[0]
    n_a = gla_w_in.shape[0]
    rank, dk = gla_w_gate2.shape[1:]
    dv = gla_w_out.shape[1]
    n_pool, page, H, hd = cache_k.shape
    n_pages = page_table.shape[1]
    past = n_pages * page
    assert H == ATT_HEADS and L % MOBA_BLOCK == 0 and MOBA_BLOCK % page == 0
    assert past % MOBA_BLOCK == 0 and LQ <= min(SUBLANES, MOBA_BLOCK) and rank <= LANES

    W = {
        'norm_mix': norm_mix.reshape(depth, 1, D), 'norm_mlp': norm_mlp.reshape(depth, 1, D),
        'gla_w_main': gla_w_in[:, :, :2 * dk + 2 * dv].astype(BF16),
        'gla_w_r': jnp.pad(gla_w_in[:, :, 2 * dk + 2 * dv:], ((0, 0), (0, 0), (0, LANES - rank))).astype(BF16),
        'gla_w_g2': jnp.pad(gla_w_gate2, ((0, 0), (0, LANES - rank), (0, 0))),
        'gla_b_gate': gla_b_gate.reshape(n_a, 1, dk), 'gla_onorm': gla_onorm.reshape(n_a, 1, -1),
        'gla_w_out': gla_w_out.astype(BF16),
        'moba_w_q': moba_w_q.astype(BF16), 'q_norm': q_norm.reshape(-1, 1, hd),
        'moba_w_o': moba_w_o.astype(BF16),
        'mlp_w_up': mlp_w_up.astype(BF16), 'mlp_w_down': mlp_w_down.astype(BF16),
    }
    w_kv_b = w_kv.astype(BF16)
    kv_norm2 = kv_norm.reshape(1, D)
    k_norm2 = k_norm.reshape(1, hd)

    n_c = B + DB
    c_all = jnp.pad(jnp.concatenate([c_prompt, c_sample], axis=0), ((0, -n_c % SUBLANES), (0, 0)))
    mod = _ada(c_all, ada_w, ada_b)
    kvm = _ada(c_all, kv_ada_w[None], kv_ada_b[None])[0]

    def prompt_mod(m):
        return m[:B].reshape(B, 1, D)

    def sample_mod(m):
        return jnp.repeat(m[B:n_c], LQ, axis=0).reshape(1, DB * LQ, D)

    def split_mods(pick):
        layers = [[pick(mod[i, :, t * D:(t + 1) * D]) for t in range(6)] for i in range(depth)]
        return layers, [pick(kvm[:, t * D:(t + 1) * D]) for t in range(2)]

    p_mods, p_kvmod = split_mods(prompt_mod)
    p_bias = _prompt_bias_tiles(rel_bias)

    def prompt_ctx(x, kvmod):
        k, v, ka, va, ksum = _kv(x, kv_norm2, kvmod[0], kvmod[1], w_kv_b, k_norm2, tm=512, seq=L,
                                 for_prompt=True)
        return k, v, (ka, va, ksum.reshape(B, L // MOBA_BLOCK, H * hd))

    def prompt_attend(q, ctx):
        return _moba_prompt(q, ctx[0], ctx[1], ctx[2], p_bias, batch=B, seq=L)

    s0_p = jnp.zeros((n_a, B) + state_gla.shape[2:], F32)
    y_p, st_p, k_p, v_p = _trunk(x_prompt.reshape(B * L, D), p_mods, p_kvmod, s0_p, W, batch=B, seq=L,
                                 tm=512, attend=prompt_attend, make_ctx=prompt_ctx)

    s_mods, s_kvmod = split_mods(sample_mod)
    s_bias = _sample_bias_rows(rel_bias, page=page, lq_n=LQ)
    pt_flat = page_table.reshape(-1).astype(jnp.int32)
    rows_s = DB * LQ

    def sample_ctx(x, kvmod):
        k, v = _kv(x, kv_norm2, kvmod[0], kvmod[1], w_kv_b, k_norm2, tm=rows_s, seq=LQ, for_prompt=False)
        ksum = _sample_block_sums(cache_k, pt_flat, dec_batch=DB, n_pages=n_pages)
        return k, v, (_pad_tokens(k, DB, LQ, SUBLANES), _pad_tokens(v, DB, LQ, SUBLANES), ksum)

    def sample_attend(q, ctx):
        kn8, vn8, ksum = ctx
        q8 = _pad_tokens(q, DB, LQ, SUBLANES)
        picks = _sample_select(q8, ksum)[:, :LQ, :ATT_HEADS * MOBA_TOPK]
        sel_flat = picks.reshape(DB, LQ, ATT_HEADS, MOBA_TOPK).transpose(0, 2, 1, 3).reshape(-1)
        o8 = _sample_attn(q8, kn8, vn8, s_bias, cache_k, cache_v, pt_flat, sel_flat,
                          lq_n=LQ, n_pages=n_pages)
        return o8[:, :LQ].reshape(rows_s, H * hd)

    y_s, st_s, k_s, v_s = _trunk(x_sample.reshape(rows_s, D), s_mods, s_kvmod, state_gla.astype(F32), W,
                                 batch=DB, seq=LQ, tm=rows_s, attend=sample_attend, make_ctx=sample_ctx)

    return (y_p.reshape(B, L, D), y_s.reshape(DB, LQ, D), st_p, st_s,
            k_p.reshape(B, L, H, hd), v_p.reshape(B, L, H, hd),
            k_s.reshape(DB, LQ, H, hd), v_s.reshape(DB, LQ, H, hd))
```

```python
import functools
import math

import numpy as np
import jax
import jax.numpy as jnp
from jax import lax
from jax.experimental import pallas as pl
from jax.experimental.pallas import tpu as pltpu

F32 = jnp.float32
BF16 = jnp.bfloat16

GLA_HEADS = 4
GLA_GATE_NORM = 16.0
GLA_CHUNK = 64
ATT_HEADS = 8
MOBA_BLOCK = 256
MOBA_TOPK = 3
N_BUCKETS = 32
MAX_DISTANCE = 128
EPS = 1e-6

LANES = 128
SUBLANES = 8
VMEM_LIMIT = 48 * 1024 * 1024
NEG = -1e30
HIGHEST = lax.Precision.HIGHEST
NT = (((1,), (1,)), ((), ()))
TN = (((0,), (0,)), ((), ()))


def _params(*sem):
    return pltpu.CompilerParams(dimension_semantics=sem, vmem_limit_bytes=VMEM_LIMIT)


def _prenorm(x, g, shift, scale):
    ms = jnp.mean(x * x, axis=-1, keepdims=True)
    return (x * lax.rsqrt(ms + EPS)) * g * (1.0 + scale) + shift


def _head_rms(y, g):
    return y * lax.rsqrt(jnp.mean(y * y, axis=-1, keepdims=True) + EPS) * g


def _row_tile(rows, want):
    tm = min(rows, want)
    assert rows % tm == 0
    return tm


def _mod_spec(mod, rows, tm, ngrid):
    G, R, D = mod.shape
    tiles_per_group = rows // (G * tm)
    assert tiles_per_group * G * tm == rows and R in (1, tm)
    if ngrid == 1:
        return pl.BlockSpec((1, R, D), lambda i: (i // tiles_per_group, 0, 0))
    return pl.BlockSpec((1, R, D), lambda i, k: (i // tiles_per_group, 0, 0))


def _ada_kernel(c_ref, w_ref, b_ref, o_ref):
    c = c_ref[...]
    a = (c * jax.nn.sigmoid(c)).astype(BF16)
    o_ref[...] = jnp.dot(a, w_ref[...].astype(BF16), preferred_element_type=F32) + b_ref[...]


def _ada(c, w, b, tn=1024):
    NL, D, N = w.shape
    M = c.shape[0]
    return pl.pallas_call(
        _ada_kernel,
        out_shape=jax.ShapeDtypeStruct((NL, M, N), F32),
        grid=(NL, N // tn),
        in_specs=[pl.BlockSpec((M, D), lambda l, j: (0, 0)),
                  pl.BlockSpec((None, D, tn), lambda l, j: (l, 0, j)),
                  pl.BlockSpec((None, 1, tn), lambda l, j: (l, 0, j))],
        out_specs=pl.BlockSpec((None, M, tn), lambda l, j: (l, 0, j)),
        compiler_params=_params("parallel", "parallel"),
        name="ada_modulation",
    )(c, w, b.reshape(NL, 1, N))


def _gla_in_kernel(x_ref, g_ref, sh_ref, sc_ref, w_ref, wr_ref, wg2_ref, bg_ref,
                   qk_ref, v_ref, gt_ref, la_ref, *, dk, dv, qscale):
    h = _prenorm(x_ref[...], g_ref[...], sh_ref[0], sc_ref[0]).astype(BF16)
    qk_ref[:, :dk] = jnp.dot(h, w_ref[:, :dk], preferred_element_type=F32) * qscale
    qk_ref[:, dk:] = jnp.dot(h, w_ref[:, dk:2 * dk], preferred_element_type=F32)
    v_ref[...] = jnp.dot(h, w_ref[:, 2 * dk:2 * dk + dv], preferred_element_type=F32)
    gt_ref[...] = jnp.dot(h, w_ref[:, 2 * dk + dv:], preferred_element_type=F32)
    r = jnp.dot(h, wr_ref[...], preferred_element_type=F32)
    z = jnp.dot(r, wg2_ref[...], precision=HIGHEST, preferred_element_type=F32) + bg_ref[...]
    log_sig = jnp.minimum(z, 0.0) - jnp.log1p(jnp.exp(-jnp.abs(z)))
    la_ref[...] = log_sig * (1.0 / GLA_GATE_NORM)


def _gla_in(x, g, shift, scale, w_main, w_r, w_g2, b_g, *, dk, dv, tm):
    rows, D = x.shape
    tm = _row_tile(rows, tm)
    kern = functools.partial(_gla_in_kernel, dk=dk, dv=dv, qscale=(dk // GLA_HEADS) ** -0.5)
    full = lambda a: pl.BlockSpec(a.shape, lambda i: (0,) * a.ndim)
    row = lambda n: pl.BlockSpec((tm, n), lambda i: (i, 0))
    return pl.pallas_call(
        kern,
        out_shape=(jax.ShapeDtypeStruct((rows, 2 * dk), F32), jax.ShapeDtypeStruct((rows, dv), F32),
                   jax.ShapeDtypeStruct((rows, dv), F32), jax.ShapeDtypeStruct((rows, dk), F32)),
        grid=(rows // tm,),
        in_specs=[row(D), full(g), _mod_spec(shift, rows, tm, 1), _mod_spec(scale, rows, tm, 1),
                  full(w_main), full(w_r), full(w_g2), full(b_g)],
        out_specs=(row(2 * dk), row(dv), row(dv), row(dk)),
        compiler_params=_params("parallel"),
        name="gla_in_proj",
    )(x, g, shift, scale, w_main, w_r, w_g2, b_g)


def _gla_rec_kernel(q_ref, k_ref, v_ref, la_ref, s0_ref, o_ref, s_ref, st_ref, *, chunk, n_inner):
    lt = pl.program_id(1)
    H, dvh, dkh = st_ref.shape

    @pl.when(lt == 0)
    def _():
        for hd in range(H):
            st_ref[hd] = s0_ref[0, hd].T

    r_io = lax.broadcasted_iota(jnp.int32, (chunk, chunk), 0)
    c_io = lax.broadcasted_iota(jnp.int32, (chunk, chunk), 1)
    lower = r_io >= c_io
    tri = lower.astype(F32)
    mid = chunk // 2 - 1

    def body(c, carry):
        rows = pl.ds(pl.multiple_of(c * chunk, chunk), chunk)
        b_all = jnp.dot(tri, la_ref[rows, :], precision=HIGHEST, preferred_element_type=F32)
        for hd in range(H):
            ks = slice(hd * dkh, (hd + 1) * dkh)
            vs = slice(hd * dvh, (hd + 1) * dvh)
            q = q_ref[rows, ks]
            k = k_ref[rows, ks]
            vb = v_ref[rows, vs].astype(BF16)
            b = b_all[:, ks]
            b_mid = b[mid:mid + 1, :]
            b_end = b[chunk - 1:chunk, :]
            qm = (q * jnp.exp(b - b_mid)).astype(BF16)
            km = (k * jnp.exp(b_mid - b)).astype(BF16)
            a = lax.dot_general(qm, km, NT, preferred_element_type=F32)
            a = jnp.where(lower, a, 0.0).astype(BF16)
            st = st_ref[hd]
            qd = (q * jnp.exp(b)).astype(BF16)
            o = jnp.dot(a, vb, preferred_element_type=F32)
            o = o + lax.dot_general(qd, st.astype(BF16), NT, preferred_element_type=F32)
            o_ref[rows, vs] = o
            ke = (k * jnp.exp(b_end - b)).astype(BF16)
            upd = lax.dot_general(vb, ke, TN, preferred_element_type=F32)
            st_ref[hd] = st * jnp.exp(b_end) + upd
        return carry

    lax.fori_loop(0, n_inner, body, 0)

    @pl.when(lt == pl.num_programs(1) - 1)
    def _():
        for hd in range(H):
            s_ref[0, hd] = st_ref[hd].T


def _gla_rec(qk, v, la, s0, *, seq, chunk, tl):
    B, H, dkh, dvh = s0.shape
    tl = min(tl, seq)
    assert seq % tl == 0 and tl % chunk == 0
    nlt = seq // tl
    kern = functools.partial(_gla_rec_kernel, chunk=chunk, n_inner=tl // chunk)
    return pl.pallas_call(
        kern,
        out_shape=(jax.ShapeDtypeStruct(v.shape, F32), jax.ShapeDtypeStruct(s0.shape, F32)),
        grid=(B, nlt),
        in_specs=[pl.BlockSpec((tl, H * dkh), lambda b, t: (b * nlt + t, 0)),
                  pl.BlockSpec((tl, H * dkh), lambda b, t: (b * nlt + t, 1)),
                  pl.BlockSpec((tl, H * dvh), lambda b, t: (b * nlt + t, 0)),
                  pl.BlockSpec((tl, H * dkh), lambda b, t: (b * nlt + t, 0)),
                  pl.BlockSpec((1, H, dkh, dvh), lambda b, t: (b, 0, 0, 0))],
        out_specs=(pl.BlockSpec((tl, H * dvh), lambda b, t: (b * nlt + t, 0)),
                   pl.BlockSpec((1, H, dkh, dvh), lambda b, t: (b, 0, 0, 0))),
        scratch_shapes=[pltpu.VMEM((H, dvh, dkh), F32)],
        compiler_params=_params("parallel", "arbitrary"),
        name="gla_recurrence",
    )(qk, qk, v, la, s0)


def _gla_out_kernel(o_ref, gt_ref, x_ref, on_ref, w_ref, gate_ref, out_ref, *, dvh):
    acc = jnp.zeros(out_ref.shape, F32)
    for hd in range(GLA_HEADS):
        sl = slice(hd * dvh, (hd + 1) * dvh)
        y = _head_rms(o_ref[:, sl], on_ref[...])
        gt = gt_ref[:, sl]
        y = y * (gt * jax.nn.sigmoid(gt))
        acc = acc + jnp.dot(y.astype(BF16), w_ref[sl, :], preferred_element_type=F32)
    out_ref[...] = x_ref[...] + gate_ref[0] * acc


def _gla_out(o, gt, x, onorm, w_out, gate, *, tm):
    rows, D = x.shape
    tm = _row_tile(rows, tm)
    row = lambda n: pl.BlockSpec((tm, n), lambda i: (i, 0))
    full = lambda a: pl.BlockSpec(a.shape, lambda i: (0,) * a.ndim)
    kern = functools.partial(_gla_out_kernel, dvh=o.shape[1] // GLA_HEADS)
    return pl.pallas_call(
        kern,
        out_shape=jax.ShapeDtypeStruct((rows, D), F32),
        grid=(rows // tm,),
        in_specs=[row(o.shape[1]), row(gt.shape[1]), row(D), full(onorm), full(w_out),
                  _mod_spec(gate, rows, tm, 1)],
        out_specs=row(D),
        compiler_params=_params("parallel"),
        name="gla_out_proj",
    )(o, gt, x, onorm, w_out, gate)


def _mlp_kernel(x_ref, g_ref, sh_ref, sc_ref, gate_ref, wu_ref, wd_ref, out_ref, h_ref):
    k = pl.program_id(1)

    @pl.when(k == 0)
    def _():
        h_ref[...] = _prenorm(x_ref[...], g_ref[...], sh_ref[0], sc_ref[0]).astype(BF16)
        out_ref[...] = jnp.zeros_like(out_ref)

    u = jnp.dot(h_ref[...], wu_ref[...], preferred_element_type=F32)
    u = jnp.square(jnp.maximum(u, 0.0)).astype(BF16)
    out_ref[...] += jnp.dot(u, wd_ref[...], preferred_element_type=F32)

    @pl.when(k == pl.num_programs(1) - 1)
    def _():
        out_ref[...] = x_ref[...] + gate_ref[0] * out_ref[...]


def _mlp(x, g, shift, scale, gate, w_up, w_down, *, tm, tf):
    rows, D = x.shape
    FF = w_up.shape[1]
    tm = _row_tile(rows, tm)
    return pl.pallas_call(
        _mlp_kernel,
        out_shape=jax.ShapeDtypeStruct((rows, D), F32),
        grid=(rows // tm, FF // tf),
        in_specs=[pl.BlockSpec((tm, D), lambda i, k: (i, 0)),
                  pl.BlockSpec(g.shape, lambda i, k: (0, 0)),
                  _mod_spec(shift, rows, tm, 2), _mod_spec(scale, rows, tm, 2),
                  _mod_spec(gate, rows, tm, 2),
                  pl.BlockSpec((D, tf), lambda i, k: (0, k)),
                  pl.BlockSpec((tf, D), lambda i, k: (k, 0))],
        out_specs=pl.BlockSpec((tm, D), lambda i, k: (i, 0)),
        scratch_shapes=[pltpu.VMEM((tm, D), BF16)],
        compiler_params=_params("parallel", "arbitrary"),
        name="mlp",
    )(x, g, shift, scale, gate, w_up, w_down)


def _kv_kernel(x_ref, g_ref, sh_ref, sc_ref, w_ref, kn_ref, k_ref, v_ref, *rest, hd, nsum, seq):
    h = _prenorm(x_ref[...], g_ref[...], sh_ref[0], sc_ref[0]).astype(BF16)
    n = k_ref.shape[1]
    tm = k_ref.shape[0]
    kk = jnp.dot(h, w_ref[:, :n], preferred_element_type=F32)
    vv = jnp.dot(h, w_ref[:, n:], preferred_element_type=F32)
    v_ref[...] = vv
    if nsum:
        ka_ref, va_ref, ks_ref = rest
        pos = (pl.program_id(0) * tm) % seq + lax.broadcasted_iota(jnp.int32, (tm, hd), 0)
        lane = lax.broadcasted_iota(jnp.int32, (tm, hd), 1)
        block_onehot = jnp.where(lane == pos // MOBA_BLOCK, 1.0, 0.0).astype(BF16)
        ones = jnp.ones((tm, hd), BF16)
    for a in range(n // hd):
        sl = slice(a * hd, (a + 1) * hd)
        kh = _head_rms(kk[:, sl], kn_ref[...])
        k_ref[:, sl] = kh
        if nsum:
            ka_ref[:, 2 * a * hd:(2 * a + 1) * hd] = kh.astype(BF16)
            ka_ref[:, (2 * a + 1) * hd:(2 * a + 2) * hd] = block_onehot
            va_ref[:, 2 * a * hd:(2 * a + 1) * hd] = vv[:, sl].astype(BF16)
            va_ref[:, (2 * a + 1) * hd:(2 * a + 2) * hd] = ones
            for r in range(nsum):
                ks_ref[r, :, sl] = jnp.sum(kh[r * MOBA_BLOCK:(r + 1) * MOBA_BLOCK], axis=0, keepdims=True)


def _kv(x, g, shift, scale, w_kv, k_norm, *, tm, seq, for_prompt):
    rows, D = x.shape
    n = w_kv.shape[1] // 2
    hd = k_norm.shape[1]
    tm = _row_tile(rows, tm)
    nsum = tm // MOBA_BLOCK if for_prompt else 0
    assert not for_prompt or (tm % MOBA_BLOCK == 0 and seq // MOBA_BLOCK <= hd)
    row = lambda m: pl.BlockSpec((tm, m), lambda i: (i, 0))
    full = lambda a: pl.BlockSpec(a.shape, lambda i: (0,) * a.ndim)
    out_shape = [jax.ShapeDtypeStruct((rows, n), F32), jax.ShapeDtypeStruct((rows, n), F32)]
    out_specs = [row(n), row(n)]
    if for_prompt:
        out_shape += [jax.ShapeDtypeStruct((rows, 2 * n), BF16), jax.ShapeDtypeStruct((rows, 2 * n), BF16),
                      jax.ShapeDtypeStruct((rows // MOBA_BLOCK, 1, n), F32)]
        out_specs += [row(2 * n), row(2 * n), pl.BlockSpec((nsum, 1, n), lambda i: (i, 0, 0))]
    return pl.pallas_call(
        functools.partial(_kv_kernel, hd=hd, nsum=nsum, seq=seq),
        out_shape=tuple(out_shape),
        grid=(rows // tm,),
        in_specs=[row(D), full(g), _mod_spec(shift, rows, tm, 1), _mod_spec(scale, rows, tm, 1),
                  full(w_kv), full(k_norm)],
        out_specs=tuple(out_specs),
        compiler_params=_params("parallel"),
        name="shared_kv",
    )(x, g, shift, scale, w_kv, k_norm)


def _q_kernel(x_ref, g_ref, sh_ref, sc_ref, w_ref, qn_ref, q_ref, *, hd):
    h = _prenorm(x_ref[...], g_ref[...], sh_ref[0], sc_ref[0]).astype(BF16)
    qq = jnp.dot(h, w_ref[...], preferred_element_type=F32)
    for a in range(q_ref.shape[1] // hd):
        sl = slice(a * hd, (a + 1) * hd)
        q_ref[:, sl] = _head_rms(qq[:, sl], qn_ref[...])


def _q_proj(x, g, shift, scale, w_q, q_norm, *, tm):
    rows, D = x.shape
    n = w_q.shape[1]
    tm = _row_tile(rows, tm)
    row = lambda m: pl.BlockSpec((tm, m), lambda i: (i, 0))
    full = lambda a: pl.BlockSpec(a.shape, lambda i: (0,) * a.ndim)
    return pl.pallas_call(
        functools.partial(_q_kernel, hd=q_norm.shape[1]),
        out_shape=jax.ShapeDtypeStruct((rows, n), F32),
        grid=(rows // tm,),
        in_specs=[row(D), full(g), _mod_spec(shift, rows, tm, 1), _mod_spec(scale, rows, tm, 1),
                  full(w_q), full(q_norm)],
        out_specs=row(n),
        compiler_params=_params("parallel"),
        name="moba_q_proj",
    )(x, g, shift, scale, w_q, q_norm)


def _top_blocks(scores, n_past, axis):
    nbk = scores.shape[axis]
    blk_id = lax.broadcasted_iota(jnp.int32, scores.shape, axis)
    sc = jnp.where(blk_id < n_past, scores, -jnp.inf)
    picks = []
    for r in range(MOBA_TOPK):
        m = jnp.max(sc, axis=axis, keepdims=True)
        idx = jnp.min(jnp.where(sc == m, blk_id, nbk), axis=axis, keepdims=True)
        picks.append((idx, (jnp.zeros_like(idx) + r) < n_past))
        sc = jnp.where(blk_id == idx, -jnp.inf, sc)
    return picks


def _q_sel_kernel(x_ref, g_ref, sh_ref, sc_ref, w_ref, qn_ref, ks_ref, qa_ref, *, hd, seq, scale):
    tm = x_ref.shape[0]
    nbk = ks_ref.shape[1]
    h = _prenorm(x_ref[...], g_ref[...], sh_ref[0], sc_ref[0]).astype(BF16)
    qq = jnp.dot(h, w_ref[...], preferred_element_type=F32)
    pos = (pl.program_id(0) * tm) % seq + lax.broadcasted_iota(jnp.int32, (1, tm), 1)
    own = pos // MOBA_BLOCK
    blk_id = lax.broadcasted_iota(jnp.int32, (nbk, tm), 0)
    for a in range(ATT_HEADS):
        sl = slice(a * hd, (a + 1) * hd)
        qh = _head_rms(qq[:, sl], qn_ref[...])
        scores = lax.dot_general(ks_ref[0, :, sl] * (1.0 / MOBA_BLOCK), qh, NT, precision=HIGHEST,
                                 preferred_element_type=F32)
        chosen = blk_id < 0
        for idx, ok in _top_blocks(scores, own, 0):
            chosen = chosen | ((blk_id == idx) & ok)
        is_prev = blk_id == own - 1
        far = jnp.where(chosen & jnp.logical_not(is_prev), 0.0, NEG)
        prev = jnp.max(jnp.where(chosen & is_prev, 0.0, NEG), axis=0, keepdims=True)
        mask_t = jnp.concatenate([far, jnp.zeros((hd - nbk - SUBLANES, tm), F32),
                                  jnp.broadcast_to(prev, (SUBLANES, tm))], axis=0)
        qa_ref[:, 2 * a * hd:(2 * a + 1) * hd] = (qh * scale).astype(BF16)
        qa_ref[:, (2 * a + 1) * hd:(2 * a + 2) * hd] = mask_t.T.astype(BF16)


def _q_proj_sel(x, g, shift, scale, w_q, q_norm, ksum, *, tm, seq):
    rows, D = x.shape
    n = w_q.shape[1]
    hd = q_norm.shape[1]
    B, nbk, _ = ksum.shape
    tm = _row_tile(rows, tm)
    tiles_per_seq = seq // tm
    assert seq % tm == 0 and tm % LANES == 0 and nbk % SUBLANES == 0 and nbk + SUBLANES <= hd
    row = lambda m: pl.BlockSpec((tm, m), lambda i: (i, 0))
    full = lambda a: pl.BlockSpec(a.shape, lambda i: (0,) * a.ndim)
    return pl.pallas_call(
        functools.partial(_q_sel_kernel, hd=hd, seq=seq, scale=hd ** -0.5),
        out_shape=jax.ShapeDtypeStruct((rows, 2 * n), BF16),
        grid=(rows // tm,),
        in_specs=[row(D), full(g), _mod_spec(shift, rows, tm, 1), _mod_spec(scale, rows, tm, 1),
                  full(w_q), full(q_norm),
                  pl.BlockSpec((1, nbk, n), lambda i: (i // tiles_per_seq, 0, 0))],
        out_specs=row(2 * n),
        compiler_params=_params("parallel"),
        name="moba_q_proj_select",
    )(x, g, shift, scale, w_q, q_norm, ksum)


def _proj_res_kernel(a_ref, x_ref, w_ref, gate_ref, out_ref):
    y = jnp.dot(a_ref[...].astype(BF16), w_ref[...], preferred_element_type=F32)
    out_ref[...] = x_ref[...] + gate_ref[0] * y


def _proj_res(a, x, w, gate, *, tm):
    rows, D = x.shape
    tm = _row_tile(rows, tm)
    row = lambda m: pl.BlockSpec((tm, m), lambda i: (i, 0))
    return pl.pallas_call(
        _proj_res_kernel,
        out_shape=jax.ShapeDtypeStruct((rows, D), F32),
        grid=(rows // tm,),
        in_specs=[row(a.shape[1]), row(D), pl.BlockSpec(w.shape, lambda i: (0, 0)),
                  _mod_spec(gate, rows, tm, 1)],
        out_specs=row(D),
        compiler_params=_params("parallel"),
        name="moba_out_proj",
    )(a, x, w, gate)


def _moba_prompt_kernel(qa_ref, ka_ref, va_ref, bias_ref, o_ref, m_ref, acc_ref,
                        sa_ref, sb_ref, pa_ref, pb_ref, al_ref, pn_ref, m0_ref, *, blk, group, n_groups):
    i = pl.program_id(2)
    hd = qa_ref.shape[1] // 2
    qb = qa_ref[:, :hd]
    prev_mask = qa_ref[:, 2 * hd - 1:].astype(F32)
    last_group = n_groups - 1

    def rows_of(g):
        return pl.ds(pl.multiple_of(g * (group * blk), group * blk), group * blk)

    def scores_of(g):
        return lax.dot_general(qa_ref[...], ka_ref[rows_of(g), :], NT, preferred_element_type=F32)

    sa_ref[...] = scores_of(0)

    r_own = pl.multiple_of(i * blk, blk)
    r_prev = pl.multiple_of(jnp.maximum(i - 1, 0) * blk, blk)
    s_own = lax.dot_general(qb, ka_ref[pl.ds(r_own, blk), :hd], NT, preferred_element_type=F32)
    t_io = lax.broadcasted_iota(jnp.int32, s_own.shape, 0)
    s_io = lax.broadcasted_iota(jnp.int32, s_own.shape, 1)
    s_own = jnp.where(s_io <= t_io, s_own + bias_ref[0, 0], NEG)
    s_prev = lax.dot_general(qb, ka_ref[pl.ds(r_prev, blk), :hd], NT, preferred_element_type=F32)
    s_prev = s_prev + bias_ref[0, 1] + prev_mask
    m0 = jnp.maximum(jnp.max(s_own, axis=1, keepdims=True), jnp.max(s_prev, axis=1, keepdims=True))
    pn_ref[:, :blk] = jnp.exp(s_prev - m0).astype(BF16)
    pn_ref[:, blk:] = jnp.exp(s_own - m0).astype(BF16)
    m_ref[...] = m0
    m0_ref[...] = m0
    acc_ref[...] = jnp.zeros_like(acc_ref)

    def softmax(s_ref, p_ref, slot):
        m_prev = m_ref[...]
        s = s_ref[...]
        m_new = jnp.maximum(m_prev, jnp.max(s, axis=1, keepdims=True))
        p_ref[...] = jnp.exp(s - m_new).astype(BF16)
        al_ref[slot] = jnp.exp(m_prev - m_new)
        m_ref[...] = m_new

    def flush(p_ref, slot, g):
        acc_ref[...] = al_ref[slot] * acc_ref[...] + jnp.dot(p_ref[...], va_ref[rows_of(g), :],
                                                             preferred_element_type=F32)

    n_pairs = ((i - 2 + group) // group + 1) // 2
    pb_ref[...] = jnp.zeros_like(pb_ref)
    al_ref[1] = jnp.ones(al_ref.shape[1:], F32)

    def body(t, carry):
        sb_ref[...] = scores_of(2 * t + 1)
        flush(pb_ref, 1, jnp.maximum(2 * t - 1, 0))
        softmax(sa_ref, pa_ref, 0)
        sa_ref[...] = scores_of(jnp.minimum(2 * t + 2, last_group))
        flush(pa_ref, 0, 2 * t)
        softmax(sb_ref, pb_ref, 1)
        return carry

    lax.fori_loop(0, n_pairs, body, 0)

    flush(pb_ref, 1, jnp.maximum(2 * n_pairs - 1, 0))
    near = jnp.dot(pn_ref[:, :blk], va_ref[pl.ds(r_prev, blk), :], preferred_element_type=F32)
    near = near + jnp.dot(pn_ref[:, blk:], va_ref[pl.ds(r_own, blk), :], preferred_element_type=F32)
    acc = acc_ref[...] + jnp.exp(m0_ref[...] - m_ref[...]) * near
    o_ref[...] = (acc[:, :hd] / acc[:, hd:]).astype(o_ref.dtype)


def _moba_prompt(qa, ka, va, bias, *, batch, seq, group=4):
    rows, n2 = qa.shape
    hd = n2 // (2 * ATT_HEADS)
    blk = MOBA_BLOCK
    nq = seq // blk
    assert nq % (2 * group) == 0 and group >= 2
    kern = functools.partial(_moba_prompt_kernel, blk=blk, group=group, n_groups=nq // group)
    return pl.pallas_call(
        kern,
        out_shape=jax.ShapeDtypeStruct((rows, ATT_HEADS * hd), BF16),
        grid=(batch, ATT_HEADS, nq),
        in_specs=[pl.BlockSpec((blk, 2 * hd), lambda b, h, i: (b * nq + i, h)),
                  pl.BlockSpec((seq, 2 * hd), lambda b, h, i: (b, h)),
                  pl.BlockSpec((seq, 2 * hd), lambda b, h, i: (b, h)),
                  pl.BlockSpec((1, 2, blk, blk), lambda b, h, i: (h, 0, 0, 0))],
        out_specs=pl.BlockSpec((blk, hd), lambda b, h, i: (b * nq + i, h)),
        scratch_shapes=[pltpu.VMEM((blk, 1), F32), pltpu.VMEM((blk, 2 * hd), F32),
                        pltpu.VMEM((blk, group * blk), F32), pltpu.VMEM((blk, group * blk), F32),
                        pltpu.VMEM((blk, group * blk), BF16), pltpu.VMEM((blk, group * blk), BF16),
                        pltpu.VMEM((2, blk, 1), F32), pltpu.VMEM((blk, 2 * blk), BF16),
                        pltpu.VMEM((blk, 1), F32)],
        compiler_params=_params("parallel", "parallel", "arbitrary"),
        name="moba_prompt_attention",
    )(qa, ka, va, bias)


def _page_sum_kernel(pt_ref, *refs, ppb):
    o_ref = refs[-1]
    pages = refs[:-1]
    for r in range(len(pages) // ppb):
        total = jnp.sum(pages[r * ppb][0], axis=0)
        for p in range(1, ppb):
            total = total + jnp.sum(pages[r * ppb + p][0], axis=0)
        o_ref[0, r] = total


def _sample_block_sums(cache_k, pt_flat, *, dec_batch, n_pages, blocks_per_step=4):
    _, page, H, hd = cache_k.shape
    ppb = MOBA_BLOCK // page
    nbk = n_pages // ppb
    bps = math.gcd(blocks_per_step, nbk)
    nper = bps * ppb

    def page_spec(p):
        return pl.BlockSpec((1, page, H, hd), lambda d, j, pt: (pt[d * n_pages + j * nper + p], 0, 0, 0))

    return pl.pallas_call(
        functools.partial(_page_sum_kernel, ppb=ppb),
        out_shape=jax.ShapeDtypeStruct((dec_batch, nbk, H, hd), F32),
        grid_spec=pltpu.PrefetchScalarGridSpec(
            num_scalar_prefetch=1,
            grid=(dec_batch, nbk // bps),
            in_specs=[page_spec(p) for p in range(nper)],
            out_specs=pl.BlockSpec((1, bps, H, hd), lambda d, j, pt: (d, j, 0, 0))),
        compiler_params=_params("parallel", "arbitrary"),
        name="sample_block_sums",
    )(pt_flat, *([cache_k] * nper))


def _sample_select_kernel(q_ref, ks_ref, idx_ref, *, hd, blk):
    nbk = ks_ref.shape[1]
    lane = lax.broadcasted_iota(jnp.int32, idx_ref.shape[1:], 1)
    out = jnp.zeros(idx_ref.shape[1:], jnp.int32)
    for a in range(ATT_HEADS):
        means = ks_ref[0, :, a, :] * (1.0 / blk)
        scores = lax.dot_general(q_ref[0, :, a * hd:(a + 1) * hd], means, NT, precision=HIGHEST,
                                 preferred_element_type=F32)
        for r, (idx, _) in enumerate(_top_blocks(scores, nbk, 1)):
            out = jnp.where(lane == a * MOBA_TOPK + r, idx, out)
    idx_ref[0] = out


def _sample_select(q8, ksum):
    DB, R, n = q8.shape
    _, nbk, H, hd = ksum.shape
    assert nbk >= MOBA_TOPK and H == ATT_HEADS
    kern = functools.partial(_sample_select_kernel, hd=hd, blk=MOBA_BLOCK)
    return pl.pallas_call(
        kern,
        out_shape=jax.ShapeDtypeStruct((DB, R, LANES), jnp.int32),
        grid=(DB,),
        in_specs=[pl.BlockSpec((1, R, n), lambda d: (d, 0, 0)),
                  pl.BlockSpec((1, nbk, H, hd), lambda d: (d, 0, 0, 0))],
        out_specs=pl.BlockSpec((1, R, LANES), lambda d: (d, 0, 0)),
        compiler_params=_params("parallel"),
        name="sample_select",
    )(q8, ksum)


def _sample_attn_kernel(pt_ref, sel_ref, q_ref, kn_ref, vn_ref, bias_ref, ck_ref, cv_ref, o_ref,
                        kbuf, vbuf, sem, *, lq_n, ppb, n_pages, page, scale):
    g = pl.program_id(0)
    nslot = lq_n * MOBA_TOPK
    nsl = nslot * ppb

    def copies(step, slot):
        d = step // ATT_HEADS
        a = step % ATT_HEADS
        out = []
        for j in range(nsl):
            phys = pt_ref[d * n_pages + sel_ref[step * nslot + j // ppb] * ppb + j % ppb]
            rows = pl.ds(j * page, page)
            out.append(pltpu.make_async_copy(ck_ref.at[phys, :, a, :], kbuf.at[slot, rows, :], sem.at[slot, 0, j]))
            out.append(pltpu.make_async_copy(cv_ref.at[phys, :, a, :], vbuf.at[slot, rows, :], sem.at[slot, 1, j]))
        return out

    @pl.when(g == 0)
    def _():
        for c in copies(0, 0):
            c.start()

    @pl.when(g + 1 < pl.num_programs(0))
    def _():
        for c in copies(g + 1, (g + 1) % 2):
            c.start()

    slot = g % 2
    q = q_ref[0] * scale
    row = lax.broadcasted_iota(jnp.int32, (q.shape[0], 1), 0)

    new_logits = []
    for s in range(lq_n):
        sc = jnp.sum(q * kn_ref[0, s:s + 1, :], axis=1, keepdims=True) + bias_ref[0, 8:16, s:s + 1]
        new_logits.append(jnp.where(s <= row, sc, NEG))

    for c in copies(g, slot):
        c.wait()

    s_all = lax.dot_general(q.astype(BF16), kbuf[slot].astype(BF16), NT, preferred_element_type=F32)
    parts = []
    for j in range(nsl):
        lp = sel_ref[g * nslot + j // ppb] * ppb + j % ppb
        is_last = (jnp.zeros((q.shape[0], page), jnp.int32) + lp) == n_pages - 1
        sj = s_all[:, j * page:(j + 1) * page] + jnp.where(is_last, bias_ref[0, 0:8, :], 0.0)
        parts.append(jnp.where(row == j // (MOBA_TOPK * ppb), sj, NEG))
    m = jnp.max(functools.reduce(jnp.maximum, parts), axis=1, keepdims=True)
    m = functools.reduce(jnp.maximum, new_logits, m)
    probs = [jnp.exp(p - m) for p in parts]
    den = jnp.sum(functools.reduce(jnp.add, probs), axis=1, keepdims=True)
    pcat = jnp.concatenate([p.astype(BF16) for p in probs], axis=1)
    acc = jnp.dot(pcat, vbuf[slot].astype(BF16), preferred_element_type=F32)
    for s in range(lq_n):
        e = jnp.exp(new_logits[s] - m)
        den = den + e
        acc = acc + e * vn_ref[0, s:s + 1, :]
    o_ref[0] = acc / den


def _sample_attn(q8, kn8, vn8, bias, cache_k, cache_v, pt_flat, sel_flat, *, lq_n, n_pages):
    DB, R, n = q8.shape
    _, page, H, hd = cache_k.shape
    ppb = MOBA_BLOCK // page
    nsl = lq_n * MOBA_TOPK * ppb
    assert page == LANES and R == SUBLANES and H == ATT_HEADS
    kern = functools.partial(_sample_attn_kernel, lq_n=lq_n, ppb=ppb, n_pages=n_pages, page=page,
                             scale=hd ** -0.5)
    head = pl.BlockSpec((1, R, hd), lambda g, pt, sel: (g // ATT_HEADS, 0, g % ATT_HEADS))
    return pl.pallas_call(
        kern,
        out_shape=jax.ShapeDtypeStruct((DB, R, n), F32),
        grid_spec=pltpu.PrefetchScalarGridSpec(
            num_scalar_prefetch=2,
            grid=(DB * ATT_HEADS,),
            in_specs=[head, head, head,
                      pl.BlockSpec((1, 2 * SUBLANES, LANES), lambda g, pt, sel: (g % ATT_HEADS, 0, 0)),
                      pl.BlockSpec(memory_space=pl.ANY), pl.BlockSpec(memory_space=pl.ANY)],
            out_specs=head,
            scratch_shapes=[pltpu.VMEM((2, nsl * page, hd), F32), pltpu.VMEM((2, nsl * page, hd), F32),
                            pltpu.SemaphoreType.DMA((2, 2, nsl))]),
        compiler_params=_params("arbitrary"),
        name="sample_attention",
    )(pt_flat, sel_flat, q8, kn8, vn8, bias, cache_k, cache_v)


def _bucket_of_distance():
    n = np.arange(MAX_DISTANCE)
    max_exact = N_BUCKETS // 2
    nf = np.maximum(n, max_exact).astype(np.float32)
    large = max_exact + (np.log(nf / np.float32(max_exact)) / np.float32(math.log(MAX_DISTANCE / max_exact))
                         * np.float32(N_BUCKETS - max_exact)).astype(np.int32)
    return np.where(n < max_exact, n, np.minimum(large, N_BUCKETS - 1)).astype(np.int32)


def _dist_table(rel_bias, width):
    t = rel_bias.astype(F32)[_bucket_of_distance()].T
    t = t - t[:, MAX_DISTANCE - 1:]
    return jnp.pad(t, ((0, 0), (0, width - MAX_DISTANCE)))


def _toeplitz(first_row_wrapped, n):
    H = first_row_wrapped.shape[0]
    x = jnp.broadcast_to(first_row_wrapped[:, None, :], (H, n, 2 * n)).reshape(H, 2 * n * n)
    return x[:, :n * (2 * n - 1)].reshape(H, n, 2 * n - 1)[:, :, :n]


def _prompt_bias_tiles(rel_bias):
    blk = MOBA_BLOCK
    assert blk >= MAX_DISTANCE
    t = _dist_table(rel_bias, blk)
    rev = jnp.flip(t[:, 1:], axis=1)
    zeros = jnp.zeros_like(t)
    own = _toeplitz(jnp.concatenate([t[:, :1], zeros, rev], axis=1), blk)
    prev = _toeplitz(jnp.concatenate([zeros[:, :1], rev, zeros], axis=1), blk)
    return jnp.stack([own, prev], axis=1)


def _sample_bias_rows(rel_bias, *, page, lq_n):
    assert page == LANES and page >= MAX_DISTANCE and lq_n <= SUBLANES
    t = _dist_table(rel_bias, 2 * page)
    r = np.arange(SUBLANES)[:, None]
    c = np.arange(LANES)[None, :]
    return jnp.concatenate([t[:, page + r - c], t[:, np.clip(r - c, 0, None)]], axis=1)


def _pad_tokens(a, groups, per, to):
    return jnp.pad(a.reshape(groups, per, a.shape[-1]), ((0, 0), (0, to - per), (0, 0)))


def _trunk(x, mods, kvmod, s0, W, *, batch, seq, tm, attend, make_ctx):
    n_a = W['gla_w_main'].shape[0]
    depth = W['mlp_w_up'].shape[0]
    dk = W['gla_w_g2'].shape[2]
    dv = W['gla_w_out'].shape[1]
    states = []
    k = v = ctx = None
    for i in range(depth):
        sh1, sc1, g1, sh2, sc2, g2 = mods[i]
        if i == n_a:
            k, v, ctx = make_ctx(x, kvmod)
        if i < n_a:
            qk, vv, gt, la = _gla_in(x, W['norm_mix'][i], sh1, sc1, W['gla_w_main'][i], W['gla_w_r'][i],
                                     W['gla_w_g2'][i], W['gla_b_gate'][i], dk=dk, dv=dv, tm=tm)
            chunk = math.gcd(GLA_CHUNK, seq)
            if chunk < 16:
                padded = 16
                qk, vv, la = (_pad_tokens(t, batch, seq, padded).reshape(batch * padded, -1)
                              for t in (qk, vv, la))
                o, s = _gla_rec(qk, vv, la, s0[i], seq=padded, chunk=padded, tl=padded)
                o = o.reshape(batch, padded, dv)[:, :seq].reshape(batch * seq, dv)
            else:
                o, s = _gla_rec(qk, vv, la, s0[i], seq=seq, chunk=chunk, tl=512)
            states.append(s)
            x = _gla_out(o, gt, x, W['gla_onorm'][i], W['gla_w_out'][i], g1, tm=tm)
        else:
            j = i - n_a
            q_args = (W['norm_mix'][i], sh1, sc1, W['moba_w_q'][j], W['q_norm'][j])
            x = _proj_res(attend(x, q_args, ctx), x, W['moba_w_o'][j], g1, tm=tm)
        x = _mlp(x, W['norm_mlp'][i], sh2, sc2, g2, W['mlp_w_up'][i], W['mlp_w_down'][i],
                 tm=min(tm * 2, 1024), tf=512)
    return x, jnp.stack(states), k, v


def kernel(x_prompt, x_sample, state_gla, cache_k, cache_v, page_table, c_prompt, c_sample,
           ada_w, ada_b, norm_mix, norm_mlp, gla_w_in, gla_w_gate2, gla_b_gate, gla_onorm, gla_w_out,
           kv_ada_w, kv_ada_b, kv_norm, w_kv, k_norm, moba_w_q, q_norm, moba_w_o, rel_bias,
           mlp_w_up, mlp_w_down):
    B, L, D = x_prompt.shape
    DB, LQ, _ = x_sample.shape
    depth = ada_w.shape[0]
    n_a = gla_w_in.shape[0]
    rank, dk = gla_w_gate2.shape[1:]
    dv = gla_w_out.shape[1]
    n_pool, page, H, hd = cache_k.shape
    n_pages = page_table.shape[1]
    past = n_pages * page
    assert H == ATT_HEADS and L % MOBA_BLOCK == 0 and MOBA_BLOCK % page == 0
    assert past % MOBA_BLOCK == 0 and LQ <= min(SUBLANES, MOBA_BLOCK) and rank <= LANES

    W = {
        'norm_mix': norm_mix.reshape(depth, 1, D), 'norm_mlp': norm_mlp.reshape(depth, 1, D),
        'gla_w_main': gla_w_in[:, :, :2 * dk + 2 * dv].astype(BF16),
        'gla_w_r': jnp.pad(gla_w_in[:, :, 2 * dk + 2 * dv:], ((0, 0), (0, 0), (0, LANES - rank))).astype(BF16),
        'gla_w_g2': jnp.pad(gla_w_gate2, ((0, 0), (0, LANES - rank), (0, 0))),
        'gla_b_gate': gla_b_gate.reshape(n_a, 1, dk), 'gla_onorm': gla_onorm.reshape(n_a, 1, -1),
        'gla_w_out': gla_w_out.astype(BF16),
        'moba_w_q': moba_w_q.astype(BF16), 'q_norm': q_norm.reshape(-1, 1, hd),
        'moba_w_o': moba_w_o.astype(BF16),
        'mlp_w_up': mlp_w_up.astype(BF16), 'mlp_w_down': mlp_w_down.astype(BF16),
    }
    w_kv_b = w_kv.astype(BF16)
    kv_norm2 = kv_norm.reshape(1, D)
    k_norm2 = k_norm.reshape(1, hd)

    n_c = B + DB
    c_all = jnp.pad(jnp.concatenate([c_prompt, c_sample], axis=0), ((0, -n_c % SUBLANES), (0, 0)))
    mod = _ada(c_all, ada_w, ada_b)
    kvm = _ada(c_all, kv_ada_w[None], kv_ada_b[None])[0]

    def prompt_mod(m):
        return m[:B].reshape(B, 1, D)

    def sample_mod(m):
        return jnp.repeat(m[B:n_c], LQ, axis=0).reshape(1, DB * LQ, D)

    def split_mods(pick):
        layers = [[pick(mod[i, :, t * D:(t + 1) * D]) for t in range(6)] for i in range(depth)]
        return layers, [pick(kvm[:, t * D:(t + 1) * D]) for t in range(2)]

    p_mods, p_kvmod = split_mods(prompt_mod)
    p_bias = _prompt_bias_tiles(rel_bias)

    def prompt_ctx(x, kvmod):
        k, v, ka, va, ksum = _kv(x, kv_norm2, kvmod[0], kvmod[1], w_kv_b, k_norm2, tm=512, seq=L,
                                 for_prompt=True)
        return k, v, (ka, va, ksum.reshape(B, L // MOBA_BLOCK, H * hd))

    def prompt_attend(x, q_args, ctx):
        ka, va, ksum = ctx
        qa = _q_proj_sel(x, *q_args, ksum, tm=512, seq=L)
        return _moba_prompt(qa, ka, va, p_bias, batch=B, seq=L)

    s0_p = jnp.zeros((n_a, B) + state_gla.shape[2:], F32)
    y_p, st_p, k_p, v_p = _trunk(x_prompt.reshape(B * L, D), p_mods, p_kvmod, s0_p, W, batch=B, seq=L,
                                 tm=512, attend=prompt_attend, make_ctx=prompt_ctx)

    s_mods, s_kvmod = split_mods(sample_mod)
    s_bias = _sample_bias_rows(rel_bias, page=page, lq_n=LQ)
    pt_flat = page_table.reshape(-1).astype(jnp.int32)
    rows_s = DB * LQ

    def sample_ctx(x, kvmod):
        k, v = _kv(x, kv_norm2, kvmod[0], kvmod[1], w_kv_b, k_norm2, tm=rows_s, seq=LQ, for_prompt=False)
        ksum = _sample_block_sums(cache_k, pt_flat, dec_batch=DB, n_pages=n_pages)
        return k, v, (_pad_tokens(k, DB, LQ, SUBLANES), _pad_tokens(v, DB, LQ, SUBLANES), ksum)

    def sample_attend(x, q_args, ctx):
        kn8, vn8, ksum = ctx
        q8 = _pad_tokens(_q_proj(x, *q_args, tm=rows_s), DB, LQ, SUBLANES)
        picks = _sample_select(q8, ksum)[:, :LQ, :ATT_HEADS * MOBA_TOPK]
        sel_flat = picks.reshape(DB, LQ, ATT_HEADS, MOBA_TOPK).transpose(0, 2, 1, 3).reshape(-1)
        o8 = _sample_attn(q8, kn8, vn8, s_bias, cache_k, cache_v, pt_flat, sel_flat,
                          lq_n=LQ, n_pages=n_pages)
        return o8[:, :LQ].reshape(rows_s, H * hd)

    y_s, st_s, k_s, v_s = _trunk(x_sample.reshape(rows_s, D), s_mods, s_kvmod, state_gla.astype(F32), W,
                                 batch=DB, seq=LQ, tm=rows_s, attend=sample_attend, make_ctx=sample_ctx)

    return (y_p.reshape(B, L, D), y_s.reshape(DB, LQ, D), st_p, st_s,
            k_p.reshape(B, L, H, hd), v_p.reshape(B, L, H, hd),
            k_s.reshape(DB, LQ, H, hd), v_s.reshape(DB, LQ, H, hd))
```

```python
import functools
import math

import numpy as np
import jax
import jax.numpy as jnp
from jax import lax
from jax.experimental import pallas as pl
from jax.experimental.pallas import tpu as pltpu

F32 = jnp.float32
BF16 = jnp.bfloat16

GLA_HEADS = 4
GLA_GATE_NORM = 16.0
GLA_CHUNK = 64
ATT_HEADS = 8
MOBA_BLOCK = 256
MOBA_TOPK = 3
N_BUCKETS = 32
MAX_DISTANCE = 128
EPS = 1e-6

LANES = 128
SUBLANES = 8
VMEM_LIMIT = 48 * 1024 * 1024
NEG = -1e30
HIGHEST = lax.Precision.HIGHEST
NT = (((1,), (1,)), ((), ()))
TN = (((0,), (0,)), ((), ()))


def _params(*sem):
    return pltpu.CompilerParams(dimension_semantics=sem, vmem_limit_bytes=VMEM_LIMIT)


def _prenorm(x, g, shift, scale):
    ms = jnp.mean(x * x, axis=-1, keepdims=True)
    return (x * lax.rsqrt(ms + EPS)) * g * (1.0 + scale) + shift


def _head_rms(y, g):
    return y * lax.rsqrt(jnp.mean(y * y, axis=-1, keepdims=True) + EPS) * g


def _row_tile(rows, want):
    tm = min(rows, want)
    assert rows % tm == 0
    return tm


def _mod_spec(mod, rows, tm, ngrid):
    G, R, D = mod.shape
    tiles_per_group = rows // (G * tm)
    assert tiles_per_group * G * tm == rows and R in (1, tm)
    if ngrid == 1:
        return pl.BlockSpec((1, R, D), lambda i: (i // tiles_per_group, 0, 0))
    return pl.BlockSpec((1, R, D), lambda i, k: (i // tiles_per_group, 0, 0))


def _ada_kernel(c_ref, w_ref, b_ref, o_ref):
    c = c_ref[...]
    a = (c * jax.nn.sigmoid(c)).astype(BF16)
    o_ref[...] = jnp.dot(a, w_ref[...].astype(BF16), preferred_element_type=F32) + b_ref[...]


def _ada(c, w, b, tn=1024):
    NL, D, N = w.shape
    M = c.shape[0]
    return pl.pallas_call(
        _ada_kernel,
        out_shape=jax.ShapeDtypeStruct((NL, M, N), F32),
        grid=(NL, N // tn),
        in_specs=[pl.BlockSpec((M, D), lambda l, j: (0, 0)),
                  pl.BlockSpec((None, D, tn), lambda l, j: (l, 0, j)),
                  pl.BlockSpec((None, 1, tn), lambda l, j: (l, 0, j))],
        out_specs=pl.BlockSpec((None, M, tn), lambda l, j: (l, 0, j)),
        compiler_params=_params("parallel", "parallel"),
        name="ada_modulation",
    )(c, w, b.reshape(NL, 1, N))


def _gla_in_kernel(x_ref, g_ref, sh_ref, sc_ref, w_ref, wr_ref, wg2_ref, bg_ref,
                   qk_ref, v_ref, gt_ref, la_ref, *, dk, dv, qscale):
    h = _prenorm(x_ref[...], g_ref[...], sh_ref[0], sc_ref[0]).astype(BF16)
    qk_ref[:, :dk] = jnp.dot(h, w_ref[:, :dk], preferred_element_type=F32) * qscale
    qk_ref[:, dk:] = jnp.dot(h, w_ref[:, dk:2 * dk], preferred_element_type=F32)
    v_ref[...] = jnp.dot(h, w_ref[:, 2 * dk:2 * dk + dv], preferred_element_type=F32)
    gt_ref[...] = jnp.dot(h, w_ref[:, 2 * dk + dv:], preferred_element_type=F32)
    r = jnp.dot(h, wr_ref[...], preferred_element_type=F32)
    z = jnp.dot(r, wg2_ref[...], precision=HIGHEST, preferred_element_type=F32) + bg_ref[...]
    log_sig = jnp.minimum(z, 0.0) - jnp.log1p(jnp.exp(-jnp.abs(z)))
    la_ref[...] = log_sig * (1.0 / GLA_GATE_NORM)


def _gla_in(x, g, shift, scale, w_main, w_r, w_g2, b_g, *, dk, dv, tm):
    rows, D = x.shape
    tm = _row_tile(rows, tm)
    kern = functools.partial(_gla_in_kernel, dk=dk, dv=dv, qscale=(dk // GLA_HEADS) ** -0.5)
    full = lambda a: pl.BlockSpec(a.shape, lambda i: (0,) * a.ndim)
    row = lambda n: pl.BlockSpec((tm, n), lambda i: (i, 0))
    return pl.pallas_call(
        kern,
        out_shape=(jax.ShapeDtypeStruct((rows, 2 * dk), F32), jax.ShapeDtypeStruct((rows, dv), F32),
                   jax.ShapeDtypeStruct((rows, dv), F32), jax.ShapeDtypeStruct((rows, dk), F32)),
        grid=(rows // tm,),
        in_specs=[row(D), full(g), _mod_spec(shift, rows, tm, 1), _mod_spec(scale, rows, tm, 1),
                  full(w_main), full(w_r), full(w_g2), full(b_g)],
        out_specs=(row(2 * dk), row(dv), row(dv), row(dk)),
        compiler_params=_params("parallel"),
        name="gla_in_proj",
    )(x, g, shift, scale, w_main, w_r, w_g2, b_g)


def _gla_rec_kernel(q_ref, k_ref, v_ref, la_ref, s0_ref, o_ref, s_ref, st_ref, *, chunk, n_inner):
    lt = pl.program_id(1)
    H, dvh, dkh = st_ref.shape

    @pl.when(lt == 0)
    def _():
        for hd in range(H):
            st_ref[hd] = s0_ref[0, hd].T

    r_io = lax.broadcasted_iota(jnp.int32, (chunk, chunk), 0)
    c_io = lax.broadcasted_iota(jnp.int32, (chunk, chunk), 1)
    lower = r_io >= c_io
    tri = lower.astype(F32)
    mid = chunk // 2 - 1

    def body(c, carry):
        rows = pl.ds(pl.multiple_of(c * chunk, chunk), chunk)
        b_all = jnp.dot(tri, la_ref[rows, :], precision=HIGHEST, preferred_element_type=F32)
        for hd in range(H):
            ks = slice(hd * dkh, (hd + 1) * dkh)
            vs = slice(hd * dvh, (hd + 1) * dvh)
            q = q_ref[rows, ks]
            k = k_ref[rows, ks]
            vb = v_ref[rows, vs].astype(BF16)
            b = b_all[:, ks]
            b_mid = b[mid:mid + 1, :]
            b_end = b[chunk - 1:chunk, :]
            qm = (q * jnp.exp(b - b_mid)).astype(BF16)
            km = (k * jnp.exp(b_mid - b)).astype(BF16)
            a = lax.dot_general(qm, km, NT, preferred_element_type=F32)
            a = jnp.where(lower, a, 0.0).astype(BF16)
            st = st_ref[hd]
            qd = (q * jnp.exp(b)).astype(BF16)
            o = jnp.dot(a, vb, preferred_element_type=F32)
            o = o + lax.dot_general(qd, st.astype(BF16), NT, preferred_element_type=F32)
            o_ref[rows, vs] = o
            ke = (k * jnp.exp(b_end - b)).astype(BF16)
            upd = lax.dot_general(vb, ke, TN, preferred_element_type=F32)
            st_ref[hd] = st * jnp.exp(b_end) + upd
        return carry

    lax.fori_loop(0, n_inner, body, 0)

    @pl.when(lt == pl.num_programs(1) - 1)
    def _():
        for hd in range(H):
            s_ref[0, hd] = st_ref[hd].T


def _gla_rec(qk, v, la, s0, *, seq, chunk, tl):
    B, H, dkh, dvh = s0.shape
    tl = min(tl, seq)
    assert seq % tl == 0 and tl % chunk == 0
    nlt = seq // tl
    kern = functools.partial(_gla_rec_kernel, chunk=chunk, n_inner=tl // chunk)
    return pl.pallas_call(
        kern,
        out_shape=(jax.ShapeDtypeStruct(v.shape, F32), jax.ShapeDtypeStruct(s0.shape, F32)),
        grid=(B, nlt),
        in_specs=[pl.BlockSpec((tl, H * dkh), lambda b, t: (b * nlt + t, 0)),
                  pl.BlockSpec((tl, H * dkh), lambda b, t: (b * nlt + t, 1)),
                  pl.BlockSpec((tl, H * dvh), lambda b, t: (b * nlt + t, 0)),
                  pl.BlockSpec((tl, H * dkh), lambda b, t: (b * nlt + t, 0)),
                  pl.BlockSpec((1, H, dkh, dvh), lambda b, t: (b, 0, 0, 0))],
        out_specs=(pl.BlockSpec((tl, H * dvh), lambda b, t: (b * nlt + t, 0)),
                   pl.BlockSpec((1, H, dkh, dvh), lambda b, t: (b, 0, 0, 0))),
        scratch_shapes=[pltpu.VMEM((H, dvh, dkh), F32)],
        compiler_params=_params("parallel", "arbitrary"),
        name="gla_recurrence",
    )(qk, qk, v, la, s0)


def _gla_out_kernel(o_ref, gt_ref, x_ref, on_ref, w_ref, gate_ref, out_ref, *, dvh):
    acc = jnp.zeros(out_ref.shape, F32)
    for hd in range(GLA_HEADS):
        sl = slice(hd * dvh, (hd + 1) * dvh)
        y = _head_rms(o_ref[:, sl], on_ref[...])
        gt = gt_ref[:, sl]
        y = y * (gt * jax.nn.sigmoid(gt))
        acc = acc + jnp.dot(y.astype(BF16), w_ref[sl, :], preferred_element_type=F32)
    out_ref[...] = x_ref[...] + gate_ref[0] * acc


def _gla_out(o, gt, x, onorm, w_out, gate, *, tm):
    rows, D = x.shape
    tm = _row_tile(rows, tm)
    row = lambda n: pl.BlockSpec((tm, n), lambda i: (i, 0))
    full = lambda a: pl.BlockSpec(a.shape, lambda i: (0,) * a.ndim)
    kern = functools.partial(_gla_out_kernel, dvh=o.shape[1] // GLA_HEADS)
    return pl.pallas_call(
        kern,
        out_shape=jax.ShapeDtypeStruct((rows, D), F32),
        grid=(rows // tm,),
        in_specs=[row(o.shape[1]), row(gt.shape[1]), row(D), full(onorm), full(w_out),
                  _mod_spec(gate, rows, tm, 1)],
        out_specs=row(D),
        compiler_params=_params("parallel"),
        name="gla_out_proj",
    )(o, gt, x, onorm, w_out, gate)


def _mlp_kernel(x_ref, g_ref, sh_ref, sc_ref, gate_ref, wu_ref, wd_ref, out_ref, h_ref):
    k = pl.program_id(1)

    @pl.when(k == 0)
    def _():
        h_ref[...] = _prenorm(x_ref[...], g_ref[...], sh_ref[0], sc_ref[0]).astype(BF16)
        out_ref[...] = jnp.zeros_like(out_ref)

    u = jnp.dot(h_ref[...], wu_ref[...], preferred_element_type=F32)
    u = jnp.square(jnp.maximum(u, 0.0)).astype(BF16)
    out_ref[...] += jnp.dot(u, wd_ref[...], preferred_element_type=F32)

    @pl.when(k == pl.num_programs(1) - 1)
    def _():
        out_ref[...] = x_ref[...] + gate_ref[0] * out_ref[...]


def _mlp(x, g, shift, scale, gate, w_up, w_down, *, tm, tf):
    rows, D = x.shape
    FF = w_up.shape[1]
    tm = _row_tile(rows, tm)
    return pl.pallas_call(
        _mlp_kernel,
        out_shape=jax.ShapeDtypeStruct((rows, D), F32),
        grid=(rows // tm, FF // tf),
        in_specs=[pl.BlockSpec((tm, D), lambda i, k: (i, 0)),
                  pl.BlockSpec(g.shape, lambda i, k: (0, 0)),
                  _mod_spec(shift, rows, tm, 2), _mod_spec(scale, rows, tm, 2),
                  _mod_spec(gate, rows, tm, 2),
                  pl.BlockSpec((D, tf), lambda i, k: (0, k)),
                  pl.BlockSpec((tf, D), lambda i, k: (k, 0))],
        out_specs=pl.BlockSpec((tm, D), lambda i, k: (i, 0)),
        scratch_shapes=[pltpu.VMEM((tm, D), BF16)],
        compiler_params=_params("parallel", "arbitrary"),
        name="mlp",
    )(x, g, shift, scale, gate, w_up, w_down)


def _kv_kernel(x_ref, g_ref, sh_ref, sc_ref, w_ref, kn_ref, k_ref, v_ref, *rest, hd, nsum, seq):
    h = _prenorm(x_ref[...], g_ref[...], sh_ref[0], sc_ref[0]).astype(BF16)
    n = k_ref.shape[1]
    tm = k_ref.shape[0]
    kk = jnp.dot(h, w_ref[:, :n], preferred_element_type=F32)
    vv = jnp.dot(h, w_ref[:, n:], preferred_element_type=F32)
    v_ref[...] = vv
    if nsum:
        ka_ref, va_ref, ks_ref = rest
        pos = (pl.program_id(0) * tm) % seq + lax.broadcasted_iota(jnp.int32, (tm, hd), 0)
        lane = lax.broadcasted_iota(jnp.int32, (tm, hd), 1)
        block_onehot = jnp.where(lane == pos // MOBA_BLOCK, 1.0, 0.0).astype(BF16)
        ones = jnp.ones((tm, hd), BF16)
    for a in range(n // hd):
        sl = slice(a * hd, (a + 1) * hd)
        kh = _head_rms(kk[:, sl], kn_ref[...])
        k_ref[:, sl] = kh
        if nsum:
            ka_ref[:, 2 * a * hd:(2 * a + 1) * hd] = kh.astype(BF16)
            ka_ref[:, (2 * a + 1) * hd:(2 * a + 2) * hd] = block_onehot
            va_ref[:, 2 * a * hd:(2 * a + 1) * hd] = vv[:, sl].astype(BF16)
            va_ref[:, (2 * a + 1) * hd:(2 * a + 2) * hd] = ones
            for r in range(nsum):
                ks_ref[r, :, sl] = jnp.sum(kh[r * MOBA_BLOCK:(r + 1) * MOBA_BLOCK], axis=0, keepdims=True)


def _kv(x, g, shift, scale, w_kv, k_norm, *, tm, seq, for_prompt):
    rows, D = x.shape
    n = w_kv.shape[1] // 2
    hd = k_norm.shape[1]
    tm = _row_tile(rows, tm)
    nsum = tm // MOBA_BLOCK if for_prompt else 0
    assert not for_prompt or (tm % MOBA_BLOCK == 0 and seq // MOBA_BLOCK <= hd)
    row = lambda m: pl.BlockSpec((tm, m), lambda i: (i, 0))
    full = lambda a: pl.BlockSpec(a.shape, lambda i: (0,) * a.ndim)
    out_shape = [jax.ShapeDtypeStruct((rows, n), F32), jax.ShapeDtypeStruct((rows, n), F32)]
    out_specs = [row(n), row(n)]
    if for_prompt:
        out_shape += [jax.ShapeDtypeStruct((rows, 2 * n), BF16), jax.ShapeDtypeStruct((rows, 2 * n), BF16),
                      jax.ShapeDtypeStruct((rows // MOBA_BLOCK, 1, n), F32)]
        out_specs += [row(2 * n), row(2 * n), pl.BlockSpec((nsum, 1, n), lambda i: (i, 0, 0))]
    return pl.pallas_call(
        functools.partial(_kv_kernel, hd=hd, nsum=nsum, seq=seq),
        out_shape=tuple(out_shape),
        grid=(rows // tm,),
        in_specs=[row(D), full(g), _mod_spec(shift, rows, tm, 1), _mod_spec(scale, rows, tm, 1),
                  full(w_kv), full(k_norm)],
        out_specs=tuple(out_specs),
        compiler_params=_params("parallel"),
        name="shared_kv",
    )(x, g, shift, scale, w_kv, k_norm)


def _q_kernel(x_ref, g_ref, sh_ref, sc_ref, w_ref, qn_ref, q_ref, *, hd):
    h = _prenorm(x_ref[...], g_ref[...], sh_ref[0], sc_ref[0]).astype(BF16)
    qq = jnp.dot(h, w_ref[...], preferred_element_type=F32)
    for a in range(q_ref.shape[1] // hd):
        sl = slice(a * hd, (a + 1) * hd)
        q_ref[:, sl] = _head_rms(qq[:, sl], qn_ref[...])


def _q_proj(x, g, shift, scale, w_q, q_norm, *, tm):
    rows, D = x.shape
    n = w_q.shape[1]
    tm = _row_tile(rows, tm)
    row = lambda m: pl.BlockSpec((tm, m), lambda i: (i, 0))
    full = lambda a: pl.BlockSpec(a.shape, lambda i: (0,) * a.ndim)
    return pl.pallas_call(
        functools.partial(_q_kernel, hd=q_norm.shape[1]),
        out_shape=jax.ShapeDtypeStruct((rows, n), F32),
        grid=(rows // tm,),
        in_specs=[row(D), full(g), _mod_spec(shift, rows, tm, 1), _mod_spec(scale, rows, tm, 1),
                  full(w_q), full(q_norm)],
        out_specs=row(n),
        compiler_params=_params("parallel"),
        name="moba_q_proj",
    )(x, g, shift, scale, w_q, q_norm)


def _top_blocks(scores, n_past, axis):
    nbk = scores.shape[axis]
    blk_id = lax.broadcasted_iota(jnp.int32, scores.shape, axis)
    sc = jnp.where(blk_id < n_past, scores, -jnp.inf)
    picks = []
    for r in range(MOBA_TOPK):
        m = jnp.max(sc, axis=axis, keepdims=True)
        idx = jnp.min(jnp.where(sc == m, blk_id, nbk), axis=axis, keepdims=True)
        picks.append((idx, (jnp.zeros_like(idx) + r) < n_past))
        sc = jnp.where(blk_id == idx, -jnp.inf, sc)
    return picks


def _q_sel_kernel(x_ref, g_ref, sh_ref, sc_ref, w_ref, qn_ref, ks_ref, qa_ref, *, hd, seq, scale):
    tm = x_ref.shape[0]
    nbk = ks_ref.shape[1]
    h = _prenorm(x_ref[...], g_ref[...], sh_ref[0], sc_ref[0]).astype(BF16)
    qq = jnp.dot(h, w_ref[...], preferred_element_type=F32)
    pos = (pl.program_id(0) * tm) % seq + lax.broadcasted_iota(jnp.int32, (1, tm), 1)
    own = pos // MOBA_BLOCK
    blk_id = lax.broadcasted_iota(jnp.int32, (nbk, tm), 0)
    for a in range(ATT_HEADS):
        sl = slice(a * hd, (a + 1) * hd)
        qh = _head_rms(qq[:, sl], qn_ref[...])
        scores = lax.dot_general(ks_ref[0, :, sl] * (1.0 / MOBA_BLOCK), qh, NT, precision=HIGHEST,
                                 preferred_element_type=F32)
        chosen = blk_id < 0
        for idx, ok in _top_blocks(scores, own, 0):
            chosen = chosen | ((blk_id == idx) & ok)
        is_prev = blk_id == own - 1
        far = jnp.where(chosen & jnp.logical_not(is_prev), 0.0, NEG)
        prev = jnp.max(jnp.where(chosen & is_prev, 0.0, NEG), axis=0, keepdims=True)
        mask_t = jnp.concatenate([far, jnp.zeros((hd - nbk - SUBLANES, tm), F32),
                                  jnp.broadcast_to(prev, (SUBLANES, tm))], axis=0)
        qa_ref[:, 2 * a * hd:(2 * a + 1) * hd] = (qh * scale).astype(BF16)
        qa_ref[:, (2 * a + 1) * hd:(2 * a + 2) * hd] = mask_t.T.astype(BF16)


def _q_proj_sel(x, g, shift, scale, w_q, q_norm, ksum, *, tm, seq):
    rows, D = x.shape
    n = w_q.shape[1]
    hd = q_norm.shape[1]
    B, nbk, _ = ksum.shape
    tm = _row_tile(rows, tm)
    tiles_per_seq = seq // tm
    assert seq % tm == 0 and tm % LANES == 0 and nbk % SUBLANES == 0 and nbk + SUBLANES <= hd
    row = lambda m: pl.BlockSpec((tm, m), lambda i: (i, 0))
    full = lambda a: pl.BlockSpec(a.shape, lambda i: (0,) * a.ndim)
    return pl.pallas_call(
        functools.partial(_q_sel_kernel, hd=hd, seq=seq, scale=hd ** -0.5),
        out_shape=jax.ShapeDtypeStruct((rows, 2 * n), BF16),
        grid=(rows // tm,),
        in_specs=[row(D), full(g), _mod_spec(shift, rows, tm, 1), _mod_spec(scale, rows, tm, 1),
                  full(w_q), full(q_norm),
                  pl.BlockSpec((1, nbk, n), lambda i: (i // tiles_per_seq, 0, 0))],
        out_specs=row(2 * n),
        compiler_params=_params("parallel"),
        name="moba_q_proj_select",
    )(x, g, shift, scale, w_q, q_norm, ksum)


def _proj_res_kernel(a_ref, x_ref, w_ref, gate_ref, out_ref):
    y = jnp.dot(a_ref[...].astype(BF16), w_ref[...], preferred_element_type=F32)
    out_ref[...] = x_ref[...] + gate_ref[0] * y


def _proj_res(a, x, w, gate, *, tm):
    rows, D = x.shape
    tm = _row_tile(rows, tm)
    row = lambda m: pl.BlockSpec((tm, m), lambda i: (i, 0))
    return pl.pallas_call(
        _proj_res_kernel,
        out_shape=jax.ShapeDtypeStruct((rows, D), F32),
        grid=(rows // tm,),
        in_specs=[row(a.shape[1]), row(D), pl.BlockSpec(w.shape, lambda i: (0, 0)),
                  _mod_spec(gate, rows, tm, 1)],
        out_specs=row(D),
        compiler_params=_params("parallel"),
        name="moba_out_proj",
    )(a, x, w, gate)


def _moba_prompt_kernel(qa_ref, ka_ref, va_ref, bias_ref, o_ref, m_ref, acc_ref,
                        sa_ref, sb_ref, pa_ref, pb_ref, al_ref, pn_ref, m0_ref,
                        *, blk, group, n_groups, halves):
    it = pl.program_id(2)
    hd = qa_ref.shape[1] // 2
    last_group = n_groups - 1

    def rows_of(g):
        return pl.ds(pl.multiple_of(g * (group * blk), group * blk), group * blk)

    def scores_of(g):
        return lax.dot_general(qa_ref[...], ka_ref[rows_of(g), :], NT, preferred_element_type=F32)

    sa_ref[...] = scores_of(0)

    t_io = lax.broadcasted_iota(jnp.int32, (blk, blk), 0)
    s_io = lax.broadcasted_iota(jnp.int32, (blk, blk), 1)

    def near_rows(hf):
        own = it * halves + hf
        return (pl.ds(pl.multiple_of(jnp.maximum(own - 1, 0) * blk, blk), blk),
                pl.ds(pl.multiple_of(own * blk, blk), blk))

    for hf in range(halves):
        rq = slice(hf * blk, (hf + 1) * blk)
        r_prev, r_own = near_rows(hf)
        qb = qa_ref[rq, :hd]
        prev_mask = qa_ref[rq, 2 * hd - 1:].astype(F32)
        s_own = lax.dot_general(qb, ka_ref[r_own, :hd], NT, preferred_element_type=F32)
        s_own = jnp.where(s_io <= t_io, s_own + bias_ref[0, 0], NEG)
        s_prev = lax.dot_general(qb, ka_ref[r_prev, :hd], NT, preferred_element_type=F32)
        s_prev = s_prev + bias_ref[0, 1] + prev_mask
        m0 = jnp.maximum(jnp.max(s_own, axis=1, keepdims=True), jnp.max(s_prev, axis=1, keepdims=True))
        pn_ref[rq, :blk] = jnp.exp(s_prev - m0).astype(BF16)
        pn_ref[rq, blk:] = jnp.exp(s_own - m0).astype(BF16)
        m_ref[rq, :] = m0
        m0_ref[rq, :] = m0
    acc_ref[...] = jnp.zeros_like(acc_ref)

    def softmax(s_ref, p_ref, slot):
        m_prev = m_ref[...]
        s = s_ref[...]
        m_new = jnp.maximum(m_prev, jnp.max(s, axis=1, keepdims=True))
        p_ref[...] = jnp.exp(s - m_new).astype(BF16)
        al_ref[slot] = jnp.exp(m_prev - m_new)
        m_ref[...] = m_new

    def flush(p_ref, slot, g):
        acc_ref[...] = al_ref[slot] * acc_ref[...] + jnp.dot(p_ref[...], va_ref[rows_of(g), :],
                                                             preferred_element_type=F32)

    n_far = it * halves + halves - 2
    n_pairs = ((n_far + group - 1) // group + 1) // 2
    pb_ref[...] = jnp.zeros_like(pb_ref)
    al_ref[1] = jnp.ones(al_ref.shape[1:], F32)

    def body(t, carry):
        sb_ref[...] = scores_of(2 * t + 1)
        flush(pb_ref, 1, jnp.maximum(2 * t - 1, 0))
        softmax(sa_ref, pa_ref, 0)
        sa_ref[...] = scores_of(jnp.minimum(2 * t + 2, last_group))
        flush(pa_ref, 0, 2 * t)
        softmax(sb_ref, pb_ref, 1)
        return carry

    lax.fori_loop(0, n_pairs, body, 0)

    flush(pb_ref, 1, jnp.maximum(2 * n_pairs - 1, 0))
    for hf in range(halves):
        rq = slice(hf * blk, (hf + 1) * blk)
        r_prev, r_own = near_rows(hf)
        near = jnp.dot(pn_ref[rq, :blk], va_ref[r_prev, :], preferred_element_type=F32)
        near = near + jnp.dot(pn_ref[rq, blk:], va_ref[r_own, :], preferred_element_type=F32)
        acc = acc_ref[rq, :] + jnp.exp(m0_ref[rq, :] - m_ref[rq, :]) * near
        o_ref[rq, :] = (acc[:, :hd] / acc[:, hd:]).astype(o_ref.dtype)


def _moba_prompt(qa, ka, va, bias, *, batch, seq, group=4, halves=2):
    rows, n2 = qa.shape
    hd = n2 // (2 * ATT_HEADS)
    blk = MOBA_BLOCK
    nq = seq // blk
    assert nq % (2 * group) == 0 and group >= 2 and nq % halves == 0
    nt = nq // halves
    tq = halves * blk
    kern = functools.partial(_moba_prompt_kernel, blk=blk, group=group, n_groups=nq // group, halves=halves)
    return pl.pallas_call(
        kern,
        out_shape=jax.ShapeDtypeStruct((rows, ATT_HEADS * hd), BF16),
        grid=(batch, ATT_HEADS, nt),
        in_specs=[pl.BlockSpec((tq, 2 * hd), lambda b, h, i: (b * nt + i, h)),
                  pl.BlockSpec((seq, 2 * hd), lambda b, h, i: (b, h)),
                  pl.BlockSpec((seq, 2 * hd), lambda b, h, i: (b, h)),
                  pl.BlockSpec((1, 2, blk, blk), lambda b, h, i: (h, 0, 0, 0))],
        out_specs=pl.BlockSpec((tq, hd), lambda b, h, i: (b * nt + i, h)),
        scratch_shapes=[pltpu.VMEM((tq, 1), F32), pltpu.VMEM((tq, 2 * hd), F32),
                        pltpu.VMEM((tq, group * blk), F32), pltpu.VMEM((tq, group * blk), F32),
                        pltpu.VMEM((tq, group * blk), BF16), pltpu.VMEM((tq, group * blk), BF16),
                        pltpu.VMEM((2, tq, 1), F32), pltpu.VMEM((tq, 2 * blk), BF16),
                        pltpu.VMEM((tq, 1), F32)],
        compiler_params=_params("parallel", "parallel", "arbitrary"),
        name="moba_prompt_attention",
    )(qa, ka, va, bias)


def _page_sum_kernel(pt_ref, *refs, ppb):
    o_ref = refs[-1]
    pages = refs[:-1]
    for r in range(len(pages) // ppb):
        total = jnp.sum(pages[r * ppb][0], axis=0)
        for p in range(1, ppb):
            total = total + jnp.sum(pages[r * ppb + p][0], axis=0)
        o_ref[0, r] = total


def _sample_block_sums(cache_k, pt_flat, *, dec_batch, n_pages, blocks_per_step=4):
    _, page, H, hd = cache_k.shape
    ppb = MOBA_BLOCK // page
    nbk = n_pages // ppb
    bps = math.gcd(blocks_per_step, nbk)
    nper = bps * ppb

    def page_spec(p):
        return pl.BlockSpec((1, page, H, hd), lambda d, j, pt: (pt[d * n_pages + j * nper + p], 0, 0, 0))

    return pl.pallas_call(
        functools.partial(_page_sum_kernel, ppb=ppb),
        out_shape=jax.ShapeDtypeStruct((dec_batch, nbk, H, hd), F32),
        grid_spec=pltpu.PrefetchScalarGridSpec(
            num_scalar_prefetch=1,
            grid=(dec_batch, nbk // bps),
            in_specs=[page_spec(p) for p in range(nper)],
            out_specs=pl.BlockSpec((1, bps, H, hd), lambda d, j, pt: (d, j, 0, 0))),
        compiler_params=_params("parallel", "arbitrary"),
        name="sample_block_sums",
    )(pt_flat, *([cache_k] * nper))


def _sample_select_kernel(q_ref, ks_ref, idx_ref, *, hd, blk):
    nbk = ks_ref.shape[1]
    lane = lax.broadcasted_iota(jnp.int32, idx_ref.shape[1:], 1)
    out = jnp.zeros(idx_ref.shape[1:], jnp.int32)
    for a in range(ATT_HEADS):
        means = ks_ref[0, :, a, :] * (1.0 / blk)
        scores = lax.dot_general(q_ref[0, :, a * hd:(a + 1) * hd], means, NT, precision=HIGHEST,
                                 preferred_element_type=F32)
        for r, (idx, _) in enumerate(_top_blocks(scores, nbk, 1)):
            out = jnp.where(lane == a * MOBA_TOPK + r, idx, out)
    idx_ref[0] = out


def _sample_select(q8, ksum):
    DB, R, n = q8.shape
    _, nbk, H, hd = ksum.shape
    assert nbk >= MOBA_TOPK and H == ATT_HEADS
    kern = functools.partial(_sample_select_kernel, hd=hd, blk=MOBA_BLOCK)
    return pl.pallas_call(
        kern,
        out_shape=jax.ShapeDtypeStruct((DB, R, LANES), jnp.int32),
        grid=(DB,),
        in_specs=[pl.BlockSpec((1, R, n), lambda d: (d, 0, 0)),
                  pl.BlockSpec((1, nbk, H, hd), lambda d: (d, 0, 0, 0))],
        out_specs=pl.BlockSpec((1, R, LANES), lambda d: (d, 0, 0)),
        compiler_params=_params("parallel"),
        name="sample_select",
    )(q8, ksum)


def _sample_attn_kernel(pt_ref, sel_ref, q_ref, kn_ref, vn_ref, bias_ref, ck_ref, cv_ref, o_ref,
                        kbuf, vbuf, sem, *, lq_n, ppb, n_pages, page, scale):
    g = pl.program_id(0)
    nslot = lq_n * MOBA_TOPK
    nsl = nslot * ppb

    def copies(step, slot):
        d = step // ATT_HEADS
        a = step % ATT_HEADS
        out = []
        for j in range(nsl):
            phys = pt_ref[d * n_pages + sel_ref[step * nslot + j // ppb] * ppb + j % ppb]
            rows = pl.ds(j * page, page)
            out.append(pltpu.make_async_copy(ck_ref.at[phys, :, a, :], kbuf.at[slot, rows, :], sem.at[slot, 0, j]))
            out.append(pltpu.make_async_copy(cv_ref.at[phys, :, a, :], vbuf.at[slot, rows, :], sem.at[slot, 1, j]))
        return out

    @pl.when(g == 0)
    def _():
        for c in copies(0, 0):
            c.start()

    @pl.when(g + 1 < pl.num_programs(0))
    def _():
        for c in copies(g + 1, (g + 1) % 2):
            c.start()

    slot = g % 2
    q = q_ref[0] * scale
    row = lax.broadcasted_iota(jnp.int32, (q.shape[0], 1), 0)

    new_logits = []
    for s in range(lq_n):
        sc = jnp.sum(q * kn_ref[0, s:s + 1, :], axis=1, keepdims=True) + bias_ref[0, 8:16, s:s + 1]
        new_logits.append(jnp.where(s <= row, sc, NEG))

    for c in copies(g, slot):
        c.wait()

    s_all = lax.dot_general(q.astype(BF16), kbuf[slot].astype(BF16), NT, preferred_element_type=F32)
    parts = []
    for j in range(nsl):
        lp = sel_ref[g * nslot + j // ppb] * ppb + j % ppb
        is_last = (jnp.zeros((q.shape[0], page), jnp.int32) + lp) == n_pages - 1
        sj = s_all[:, j * page:(j + 1) * page] + jnp.where(is_last, bias_ref[0, 0:8, :], 0.0)
        parts.append(jnp.where(row == j // (MOBA_TOPK * ppb), sj, NEG))
    m = jnp.max(functools.reduce(jnp.maximum, parts), axis=1, keepdims=True)
    m = functools.reduce(jnp.maximum, new_logits, m)
    probs = [jnp.exp(p - m) for p in parts]
    den = jnp.sum(functools.reduce(jnp.add, probs), axis=1, keepdims=True)
    pcat = jnp.concatenate([p.astype(BF16) for p in probs], axis=1)
    acc = jnp.dot(pcat, vbuf[slot].astype(BF16), preferred_element_type=F32)
    for s in range(lq_n):
        e = jnp.exp(new_logits[s] - m)
        den = den + e
        acc = acc + e * vn_ref[0, s:s + 1, :]
    o_ref[0] = acc / den


def _sample_attn(q8, kn8, vn8, bias, cache_k, cache_v, pt_flat, sel_flat, *, lq_n, n_pages):
    DB, R, n = q8.shape
    _, page, H, hd = cache_k.shape
    ppb = MOBA_BLOCK // page
    nsl = lq_n * MOBA_TOPK * ppb
    assert page == LANES and R == SUBLANES and H == ATT_HEADS
    kern = functools.partial(_sample_attn_kernel, lq_n=lq_n, ppb=ppb, n_pages=n_pages, page=page,
                             scale=hd ** -0.5)
    head = pl.BlockSpec((1, R, hd), lambda g, pt, sel: (g // ATT_HEADS, 0, g % ATT_HEADS))
    return pl.pallas_call(
        kern,
        out_shape=jax.ShapeDtypeStruct((DB, R, n), F32),
        grid_spec=pltpu.PrefetchScalarGridSpec(
            num_scalar_prefetch=2,
            grid=(DB * ATT_HEADS,),
            in_specs=[head, head, head,
                      pl.BlockSpec((1, 2 * SUBLANES, LANES), lambda g, pt, sel: (g % ATT_HEADS, 0, 0)),
                      pl.BlockSpec(memory_space=pl.ANY), pl.BlockSpec(memory_space=pl.ANY)],
            out_specs=head,
            scratch_shapes=[pltpu.VMEM((2, nsl * page, hd), F32), pltpu.VMEM((2, nsl * page, hd), F32),
                            pltpu.SemaphoreType.DMA((2, 2, nsl))]),
        compiler_params=_params("arbitrary"),
        name="sample_attention",
    )(pt_flat, sel_flat, q8, kn8, vn8, bias, cache_k, cache_v)


def _bucket_of_distance():
    n = np.arange(MAX_DISTANCE)
    max_exact = N_BUCKETS // 2
    nf = np.maximum(n, max_exact).astype(np.float32)
    large = max_exact + (np.log(nf / np.float32(max_exact)) / np.float32(math.log(MAX_DISTANCE / max_exact))
                         * np.float32(N_BUCKETS - max_exact)).astype(np.int32)
    return np.where(n < max_exact, n, np.minimum(large, N_BUCKETS - 1)).astype(np.int32)


def _dist_table(rel_bias, width):
    t = rel_bias.astype(F32)[_bucket_of_distance()].T
    t = t - t[:, MAX_DISTANCE - 1:]
    return jnp.pad(t, ((0, 0), (0, width - MAX_DISTANCE)))


def _toeplitz(first_row_wrapped, n):
    H = first_row_wrapped.shape[0]
    x = jnp.broadcast_to(first_row_wrapped[:, None, :], (H, n, 2 * n)).reshape(H, 2 * n * n)
    return x[:, :n * (2 * n - 1)].reshape(H, n, 2 * n - 1)[:, :, :n]


def _prompt_bias_tiles(rel_bias):
    blk = MOBA_BLOCK
    assert blk >= MAX_DISTANCE
    t = _dist_table(rel_bias, blk)
    rev = jnp.flip(t[:, 1:], axis=1)
    zeros = jnp.zeros_like(t)
    own = _toeplitz(jnp.concatenate([t[:, :1], zeros, rev], axis=1), blk)
    prev = _toeplitz(jnp.concatenate([zeros[:, :1], rev, zeros], axis=1), blk)
    return jnp.stack([own, prev], axis=1)


def _sample_bias_rows(rel_bias, *, page, lq_n):
    assert page == LANES and page >= MAX_DISTANCE and lq_n <= SUBLANES
    t = _dist_table(rel_bias, 2 * page)
    r = np.arange(SUBLANES)[:, None]
    c = np.arange(LANES)[None, :]
    return jnp.concatenate([t[:, page + r - c], t[:, np.clip(r - c, 0, None)]], axis=1)


def _pad_tokens(a, groups, per, to):
    return jnp.pad(a.reshape(groups, per, a.shape[-1]), ((0, 0), (0, to - per), (0, 0)))


def _trunk(x, mods, kvmod, s0, W, *, batch, seq, tm, attend, make_ctx):
    n_a = W['gla_w_main'].shape[0]
    depth = W['mlp_w_up'].shape[0]
    dk = W['gla_w_g2'].shape[2]
    dv = W['gla_w_out'].shape[1]
    states = []
    k = v = ctx = None
    for i in range(depth):
        sh1, sc1, g1, sh2, sc2, g2 = mods[i]
        if i == n_a:
            k, v, ctx = make_ctx(x, kvmod)
        if i < n_a:
            qk, vv, gt, la = _gla_in(x, W['norm_mix'][i], sh1, sc1, W['gla_w_main'][i], W['gla_w_r'][i],
                                     W['gla_w_g2'][i], W['gla_b_gate'][i], dk=dk, dv=dv, tm=tm)
            chunk = math.gcd(GLA_CHUNK, seq)
            if chunk < 16:
                padded = 16
                qk, vv, la = (_pad_tokens(t, batch, seq, padded).reshape(batch * padded, -1)
                              for t in (qk, vv, la))
                o, s = _gla_rec(qk, vv, la, s0[i], seq=padded, chunk=padded, tl=padded)
                o = o.reshape(batch, padded, dv)[:, :seq].reshape(batch * seq, dv)
            else:
                o, s = _gla_rec(qk, vv, la, s0[i], seq=seq, chunk=chunk, tl=512)
            states.append(s)
            x = _gla_out(o, gt, x, W['gla_onorm'][i], W['gla_w_out'][i], g1, tm=tm)
        else:
            j = i - n_a
            q_args = (W['norm_mix'][i], sh1, sc1, W['moba_w_q'][j], W['q_norm'][j])
            x = _proj_res(attend(x, q_args, ctx), x, W['moba_w_o'][j], g1, tm=tm)
        x = _mlp(x, W['norm_mlp'][i], sh2, sc2, g2, W['mlp_w_up'][i], W['mlp_w_down'][i],
                 tm=min(tm * 2, 1024), tf=1024)
    return x, jnp.stack(states), k, v


def kernel(x_prompt, x_sample, state_gla, cache_k, cache_v, page_table, c_prompt, c_sample,
           ada_w, ada_b, norm_mix, norm_mlp, gla_w_in, gla_w_gate2, gla_b_gate, gla_onorm, gla_w_out,
           kv_ada_w, kv_ada_b, kv_norm, w_kv, k_norm, moba_w_q, q_norm, moba_w_o, rel_bias,
           mlp_w_up, mlp_w_down):
    B, L, D = x_prompt.shape
    DB, LQ, _ = x_sample.shape
    depth = ada_w.shape[0]
    n_a = gla_w_in.shape[0]
    rank, dk = gla_w_gate2.shape[1:]
    dv = gla_w_out.shape[1]
    n_pool, page, H, hd = cache_k.shape
    n_pages = page_table.shape[1]
    past = n_pages * page
    assert H == ATT_HEADS and L % MOBA_BLOCK == 0 and MOBA_BLOCK % page == 0
    assert past % MOBA_BLOCK == 0 and LQ <= min(SUBLANES, MOBA_BLOCK) and rank <= LANES

    W = {
        'norm_mix': norm_mix.reshape(depth, 1, D), 'norm_mlp': norm_mlp.reshape(depth, 1, D),
        'gla_w_main': gla_w_in[:, :, :2 * dk + 2 * dv].astype(BF16),
        'gla_w_r': jnp.pad(gla_w_in[:, :, 2 * dk + 2 * dv:], ((0, 0), (0, 0), (0, LANES - rank))).astype(BF16),
        'gla_w_g2': jnp.pad(gla_w_gate2, ((0, 0), (0, LANES - rank), (0, 0))),
        'gla_b_gate': gla_b_gate.reshape(n_a, 1, dk), 'gla_onorm': gla_onorm.reshape(n_a, 1, -1),
        'gla_w_out': gla_w_out.astype(BF16),
        'moba_w_q': moba_w_q.astype(BF16), 'q_norm': q_norm.reshape(-1, 1, hd),
        'moba_w_o': moba_w_o.astype(BF16),
        'mlp_w_up': mlp_w_up.astype(BF16), 'mlp_w_down': mlp_w_down.astype(BF16),
    }
    w_kv_b = w_kv.astype(BF16)
    kv_norm2 = kv_norm.reshape(1, D)
    k_norm2 = k_norm.reshape(1, hd)

    n_c = B + DB
    c_all = jnp.pad(jnp.concatenate([c_prompt, c_sample], axis=0), ((0, -n_c % SUBLANES), (0, 0)))
    mod = _ada(c_all, ada_w, ada_b)
    kvm = _ada(c_all, kv_ada_w[None], kv_ada_b[None])[0]

    def prompt_mod(m):
        return m[:B].reshape(B, 1, D)

    def sample_mod(m):
        return jnp.repeat(m[B:n_c], LQ, axis=0).reshape(1, DB * LQ, D)

    def split_mods(pick):
        layers = [[pick(mod[i, :, t * D:(t + 1) * D]) for t in range(6)] for i in range(depth)]
        return layers, [pick(kvm[:, t * D:(t + 1) * D]) for t in range(2)]

    p_mods, p_kvmod = split_mods(prompt_mod)
    p_bias = _prompt_bias_tiles(rel_bias)

    def prompt_ctx(x, kvmod):
        k, v, ka, va, ksum = _kv(x, kv_norm2, kvmod[0], kvmod[1], w_kv_b, k_norm2, tm=512, seq=L,
                                 for_prompt=True)
        return k, v, (ka, va, ksum.reshape(B, L // MOBA_BLOCK, H * hd))

    def prompt_attend(x, q_args, ctx):
        ka, va, ksum = ctx
        qa = _q_proj_sel(x, *q_args, ksum, tm=512, seq=L)
        return _moba_prompt(qa, ka, va, p_bias, batch=B, seq=L)

    s0_p = jnp.zeros((n_a, B) + state_gla.shape[2:], F32)
    y_p, st_p, k_p, v_p = _trunk(x_prompt.reshape(B * L, D), p_mods, p_kvmod, s0_p, W, batch=B, seq=L,
                                 tm=512, attend=prompt_attend, make_ctx=prompt_ctx)

    s_mods, s_kvmod = split_mods(sample_mod)
    s_bias = _sample_bias_rows(rel_bias, page=page, lq_n=LQ)
    pt_flat = page_table.reshape(-1).astype(jnp.int32)
    rows_s = DB * LQ

    def sample_ctx(x, kvmod):
        k, v = _kv(x, kv_norm2, kvmod[0], kvmod[1], w_kv_b, k_norm2, tm=rows_s, seq=LQ, for_prompt=False)
        ksum = _sample_block_sums(cache_k, pt_flat, dec_batch=DB, n_pages=n_pages)
        return k, v, (_pad_tokens(k, DB, LQ, SUBLANES), _pad_tokens(v, DB, LQ, SUBLANES), ksum)

    def sample_attend(x, q_args, ctx):
        kn8, vn8, ksum = ctx
        q8 = _pad_tokens(_q_proj(x, *q_args, tm=rows_s), DB, LQ, SUBLANES)
        picks = _sample_select(q8, ksum)[:, :LQ, :ATT_HEADS * MOBA_TOPK]
        sel_flat = picks.reshape(DB, LQ, ATT_HEADS, MOBA_TOPK).transpose(0, 2, 1, 3).reshape(-1)
        o8 = _sample_attn(q8, kn8, vn8, s_bias, cache_k, cache_v, pt_flat, sel_flat,
                          lq_n=LQ, n_pages=n_pages)
        return o8[:, :LQ].reshape(rows_s, H * hd)

    y_s, st_s, k_s, v_s = _trunk(x_sample.reshape(rows_s, D), s_mods, s_kvmod, state_gla.astype(F32), W,
                                 batch=DB, seq=LQ, tm=rows_s, attend=sample_attend, make_ctx=sample_ctx)

    return (y_p.reshape(B, L, D), y_s.reshape(DB, LQ, D), st_p, st_s,
            k_p.reshape(B, L, H, hd), v_p.reshape(B, L, H, hd),
            k_s.reshape(DB, LQ, H, hd), v_s.reshape(DB, LQ, H, hd))
```

```python
import functools
import math

import numpy as np
import jax
import jax.numpy as jnp
from jax import lax
from jax.experimental import pallas as pl
from jax.experimental.pallas import tpu as pltpu

F32 = jnp.float32
BF16 = jnp.bfloat16

GLA_HEADS = 4
GLA_GATE_NORM = 16.0
GLA_CHUNK = 64
ATT_HEADS = 8
MOBA_BLOCK = 256
MOBA_TOPK = 3
N_BUCKETS = 32
MAX_DISTANCE = 128
EPS = 1e-6

LANES = 128
SUBLANES = 8
VMEM_LIMIT = 48 * 1024 * 1024
MIX_MLP_VMEM_LIMIT = 56 * 1024 * 1024
NEG = -1e30
HIGHEST = lax.Precision.HIGHEST
NT = (((1,), (1,)), ((), ()))
TN = (((0,), (0,)), ((), ()))


def _params(*sem):
    return pltpu.CompilerParams(dimension_semantics=sem, vmem_limit_bytes=VMEM_LIMIT)


def _prenorm(x, g, shift, scale):
    ms = jnp.mean(x * x, axis=-1, keepdims=True)
    return (x * lax.rsqrt(ms + EPS)) * g * (1.0 + scale) + shift


def _head_rms(y, g):
    return y * lax.rsqrt(jnp.mean(y * y, axis=-1, keepdims=True) + EPS) * g


def _mod_rows(mod_ref, rows):
    return mod_ref[0] if mod_ref.shape[1] == 1 else mod_ref[0, rows, :]


def _row_tile(rows, want):
    tm = min(rows, want)
    assert rows % tm == 0
    return tm


def _mod_spec(mod, rows, tm, ngrid):
    G, R, D = mod.shape
    tiles_per_group = rows // (G * tm)
    assert tiles_per_group * G * tm == rows and R in (1, tm)
    if ngrid == 1:
        return pl.BlockSpec((1, R, D), lambda i: (i // tiles_per_group, 0, 0))
    return pl.BlockSpec((1, R, D), lambda i, k: (i // tiles_per_group, 0, 0))


def _ada_kernel(c_ref, w_ref, b_ref, o_ref):
    c = c_ref[...]
    a = (c * jax.nn.sigmoid(c)).astype(BF16)
    o_ref[...] = jnp.dot(a, w_ref[...].astype(BF16), preferred_element_type=F32) + b_ref[...]


def _ada(c, w, b, tn=1024):
    NL, D, N = w.shape
    M = c.shape[0]
    return pl.pallas_call(
        _ada_kernel,
        out_shape=jax.ShapeDtypeStruct((NL, M, N), F32),
        grid=(NL, N // tn),
        in_specs=[pl.BlockSpec((M, D), lambda l, j: (0, 0)),
                  pl.BlockSpec((None, D, tn), lambda l, j: (l, 0, j)),
                  pl.BlockSpec((None, 1, tn), lambda l, j: (l, 0, j))],
        out_specs=pl.BlockSpec((None, M, tn), lambda l, j: (l, 0, j)),
        compiler_params=_params("parallel", "parallel"),
        name="ada_modulation",
    )(c, w, b.reshape(NL, 1, N))


def _gla_in_kernel(x_ref, g_ref, sh_ref, sc_ref, w_ref, wr_ref, wg2_ref, bg_ref,
                   qk_ref, v_ref, gt_ref, la_ref, *, dk, dv, qscale):
    dot = functools.partial(jnp.dot, preferred_element_type=F32)
    tm = x_ref.shape[0]
    n_split = 2 if tm % (2 * LANES) == 0 else 1
    for part in range(n_split):
        rs = slice(part * (tm // n_split), (part + 1) * (tm // n_split))
        h = _prenorm(x_ref[rs, :], g_ref[...], _mod_rows(sh_ref, rs), _mod_rows(sc_ref, rs)).astype(BF16)
        r = dot(h, wr_ref[...])
        r_hi = r.astype(BF16)
        r_lo = (r - r_hi.astype(F32)).astype(BF16)
        z = dot(r_hi, wg2_ref[0]) + dot(r_lo, wg2_ref[0]) + dot(r_hi, wg2_ref[1]) + bg_ref[...]
        log_sig = jnp.minimum(z, 0.0) - jnp.log1p(jnp.exp(-jnp.abs(z)))
        la_ref[rs, :] = log_sig * (1.0 / GLA_GATE_NORM)
        qk_ref[rs, :dk] = (dot(h, w_ref[:, :dk]) * qscale).astype(qk_ref.dtype)
        qk_ref[rs, dk:] = dot(h, w_ref[:, dk:2 * dk]).astype(qk_ref.dtype)
        v_ref[rs, :] = dot(h, w_ref[:, 2 * dk:2 * dk + dv]).astype(v_ref.dtype)
        gt_ref[rs, :] = dot(h, w_ref[:, 2 * dk + dv:]).astype(gt_ref.dtype)


def _gla_in(x, g, shift, scale, w_main, w_r, w_g2, b_g, *, dk, dv, tm):
    rows, D = x.shape
    tm = _row_tile(rows, tm)
    kern = functools.partial(_gla_in_kernel, dk=dk, dv=dv, qscale=(dk // GLA_HEADS) ** -0.5)
    full = lambda a: pl.BlockSpec(a.shape, lambda i: (0,) * a.ndim)
    row = lambda n: pl.BlockSpec((tm, n), lambda i: (i, 0))
    return pl.pallas_call(
        kern,
        out_shape=(jax.ShapeDtypeStruct((rows, 2 * dk), BF16), jax.ShapeDtypeStruct((rows, dv), BF16),
                   jax.ShapeDtypeStruct((rows, dv), BF16), jax.ShapeDtypeStruct((rows, dk), F32)),
        grid=(rows // tm,),
        in_specs=[row(D), full(g), _mod_spec(shift, rows, tm, 1), _mod_spec(scale, rows, tm, 1),
                  full(w_main), full(w_r), full(w_g2), full(b_g)],
        out_specs=(row(2 * dk), row(dv), row(dv), row(dk)),
        compiler_params=_params("parallel"),
        name="gla_in_proj",
    )(x, g, shift, scale, w_main, w_r, w_g2, b_g)


def _gla_rec_kernel(q_ref, k_ref, v_ref, la_ref, s0_ref, o_ref, s_ref, st_ref, *, chunk, n_inner):
    lt = pl.program_id(1)
    H, dvh, dkh = st_ref.shape

    @pl.when(lt == 0)
    def _():
        for hd in range(H):
            st_ref[hd] = s0_ref[0, hd].T

    r_io = lax.broadcasted_iota(jnp.int32, (chunk, chunk), 0)
    c_io = lax.broadcasted_iota(jnp.int32, (chunk, chunk), 1)
    lower = r_io >= c_io
    tri = lower.astype(F32)
    mid = chunk // 2 - 1

    def body(c, carry):
        rows = pl.ds(pl.multiple_of(c * chunk, chunk), chunk)
        b_all = jnp.dot(tri, la_ref[rows, :], precision=HIGHEST, preferred_element_type=F32)
        for hd in range(H):
            ks = slice(hd * dkh, (hd + 1) * dkh)
            vs = slice(hd * dvh, (hd + 1) * dvh)
            q = q_ref[rows, ks].astype(F32)
            k = k_ref[rows, ks].astype(F32)
            vb = v_ref[rows, vs]
            b = b_all[:, ks]
            b_mid = b[mid:mid + 1, :]
            b_end = b[chunk - 1:chunk, :]
            qm = (q * jnp.exp(b - b_mid)).astype(BF16)
            km = (k * jnp.exp(b_mid - b)).astype(BF16)
            a = lax.dot_general(qm, km, NT, preferred_element_type=F32)
            a = jnp.where(lower, a, 0.0).astype(BF16)
            st = st_ref[hd]
            qd = (q * jnp.exp(b)).astype(BF16)
            o = jnp.dot(a, vb, preferred_element_type=F32)
            o = o + lax.dot_general(qd, st.astype(BF16), NT, preferred_element_type=F32)
            o_ref[rows, vs] = o.astype(o_ref.dtype)
            ke = (k * jnp.exp(b_end - b)).astype(BF16)
            upd = lax.dot_general(vb, ke, TN, preferred_element_type=F32)
            st_ref[hd] = st * jnp.exp(b_end) + upd
        return carry

    lax.fori_loop(0, n_inner, body, 0)

    @pl.when(lt == pl.num_programs(1) - 1)
    def _():
        for hd in range(H):
            s_ref[0, hd] = st_ref[hd].T


def _gla_rec(qk, v, la, s0, *, seq, chunk, tl):
    B, H, dkh, dvh = s0.shape
    tl = min(tl, seq)
    assert seq % tl == 0 and tl % chunk == 0
    nlt = seq // tl
    kern = functools.partial(_gla_rec_kernel, chunk=chunk, n_inner=tl // chunk)
    return pl.pallas_call(
        kern,
        out_shape=(jax.ShapeDtypeStruct(v.shape, BF16), jax.ShapeDtypeStruct(s0.shape, F32)),
        grid=(B, nlt),
        in_specs=[pl.BlockSpec((tl, H * dkh), lambda b, t: (b * nlt + t, 0)),
                  pl.BlockSpec((tl, H * dkh), lambda b, t: (b * nlt + t, 1)),
                  pl.BlockSpec((tl, H * dvh), lambda b, t: (b * nlt + t, 0)),
                  pl.BlockSpec((tl, H * dkh), lambda b, t: (b * nlt + t, 0)),
                  pl.BlockSpec((1, H, dkh, dvh), lambda b, t: (b, 0, 0, 0))],
        out_specs=(pl.BlockSpec((tl, H * dvh), lambda b, t: (b * nlt + t, 0)),
                   pl.BlockSpec((1, H, dkh, dvh), lambda b, t: (b, 0, 0, 0))),
        scratch_shapes=[pltpu.VMEM((H, dvh, dkh), F32)],
        compiler_params=_params("parallel", "arbitrary"),
        name="gla_recurrence",
    )(qk, qk, v, la, s0)


def _mix_mlp_kernel(*refs, gla):
    if gla:
        a_ref, gt_ref, on_ref = refs[:3]
        refs = refs[3:]
    else:
        a_ref = refs[0]
        refs = refs[1:]
    x_ref, wo_ref, g1_ref, g_ref, sh_ref, sc_ref, g2_ref, wu_ref, wd_ref, out_ref, h_ref = refs
    k = pl.program_id(1)

    @pl.when(k == 0)
    def _():
        if gla:
            dvh = a_ref.shape[1] // GLA_HEADS
            mix = jnp.zeros(out_ref.shape, F32)
            for hd in range(GLA_HEADS):
                sl = slice(hd * dvh, (hd + 1) * dvh)
                y = _head_rms(a_ref[:, sl].astype(F32), on_ref[...])
                gt = gt_ref[:, sl].astype(F32)
                y = y * (gt * jax.nn.sigmoid(gt))
                mix = mix + jnp.dot(y.astype(BF16), wo_ref[sl, :], preferred_element_type=F32)
        else:
            mix = jnp.dot(a_ref[...].astype(BF16), wo_ref[...], preferred_element_type=F32)
        x1 = x_ref[...] + g1_ref[0] * mix
        out_ref[...] = x1
        h_ref[...] = _prenorm(x1, g_ref[...], sh_ref[0], sc_ref[0]).astype(BF16)

    u = jnp.dot(h_ref[...], wu_ref[...], preferred_element_type=F32)
    u = jnp.square(jnp.maximum(u, 0.0)).astype(BF16)
    out_ref[...] += g2_ref[0] * jnp.dot(u, wd_ref[...], preferred_element_type=F32)


def _mix_mlp(mix_inputs, x, w_out, gate1, g, shift, scale, gate2, w_up, w_down, *, tm, tf):
    rows, D = x.shape
    FF = w_up.shape[1]
    tm = _row_tile(rows, tm)
    gla = len(mix_inputs) == 3
    row = lambda a: pl.BlockSpec((tm, a.shape[1]), lambda i, k: (i, 0))
    full = lambda a: pl.BlockSpec(a.shape, lambda i, k: (0,) * a.ndim)
    mix_specs = [row(mix_inputs[0]), row(mix_inputs[1]), full(mix_inputs[2])] if gla else [row(mix_inputs[0])]
    return pl.pallas_call(
        functools.partial(_mix_mlp_kernel, gla=gla),
        out_shape=jax.ShapeDtypeStruct((rows, D), F32),
        grid=(rows // tm, FF // tf),
        in_specs=mix_specs + [row(x), full(w_out), _mod_spec(gate1, rows, tm, 2), full(g),
                              _mod_spec(shift, rows, tm, 2), _mod_spec(scale, rows, tm, 2),
                              _mod_spec(gate2, rows, tm, 2),
                              pl.BlockSpec((D, tf), lambda i, k: (0, k)),
                              pl.BlockSpec((tf, D), lambda i, k: (k, 0))],
        out_specs=pl.BlockSpec((tm, D), lambda i, k: (i, 0)),
        scratch_shapes=[pltpu.VMEM((tm, D), BF16)],
        compiler_params=pltpu.CompilerParams(dimension_semantics=("parallel", "arbitrary"),
                                             vmem_limit_bytes=MIX_MLP_VMEM_LIMIT),
        name="gla_out_mlp" if gla else "moba_out_mlp",
    )(*mix_inputs, x, w_out, gate1, g, shift, scale, gate2, w_up, w_down)


def _kv_kernel(x_ref, g_ref, sh_ref, sc_ref, w_ref, kn_ref, k_ref, v_ref, *rest, hd, nsum, seq):
    h = _prenorm(x_ref[...], g_ref[...], sh_ref[0], sc_ref[0]).astype(BF16)
    n = k_ref.shape[1]
    tm = k_ref.shape[0]
    kk = jnp.dot(h, w_ref[:, :n], preferred_element_type=F32)
    vv = jnp.dot(h, w_ref[:, n:], preferred_element_type=F32)
    v_ref[...] = vv
    if nsum:
        ka_ref, va_ref, ks_ref = rest
        pos = (pl.program_id(0) * tm) % seq + lax.broadcasted_iota(jnp.int32, (tm, hd), 0)
        lane = lax.broadcasted_iota(jnp.int32, (tm, hd), 1)
        block_onehot = jnp.where(lane == pos // MOBA_BLOCK, 1.0, 0.0).astype(BF16)
        ones = jnp.ones((tm, hd), BF16)
    for a in range(n // hd):
        sl = slice(a * hd, (a + 1) * hd)
        kh = _head_rms(kk[:, sl], kn_ref[...])
        k_ref[:, sl] = kh
        if nsum:
            ka_ref[:, 2 * a * hd:(2 * a + 1) * hd] = kh.astype(BF16)
            ka_ref[:, (2 * a + 1) * hd:(2 * a + 2) * hd] = block_onehot
            va_ref[:, 2 * a * hd:(2 * a + 1) * hd] = vv[:, sl].astype(BF16)
            va_ref[:, (2 * a + 1) * hd:(2 * a + 2) * hd] = ones
            for r in range(nsum):
                ks_ref[r, :, sl] = jnp.sum(kh[r * MOBA_BLOCK:(r + 1) * MOBA_BLOCK], axis=0, keepdims=True)


def _kv(x, g, shift, scale, w_kv, k_norm, *, tm, seq, for_prompt):
    rows, D = x.shape
    n = w_kv.shape[1] // 2
    hd = k_norm.shape[1]
    tm = _row_tile(rows, tm)
    nsum = tm // MOBA_BLOCK if for_prompt else 0
    assert not for_prompt or (tm % MOBA_BLOCK == 0 and seq // MOBA_BLOCK <= hd)
    row = lambda m: pl.BlockSpec((tm, m), lambda i: (i, 0))
    full = lambda a: pl.BlockSpec(a.shape, lambda i: (0,) * a.ndim)
    out_shape = [jax.ShapeDtypeStruct((rows, n), F32), jax.ShapeDtypeStruct((rows, n), F32)]
    out_specs = [row(n), row(n)]
    if for_prompt:
        out_shape += [jax.ShapeDtypeStruct((rows, 2 * n), BF16), jax.ShapeDtypeStruct((rows, 2 * n), BF16),
                      jax.ShapeDtypeStruct((rows // MOBA_BLOCK, 1, n), F32)]
        out_specs += [row(2 * n), row(2 * n), pl.BlockSpec((nsum, 1, n), lambda i: (i, 0, 0))]
    return pl.pallas_call(
        functools.partial(_kv_kernel, hd=hd, nsum=nsum, seq=seq),
        out_shape=tuple(out_shape),
        grid=(rows // tm,),
        in_specs=[row(D), full(g), _mod_spec(shift, rows, tm, 1), _mod_spec(scale, rows, tm, 1),
                  full(w_kv), full(k_norm)],
        out_specs=tuple(out_specs),
        compiler_params=_params("parallel"),
        name="shared_kv",
    )(x, g, shift, scale, w_kv, k_norm)


def _q_kernel(x_ref, g_ref, sh_ref, sc_ref, w_ref, qn_ref, q_ref, *, hd):
    h = _prenorm(x_ref[...], g_ref[...], sh_ref[0], sc_ref[0]).astype(BF16)
    qq = jnp.dot(h, w_ref[...], preferred_element_type=F32)
    for a in range(q_ref.shape[1] // hd):
        sl = slice(a * hd, (a + 1) * hd)
        q_ref[:, sl] = _head_rms(qq[:, sl], qn_ref[...])


def _q_proj(x, g, shift, scale, w_q, q_norm, *, tm):
    rows, D = x.shape
    n = w_q.shape[1]
    tm = _row_tile(rows, tm)
    row = lambda m: pl.BlockSpec((tm, m), lambda i: (i, 0))
    full = lambda a: pl.BlockSpec(a.shape, lambda i: (0,) * a.ndim)
    return pl.pallas_call(
        functools.partial(_q_kernel, hd=q_norm.shape[1]),
        out_shape=jax.ShapeDtypeStruct((rows, n), F32),
        grid=(rows // tm,),
        in_specs=[row(D), full(g), _mod_spec(shift, rows, tm, 1), _mod_spec(scale, rows, tm, 1),
                  full(w_q), full(q_norm)],
        out_specs=row(n),
        compiler_params=_params("parallel"),
        name="moba_q_proj",
    )(x, g, shift, scale, w_q, q_norm)


def _top_blocks(scores, n_past, axis):
    nbk = scores.shape[axis]
    blk_id = lax.broadcasted_iota(jnp.int32, scores.shape, axis)
    sc = jnp.where(blk_id < n_past, scores, -jnp.inf)
    picks = []
    for r in range(MOBA_TOPK):
        m = jnp.max(sc, axis=axis, keepdims=True)
        idx = jnp.min(jnp.where(sc == m, blk_id, nbk), axis=axis, keepdims=True)
        picks.append((idx, (jnp.zeros_like(idx) + r) < n_past))
        sc = jnp.where(blk_id == idx, -jnp.inf, sc)
    return picks


def _q_sel_kernel(x_ref, g_ref, sh_ref, sc_ref, w_ref, qn_ref, ks_ref, qa_ref, *, hd, seq, scale):
    tm = x_ref.shape[0]
    nbk = ks_ref.shape[1]
    h = _prenorm(x_ref[...], g_ref[...], sh_ref[0], sc_ref[0]).astype(BF16)
    qq = jnp.dot(h, w_ref[...], preferred_element_type=F32)
    pos = (pl.program_id(0) * tm) % seq + lax.broadcasted_iota(jnp.int32, (1, tm), 1)
    own = pos // MOBA_BLOCK
    blk_id = lax.broadcasted_iota(jnp.int32, (nbk, tm), 0)
    for a in range(ATT_HEADS):
        sl = slice(a * hd, (a + 1) * hd)
        qh = _head_rms(qq[:, sl], qn_ref[...])
        scores = lax.dot_general(ks_ref[0, :, sl] * (1.0 / MOBA_BLOCK), qh, NT, precision=HIGHEST,
                                 preferred_element_type=F32)
        chosen = blk_id < 0
        for idx, ok in _top_blocks(scores, own, 0):
            chosen = chosen | ((blk_id == idx) & ok)
        is_prev = blk_id == own - 1
        far = jnp.where(chosen & jnp.logical_not(is_prev), 0.0, NEG)
        prev = jnp.max(jnp.where(chosen & is_prev, 0.0, NEG), axis=0, keepdims=True)
        mask_t = jnp.concatenate([far, jnp.zeros((hd - nbk - SUBLANES, tm), F32),
                                  jnp.broadcast_to(prev, (SUBLANES, tm))], axis=0)
        qa_ref[:, 2 * a * hd:(2 * a + 1) * hd] = (qh * scale).astype(BF16)
        qa_ref[:, (2 * a + 1) * hd:(2 * a + 2) * hd] = mask_t.T.astype(BF16)


def _q_proj_sel(x, g, shift, scale, w_q, q_norm, ksum, *, tm, seq):
    rows, D = x.shape
    n = w_q.shape[1]
    hd = q_norm.shape[1]
    B, nbk, _ = ksum.shape
    tm = _row_tile(rows, tm)
    tiles_per_seq = seq // tm
    assert seq % tm == 0 and tm % LANES == 0 and nbk % SUBLANES == 0 and nbk + SUBLANES <= hd
    row = lambda m: pl.BlockSpec((tm, m), lambda i: (i, 0))
    full = lambda a: pl.BlockSpec(a.shape, lambda i: (0,) * a.ndim)
    return pl.pallas_call(
        functools.partial(_q_sel_kernel, hd=hd, seq=seq, scale=hd ** -0.5),
        out_shape=jax.ShapeDtypeStruct((rows, 2 * n), BF16),
        grid=(rows // tm,),
        in_specs=[row(D), full(g), _mod_spec(shift, rows, tm, 1), _mod_spec(scale, rows, tm, 1),
                  full(w_q), full(q_norm),
                  pl.BlockSpec((1, nbk, n), lambda i: (i // tiles_per_seq, 0, 0))],
        out_specs=row(2 * n),
        compiler_params=_params("parallel"),
        name="moba_q_proj_select",
    )(x, g, shift, scale, w_q, q_norm, ksum)


def _moba_prompt_kernel(qa_ref, ka_ref, va_ref, bias_ref, o_ref, m_ref, acc_ref,
                        sa_ref, sb_ref, pa_ref, pb_ref, al_ref, pn_ref, m0_ref,
                        *, blk, group, n_groups, halves):
    it = pl.program_id(2)
    hd = qa_ref.shape[1] // 2
    last_group = n_groups - 1

    def rows_of(g):
        return pl.ds(pl.multiple_of(g * (group * blk), group * blk), group * blk)

    def scores_of(g):
        return lax.dot_general(qa_ref[...], ka_ref[rows_of(g), :], NT, preferred_element_type=F32)

    sa_ref[...] = scores_of(0)

    t_io = lax.broadcasted_iota(jnp.int32, (blk, blk), 0)
    s_io = lax.broadcasted_iota(jnp.int32, (blk, blk), 1)

    def near_rows(hf):
        own = it * halves + hf
        return (pl.ds(pl.multiple_of(jnp.maximum(own - 1, 0) * blk, blk), blk),
                pl.ds(pl.multiple_of(own * blk, blk), blk))

    for hf in range(halves):
        rq = slice(hf * blk, (hf + 1) * blk)
        r_prev, r_own = near_rows(hf)
        qb = qa_ref[rq, :hd]
        prev_mask = qa_ref[rq, 2 * hd - 1:].astype(F32)
        s_own = lax.dot_general(qb, ka_ref[r_own, :hd], NT, preferred_element_type=F32)
        s_own = jnp.where(s_io <= t_io, s_own + bias_ref[0, 0], NEG)
        s_prev = lax.dot_general(qb, ka_ref[r_prev, :hd], NT, preferred_element_type=F32)
        s_prev = s_prev + bias_ref[0, 1] + prev_mask
        m0 = jnp.maximum(jnp.max(s_own, axis=1, keepdims=True), jnp.max(s_prev, axis=1, keepdims=True))
        pn_ref[rq, :blk] = jnp.exp(s_prev - m0).astype(BF16)
        pn_ref[rq, blk:] = jnp.exp(s_own - m0).astype(BF16)
        m_ref[rq, :] = m0
        m0_ref[rq, :] = m0
    acc_ref[...] = jnp.zeros_like(acc_ref)

    def softmax(s_ref, p_ref, slot):
        m_prev = m_ref[...]
        s = s_ref[...]
        m_new = jnp.maximum(m_prev, jnp.max(s, axis=1, keepdims=True))
        p_ref[...] = jnp.exp(s - m_new).astype(BF16)
        al_ref[slot] = jnp.exp(m_prev - m_new)
        m_ref[...] = m_new

    def flush(p_ref, slot, g):
        acc_ref[...] = al_ref[slot] * acc_ref[...] + jnp.dot(p_ref[...], va_ref[rows_of(g), :],
                                                             preferred_element_type=F32)

    n_far = it * halves + halves - 2
    n_pairs = ((n_far + group - 1) // group + 1) // 2
    pb_ref[...] = jnp.zeros_like(pb_ref)
    al_ref[1] = jnp.ones(al_ref.shape[1:], F32)

    def body(t, carry):
        sb_ref[...] = scores_of(2 * t + 1)
        flush(pb_ref, 1, jnp.maximum(2 * t - 1, 0))
        softmax(sa_ref, pa_ref, 0)
        sa_ref[...] = scores_of(jnp.minimum(2 * t + 2, last_group))
        flush(pa_ref, 0, 2 * t)
        softmax(sb_ref, pb_ref, 1)
        return carry

    lax.fori_loop(0, n_pairs, body, 0)

    flush(pb_ref, 1, jnp.maximum(2 * n_pairs - 1, 0))
    for hf in range(halves):
        rq = slice(hf * blk, (hf + 1) * blk)
        r_prev, r_own = near_rows(hf)
        near = jnp.dot(pn_ref[rq, :blk], va_ref[r_prev, :], preferred_element_type=F32)
        near = near + jnp.dot(pn_ref[rq, blk:], va_ref[r_own, :], preferred_element_type=F32)
        acc = acc_ref[rq, :] + jnp.exp(m0_ref[rq, :] - m_ref[rq, :]) * near
        o_ref[rq, :] = (acc[:, :hd] / acc[:, hd:]).astype(o_ref.dtype)


def _moba_prompt(qa, ka, va, bias, *, batch, seq, group=4, halves=2):
    rows, n2 = qa.shape
    hd = n2 // (2 * ATT_HEADS)
    blk = MOBA_BLOCK
    nq = seq // blk
    assert nq % (2 * group) == 0 and group >= 2 and nq % halves == 0
    nt = nq // halves
    tq = halves * blk
    kern = functools.partial(_moba_prompt_kernel, blk=blk, group=group, n_groups=nq // group, halves=halves)
    return pl.pallas_call(
        kern,
        out_shape=jax.ShapeDtypeStruct((rows, ATT_HEADS * hd), BF16),
        grid=(batch, ATT_HEADS, nt),
        in_specs=[pl.BlockSpec((tq, 2 * hd), lambda b, h, i: (b * nt + i, h)),
                  pl.BlockSpec((seq, 2 * hd), lambda b, h, i: (b, h)),
                  pl.BlockSpec((seq, 2 * hd), lambda b, h, i: (b, h)),
                  pl.BlockSpec((1, 2, blk, blk), lambda b, h, i: (h, 0, 0, 0))],
        out_specs=pl.BlockSpec((tq, hd), lambda b, h, i: (b * nt + i, h)),
        scratch_shapes=[pltpu.VMEM((tq, 1), F32), pltpu.VMEM((tq, 2 * hd), F32),
                        pltpu.VMEM((tq, group * blk), F32), pltpu.VMEM((tq, group * blk), F32),
                        pltpu.VMEM((tq, group * blk), BF16), pltpu.VMEM((tq, group * blk), BF16),
                        pltpu.VMEM((2, tq, 1), F32), pltpu.VMEM((tq, 2 * blk), BF16),
                        pltpu.VMEM((tq, 1), F32)],
        compiler_params=_params("parallel", "parallel", "arbitrary"),
        name="moba_prompt_attention",
    )(qa, ka, va, bias)


def _page_sum_kernel(pt_ref, *refs, ppb):
    o_ref = refs[-1]
    pages = refs[:-1]
    for r in range(len(pages) // ppb):
        total = jnp.sum(pages[r * ppb][0], axis=0)
        for p in range(1, ppb):
            total = total + jnp.sum(pages[r * ppb + p][0], axis=0)
        o_ref[0, r] = total


def _sample_block_sums(cache_k, pt_flat, *, dec_batch, n_pages, blocks_per_step=4):
    _, page, H, hd = cache_k.shape
    ppb = MOBA_BLOCK // page
    nbk = n_pages // ppb
    bps = math.gcd(blocks_per_step, nbk)
    nper = bps * ppb

    def page_spec(p):
        return pl.BlockSpec((1, page, H, hd), lambda d, j, pt: (pt[d * n_pages + j * nper + p], 0, 0, 0))

    return pl.pallas_call(
        functools.partial(_page_sum_kernel, ppb=ppb),
        out_shape=jax.ShapeDtypeStruct((dec_batch, nbk, H, hd), F32),
        grid_spec=pltpu.PrefetchScalarGridSpec(
            num_scalar_prefetch=1,
            grid=(dec_batch, nbk // bps),
            in_specs=[page_spec(p) for p in range(nper)],
            out_specs=pl.BlockSpec((1, bps, H, hd), lambda d, j, pt: (d, j, 0, 0))),
        compiler_params=_params("parallel", "arbitrary"),
        name="sample_block_sums",
    )(pt_flat, *([cache_k] * nper))


def _sample_select_kernel(q_ref, ks_ref, idx_ref, *, hd, blk):
    nbk = ks_ref.shape[1]
    lane = lax.broadcasted_iota(jnp.int32, idx_ref.shape[1:], 1)
    out = jnp.zeros(idx_ref.shape[1:], jnp.int32)
    for a in range(ATT_HEADS):
        means = ks_ref[0, :, a, :] * (1.0 / blk)
        scores = lax.dot_general(q_ref[0, :, a * hd:(a + 1) * hd], means, NT, precision=HIGHEST,
                                 preferred_element_type=F32)
        for r, (idx, _) in enumerate(_top_blocks(scores, nbk, 1)):
            out = jnp.where(lane == a * MOBA_TOPK + r, idx, out)
    idx_ref[0] = out


def _sample_select(q8, ksum):
    DB, R, n = q8.shape
    _, nbk, H, hd = ksum.shape
    assert nbk >= MOBA_TOPK and H == ATT_HEADS
    kern = functools.partial(_sample_select_kernel, hd=hd, blk=MOBA_BLOCK)
    return pl.pallas_call(
        kern,
        out_shape=jax.ShapeDtypeStruct((DB, R, LANES), jnp.int32),
        grid=(DB,),
        in_specs=[pl.BlockSpec((1, R, n), lambda d: (d, 0, 0)),
                  pl.BlockSpec((1, nbk, H, hd), lambda d: (d, 0, 0, 0))],
        out_specs=pl.BlockSpec((1, R, LANES), lambda d: (d, 0, 0)),
        compiler_params=_params("parallel"),
        name="sample_select",
    )(q8, ksum)


def _sample_attn_kernel(pt_ref, sel_ref, q_ref, kn_ref, vn_ref, bias_ref, ck_ref, cv_ref, o_ref,
                        kbuf, vbuf, sem, *, lq_n, ppb, n_pages, page, scale):
    g = pl.program_id(0)
    nslot = lq_n * MOBA_TOPK
    nsl = nslot * ppb

    def copies(step, slot):
        d = step // ATT_HEADS
        a = step % ATT_HEADS
        out = []
        for j in range(nsl):
            phys = pt_ref[d * n_pages + sel_ref[step * nslot + j // ppb] * ppb + j % ppb]
            rows = pl.ds(j * page, page)
            out.append(pltpu.make_async_copy(ck_ref.at[phys, :, a, :], kbuf.at[slot, rows, :], sem.at[slot, 0, j]))
            out.append(pltpu.make_async_copy(cv_ref.at[phys, :, a, :], vbuf.at[slot, rows, :], sem.at[slot, 1, j]))
        return out

    @pl.when(g == 0)
    def _():
        for c in copies(0, 0):
            c.start()

    @pl.when(g + 1 < pl.num_programs(0))
    def _():
        for c in copies(g + 1, (g + 1) % 2):
            c.start()

    slot = g % 2
    q = q_ref[0] * scale
    row = lax.broadcasted_iota(jnp.int32, (q.shape[0], 1), 0)

    new_logits = []
    for s in range(lq_n):
        sc = jnp.sum(q * kn_ref[0, s:s + 1, :], axis=1, keepdims=True) + bias_ref[0, 8:16, s:s + 1]
        new_logits.append(jnp.where(s <= row, sc, NEG))

    for c in copies(g, slot):
        c.wait()

    s_all = lax.dot_general(q.astype(BF16), kbuf[slot].astype(BF16), NT, preferred_element_type=F32)
    parts = []
    for j in range(nsl):
        lp = sel_ref[g * nslot + j // ppb] * ppb + j % ppb
        is_last = (jnp.zeros((q.shape[0], page), jnp.int32) + lp) == n_pages - 1
        sj = s_all[:, j * page:(j + 1) * page] + jnp.where(is_last, bias_ref[0, 0:8, :], 0.0)
        parts.append(jnp.where(row == j // (MOBA_TOPK * ppb), sj, NEG))
    m = jnp.max(functools.reduce(jnp.maximum, parts), axis=1, keepdims=True)
    m = functools.reduce(jnp.maximum, new_logits, m)
    probs = [jnp.exp(p - m) for p in parts]
    den = jnp.sum(functools.reduce(jnp.add, probs), axis=1, keepdims=True)
    pcat = jnp.concatenate([p.astype(BF16) for p in probs], axis=1)
    acc = jnp.dot(pcat, vbuf[slot].astype(BF16), preferred_element_type=F32)
    for s in range(lq_n):
        e = jnp.exp(new_logits[s] - m)
        den = den + e
        acc = acc + e * vn_ref[0, s:s + 1, :]
    o_ref[0] = acc / den


def _sample_attn(q8, kn8, vn8, bias, cache_k, cache_v, pt_flat, sel_flat, *, lq_n, n_pages):
    DB, R, n = q8.shape
    _, page, H, hd = cache_k.shape
    ppb = MOBA_BLOCK // page
    nsl = lq_n * MOBA_TOPK * ppb
    assert page == LANES and R == SUBLANES and H == ATT_HEADS
    kern = functools.partial(_sample_attn_kernel, lq_n=lq_n, ppb=ppb, n_pages=n_pages, page=page,
                             scale=hd ** -0.5)
    head = pl.BlockSpec((1, R, hd), lambda g, pt, sel: (g // ATT_HEADS, 0, g % ATT_HEADS))
    return pl.pallas_call(
        kern,
        out_shape=jax.ShapeDtypeStruct((DB, R, n), F32),
        grid_spec=pltpu.PrefetchScalarGridSpec(
            num_scalar_prefetch=2,
            grid=(DB * ATT_HEADS,),
            in_specs=[head, head, head,
                      pl.BlockSpec((1, 2 * SUBLANES, LANES), lambda g, pt, sel: (g % ATT_HEADS, 0, 0)),
                      pl.BlockSpec(memory_space=pl.ANY), pl.BlockSpec(memory_space=pl.ANY)],
            out_specs=head,
            scratch_shapes=[pltpu.VMEM((2, nsl * page, hd), F32), pltpu.VMEM((2, nsl * page, hd), F32),
                            pltpu.SemaphoreType.DMA((2, 2, nsl))]),
        compiler_params=_params("arbitrary"),
        name="sample_attention",
    )(pt_flat, sel_flat, q8, kn8, vn8, bias, cache_k, cache_v)


def _bucket_of_distance():
    n = np.arange(MAX_DISTANCE)
    max_exact = N_BUCKETS // 2
    nf = np.maximum(n, max_exact).astype(np.float32)
    large = max_exact + (np.log(nf / np.float32(max_exact)) / np.float32(math.log(MAX_DISTANCE / max_exact))
                         * np.float32(N_BUCKETS - max_exact)).astype(np.int32)
    return np.where(n < max_exact, n, np.minimum(large, N_BUCKETS - 1)).astype(np.int32)


def _dist_table(rel_bias, width):
    t = rel_bias.astype(F32)[_bucket_of_distance()].T
    t = t - t[:, MAX_DISTANCE - 1:]
    return jnp.pad(t, ((0, 0), (0, width - MAX_DISTANCE)))


def _toeplitz(first_row_wrapped, n):
    H = first_row_wrapped.shape[0]
    x = jnp.broadcast_to(first_row_wrapped[:, None, :], (H, n, 2 * n)).reshape(H, 2 * n * n)
    return x[:, :n * (2 * n - 1)].reshape(H, n, 2 * n - 1)[:, :, :n]


def _prompt_bias_tiles(rel_bias):
    blk = MOBA_BLOCK
    assert blk >= MAX_DISTANCE
    t = _dist_table(rel_bias, blk)
    rev = jnp.flip(t[:, 1:], axis=1)
    zeros = jnp.zeros_like(t)
    own = _toeplitz(jnp.concatenate([t[:, :1], zeros, rev], axis=1), blk)
    prev = _toeplitz(jnp.concatenate([zeros[:, :1], rev, zeros], axis=1), blk)
    return jnp.stack([own, prev], axis=1)


def _sample_bias_rows(rel_bias, *, page, lq_n):
    assert page == LANES and page >= MAX_DISTANCE and lq_n <= SUBLANES
    t = _dist_table(rel_bias, 2 * page)
    r = np.arange(SUBLANES)[:, None]
    c = np.arange(LANES)[None, :]
    return jnp.concatenate([t[:, page + r - c], t[:, np.clip(r - c, 0, None)]], axis=1)


def _hi_lo(w):
    hi = w.astype(BF16)
    return jnp.stack([hi, (w - hi.astype(F32)).astype(BF16)], axis=1)


def _pad_tokens(a, groups, per, to):
    return jnp.pad(a.reshape(groups, per, a.shape[-1]), ((0, 0), (0, to - per), (0, 0)))


def _trunk(x, mods, kvmod, s0, W, *, batch, seq, tm, attend, make_ctx):
    n_a = W['gla_w_main'].shape[0]
    depth = W['mlp_w_up'].shape[0]
    dk = W['gla_b_gate'].shape[2]
    dv = W['gla_w_out'].shape[1]
    states = []
    k = v = ctx = None
    for i in range(depth):
        sh1, sc1, g1, sh2, sc2, g2 = mods[i]
        if i == n_a:
            k, v, ctx = make_ctx(x, kvmod)
        if i < n_a:
            qk, vv, gt, la = _gla_in(x, W['norm_mix'][i], sh1, sc1, W['gla_w_main'][i], W['gla_w_r'][i],
                                     W['gla_w_g2'][i], W['gla_b_gate'][i], dk=dk, dv=dv, tm=tm)
            chunk = math.gcd(GLA_CHUNK, seq)
            if chunk < 16:
                padded = 16
                qk, vv, la = (_pad_tokens(t, batch, seq, padded).reshape(batch * padded, -1)
                              for t in (qk, vv, la))
                o, s = _gla_rec(qk, vv, la, s0[i], seq=padded, chunk=padded, tl=padded)
                o = o.reshape(batch, padded, dv)[:, :seq].reshape(batch * seq, dv)
            else:
                o, s = _gla_rec(qk, vv, la, s0[i], seq=seq, chunk=chunk, tl=512)
            states.append(s)
            mix_inputs, w_mix = (o, gt, W['gla_onorm'][i]), W['gla_w_out'][i]
        else:
            j = i - n_a
            q_args = (W['norm_mix'][i], sh1, sc1, W['moba_w_q'][j], W['q_norm'][j])
            mix_inputs, w_mix = (attend(x, q_args, ctx),), W['moba_w_o'][j]
        x = _mix_mlp(mix_inputs, x, w_mix, g1, W['norm_mlp'][i], sh2, sc2, g2,
                     W['mlp_w_up'][i], W['mlp_w_down'][i], tm=min(tm * 2, 1024), tf=1024)
    return x, jnp.stack(states), k, v


def kernel(x_prompt, x_sample, state_gla, cache_k, cache_v, page_table, c_prompt, c_sample,
           ada_w, ada_b, norm_mix, norm_mlp, gla_w_in, gla_w_gate2, gla_b_gate, gla_onorm, gla_w_out,
           kv_ada_w, kv_ada_b, kv_norm, w_kv, k_norm, moba_w_q, q_norm, moba_w_o, rel_bias,
           mlp_w_up, mlp_w_down):
    B, L, D = x_prompt.shape
    DB, LQ, _ = x_sample.shape
    depth = ada_w.shape[0]
    n_a = gla_w_in.shape[0]
    rank, dk = gla_w_gate2.shape[1:]
    dv = gla_w_out.shape[1]
    n_pool, page, H, hd = cache_k.shape
    n_pages = page_table.shape[1]
    past = n_pages * page
    assert H == ATT_HEADS and L % MOBA_BLOCK == 0 and MOBA_BLOCK % page == 0
    assert past % MOBA_BLOCK == 0 and LQ <= min(SUBLANES, MOBA_BLOCK) and rank <= LANES

    W = {
        'norm_mix': norm_mix.reshape(depth, 1, D), 'norm_mlp': norm_mlp.reshape(depth, 1, D),
        'gla_w_main': gla_w_in[:, :, :2 * dk + 2 * dv].astype(BF16),
        'gla_w_r': jnp.pad(gla_w_in[:, :, 2 * dk + 2 * dv:], ((0, 0), (0, 0), (0, LANES - rank))).astype(BF16),
        'gla_w_g2': _hi_lo(jnp.pad(gla_w_gate2, ((0, 0), (0, LANES - rank), (0, 0)))),
        'gla_b_gate': gla_b_gate.reshape(n_a, 1, dk), 'gla_onorm': gla_onorm.reshape(n_a, 1, -1),
        'gla_w_out': gla_w_out.astype(BF16),
        'moba_w_q': moba_w_q.astype(BF16), 'q_norm': q_norm.reshape(-1, 1, hd),
        'moba_w_o': moba_w_o.astype(BF16),
        'mlp_w_up': mlp_w_up.astype(BF16), 'mlp_w_down': mlp_w_down.astype(BF16),
    }
    w_kv_b = w_kv.astype(BF16)
    kv_norm2 = kv_norm.reshape(1, D)
    k_norm2 = k_norm.reshape(1, hd)

    n_c = B + DB
    c_all = jnp.pad(jnp.concatenate([c_prompt, c_sample], axis=0), ((0, -n_c % SUBLANES), (0, 0)))
    mod = _ada(c_all, ada_w, ada_b)
    kvm = _ada(c_all, kv_ada_w[None], kv_ada_b[None])[0]

    def prompt_mod(m):
        return m[:B].reshape(B, 1, D)

    def sample_mod(m):
        return jnp.repeat(m[B:n_c], LQ, axis=0).reshape(1, DB * LQ, D)

    def split_mods(pick):
        layers = [[pick(mod[i, :, t * D:(t + 1) * D]) for t in range(6)] for i in range(depth)]
        return layers, [pick(kvm[:, t * D:(t + 1) * D]) for t in range(2)]

    p_mods, p_kvmod = split_mods(prompt_mod)
    p_bias = _prompt_bias_tiles(rel_bias)

    def prompt_ctx(x, kvmod):
        k, v, ka, va, ksum = _kv(x, kv_norm2, kvmod[0], kvmod[1], w_kv_b, k_norm2, tm=512, seq=L,
                                 for_prompt=True)
        return k, v, (ka, va, ksum.reshape(B, L // MOBA_BLOCK, H * hd))

    def prompt_attend(x, q_args, ctx):
        ka, va, ksum = ctx
        qa = _q_proj_sel(x, *q_args, ksum, tm=512, seq=L)
        return _moba_prompt(qa, ka, va, p_bias, batch=B, seq=L)

    s0_p = jnp.zeros((n_a, B) + state_gla.shape[2:], F32)
    y_p, st_p, k_p, v_p = _trunk(x_prompt.reshape(B * L, D), p_mods, p_kvmod, s0_p, W, batch=B, seq=L,
                                 tm=512, attend=prompt_attend, make_ctx=prompt_ctx)

    s_mods, s_kvmod = split_mods(sample_mod)
    s_bias = _sample_bias_rows(rel_bias, page=page, lq_n=LQ)
    pt_flat = page_table.reshape(-1).astype(jnp.int32)
    rows_s = DB * LQ

    def sample_ctx(x, kvmod):
        k, v = _kv(x, kv_norm2, kvmod[0], kvmod[1], w_kv_b, k_norm2, tm=rows_s, seq=LQ, for_prompt=False)
        ksum = _sample_block_sums(cache_k, pt_flat, dec_batch=DB, n_pages=n_pages)
        return k, v, (_pad_tokens(k, DB, LQ, SUBLANES), _pad_tokens(v, DB, LQ, SUBLANES), ksum)

    def sample_attend(x, q_args, ctx):
        kn8, vn8, ksum = ctx
        q8 = _pad_tokens(_q_proj(x, *q_args, tm=rows_s), DB, LQ, SUBLANES)
        picks = _sample_select(q8, ksum)[:, :LQ, :ATT_HEADS * MOBA_TOPK]
        sel_flat = picks.reshape(DB, LQ, ATT_HEADS, MOBA_TOPK).transpose(0, 2, 1, 3).reshape(-1)
        o8 = _sample_attn(q8, kn8, vn8, s_bias, cache_k, cache_v, pt_flat, sel_flat,
                          lq_n=LQ, n_pages=n_pages)
        return o8[:, :LQ].reshape(rows_s, H * hd)

    y_s, st_s, k_s, v_s = _trunk(x_sample.reshape(rows_s, D), s_mods, s_kvmod, state_gla.astype(F32), W,
                                 batch=DB, seq=LQ, tm=rows_s, attend=sample_attend, make_ctx=sample_ctx)

    return (y_p.reshape(B, L, D), y_s.reshape(DB, LQ, D), st_p, st_s,
            k_p.reshape(B, L, H, hd), v_p.reshape(B, L, H, hd),
            k_s.reshape(DB, LQ, H, hd), v_s.reshape(DB, LQ, H, hd))
```

```python
import functools
import math

import numpy as np
import jax
import jax.numpy as jnp
from jax import lax
from jax.experimental import pallas as pl
from jax.experimental.pallas import tpu as pltpu

F32 = jnp.float32
BF16 = jnp.bfloat16

GLA_HEADS = 4
GLA_GATE_NORM = 16.0
GLA_CHUNK = 64
ATT_HEADS = 8
MOBA_BLOCK = 256
MOBA_TOPK = 3
N_BUCKETS = 32
MAX_DISTANCE = 128
EPS = 1e-6

LANES = 128
SUBLANES = 8
VMEM_LIMIT = 48 * 1024 * 1024
MIX_MLP_VMEM_LIMIT = 56 * 1024 * 1024
NEG = -1e30
NT = (((1,), (1,)), ((), ()))
TN = (((0,), (0,)), ((), ()))


def _params(*sem):
    return pltpu.CompilerParams(dimension_semantics=sem, vmem_limit_bytes=VMEM_LIMIT)


def _prenorm(x, g, shift, scale):
    ms = jnp.mean(x * x, axis=-1, keepdims=True)
    return (x * lax.rsqrt(ms + EPS)) * g * (1.0 + scale) + shift


def _head_rms(y, g):
    return y * lax.rsqrt(jnp.mean(y * y, axis=-1, keepdims=True) + EPS) * g


def _dot_nt_3pass(a, b):
    a_hi = a.astype(BF16)
    b_hi = b.astype(BF16)
    a_lo = (a - a_hi.astype(F32)).astype(BF16)
    b_lo = (b - b_hi.astype(F32)).astype(BF16)
    dot = functools.partial(lax.dot_general, dimension_numbers=NT, preferred_element_type=F32)
    return dot(a_hi, b_hi) + dot(a_lo, b_hi) + dot(a_hi, b_lo)


def _mod_rows(mod_ref, rows):
    return mod_ref[0] if mod_ref.shape[1] == 1 else mod_ref[0, rows, :]


def _row_tile(rows, want):
    tm = min(rows, want)
    assert rows % tm == 0
    return tm


def _mod_spec(mod, rows, tm, ngrid):
    G, R, D = mod.shape
    tiles_per_group = rows // (G * tm)
    assert tiles_per_group * G * tm == rows and R in (1, tm)
    if ngrid == 1:
        return pl.BlockSpec((1, R, D), lambda i: (i // tiles_per_group, 0, 0))
    return pl.BlockSpec((1, R, D), lambda i, k: (i // tiles_per_group, 0, 0))


def _ada_kernel(c_ref, w_ref, b_ref, o_ref):
    c = c_ref[...]
    a = (c * jax.nn.sigmoid(c)).astype(BF16)
    o_ref[...] = jnp.dot(a, w_ref[...].astype(BF16), preferred_element_type=F32) + b_ref[...]


def _ada(c, w, b, tn=1024):
    NL, D, N = w.shape
    M = c.shape[0]
    return pl.pallas_call(
        _ada_kernel,
        out_shape=jax.ShapeDtypeStruct((NL, M, N), F32),
        grid=(NL, N // tn),
        in_specs=[pl.BlockSpec((M, D), lambda l, j: (0, 0)),
                  pl.BlockSpec((None, D, tn), lambda l, j: (l, 0, j)),
                  pl.BlockSpec((None, 1, tn), lambda l, j: (l, 0, j))],
        out_specs=pl.BlockSpec((None, M, tn), lambda l, j: (l, 0, j)),
        compiler_params=_params("parallel", "parallel"),
        name="ada_modulation",
    )(c, w, b.reshape(NL, 1, N))


def _gla_in_kernel(x_ref, g_ref, sh_ref, sc_ref, w_ref, wr_ref, wg2_ref, bg_ref,
                   qk_ref, v_ref, gt_ref, la_ref, *, dk, dv, qscale):
    dot = functools.partial(jnp.dot, preferred_element_type=F32)
    tm = x_ref.shape[0]
    n_split = 2 if tm % (2 * LANES) == 0 else 1
    for part in range(n_split):
        rs = slice(part * (tm // n_split), (part + 1) * (tm // n_split))
        h = _prenorm(x_ref[rs, :], g_ref[...], _mod_rows(sh_ref, rs), _mod_rows(sc_ref, rs)).astype(BF16)
        r = dot(h, wr_ref[...])
        r_hi = r.astype(BF16)
        r_lo = (r - r_hi.astype(F32)).astype(BF16)
        z = dot(r_hi, wg2_ref[0]) + dot(r_lo, wg2_ref[0]) + dot(r_hi, wg2_ref[1]) + bg_ref[...]
        log_sig = jnp.minimum(z, 0.0) - jnp.log1p(jnp.exp(-jnp.abs(z)))
        la_ref[rs, :] = log_sig * (1.0 / GLA_GATE_NORM)
        qk_ref[rs, :dk] = (dot(h, w_ref[:, :dk]) * qscale).astype(qk_ref.dtype)
        qk_ref[rs, dk:] = dot(h, w_ref[:, dk:2 * dk]).astype(qk_ref.dtype)
        v_ref[rs, :] = dot(h, w_ref[:, 2 * dk:2 * dk + dv]).astype(v_ref.dtype)
        gt_ref[rs, :] = dot(h, w_ref[:, 2 * dk + dv:]).astype(gt_ref.dtype)


def _gla_in(x, g, shift, scale, w_main, w_r, w_g2, b_g, *, dk, dv, tm):
    rows, D = x.shape
    tm = _row_tile(rows, tm)
    kern = functools.partial(_gla_in_kernel, dk=dk, dv=dv, qscale=(dk // GLA_HEADS) ** -0.5)
    full = lambda a: pl.BlockSpec(a.shape, lambda i: (0,) * a.ndim)
    row = lambda n: pl.BlockSpec((tm, n), lambda i: (i, 0))
    return pl.pallas_call(
        kern,
        out_shape=(jax.ShapeDtypeStruct((rows, 2 * dk), BF16), jax.ShapeDtypeStruct((rows, dv), BF16),
                   jax.ShapeDtypeStruct((rows, dv), BF16), jax.ShapeDtypeStruct((rows, dk), F32)),
        grid=(rows // tm,),
        in_specs=[row(D), full(g), _mod_spec(shift, rows, tm, 1), _mod_spec(scale, rows, tm, 1),
                  full(w_main), full(w_r), full(w_g2), full(b_g)],
        out_specs=(row(2 * dk), row(dv), row(dv), row(dk)),
        compiler_params=_params("parallel"),
        name="gla_in_proj",
    )(x, g, shift, scale, w_main, w_r, w_g2, b_g)


def _gla_rec_kernel(q_ref, k_ref, v_ref, la_ref, s0_ref, o_ref, s_ref, st_ref, *, chunk, n_inner):
    lt = pl.program_id(1)
    H, dvh, dkh = st_ref.shape

    @pl.when(lt == 0)
    def _():
        for hd in range(H):
            st_ref[hd] = s0_ref[0, hd].T

    r_io = lax.broadcasted_iota(jnp.int32, (chunk, chunk), 0)
    c_io = lax.broadcasted_iota(jnp.int32, (chunk, chunk), 1)
    lower = r_io >= c_io
    tri = jnp.where(lower, 1.0, 0.0).astype(BF16)
    mid = chunk // 2 - 1

    def chunk_step(c):
        rows = pl.ds(pl.multiple_of(c * chunk, chunk), chunk)
        la = la_ref[rows, :]
        la_hi = la.astype(BF16)
        rest = la - la_hi.astype(F32)
        la_mid = rest.astype(BF16)
        la_lo = (rest - la_mid.astype(F32)).astype(BF16)
        b_all = sum(jnp.dot(tri, part, preferred_element_type=F32) for part in (la_hi, la_mid, la_lo))
        for hd in range(H):
            ks = slice(hd * dkh, (hd + 1) * dkh)
            vs = slice(hd * dvh, (hd + 1) * dvh)
            q = q_ref[rows, ks].astype(F32)
            k = k_ref[rows, ks].astype(F32)
            vb = v_ref[rows, vs]
            b = b_all[:, ks]
            b_mid = b[mid:mid + 1, :]
            b_end = b[chunk - 1:chunk, :]
            qm = (q * jnp.exp(b - b_mid)).astype(BF16)
            km = (k * jnp.exp(b_mid - b)).astype(BF16)
            a = lax.dot_general(qm, km, NT, preferred_element_type=F32)
            a = jnp.where(lower, a, 0.0).astype(BF16)
            st = st_ref[hd]
            qd = (q * jnp.exp(b)).astype(BF16)
            o = jnp.dot(a, vb, preferred_element_type=F32)
            o = o + lax.dot_general(qd, st.astype(BF16), NT, preferred_element_type=F32)
            o_ref[rows, vs] = o.astype(o_ref.dtype)
            ke = (k * jnp.exp(b_end - b)).astype(BF16)
            upd = lax.dot_general(vb, ke, TN, preferred_element_type=F32)
            st_ref[hd] = st * jnp.exp(b_end) + upd

    unroll = 2 if n_inner % 2 == 0 else 1

    def body(i, carry):
        for u in range(unroll):
            chunk_step(i * unroll + u)
        return carry

    lax.fori_loop(0, n_inner // unroll, body, 0)

    @pl.when(lt == pl.num_programs(1) - 1)
    def _():
        for hd in range(H):
            s_ref[0, hd] = st_ref[hd].T


def _gla_rec(qk, v, la, s0, *, seq, chunk, tl):
    B, H, dkh, dvh = s0.shape
    tl = min(tl, seq)
    assert seq % tl == 0 and tl % chunk == 0
    nlt = seq // tl
    kern = functools.partial(_gla_rec_kernel, chunk=chunk, n_inner=tl // chunk)
    return pl.pallas_call(
        kern,
        out_shape=(jax.ShapeDtypeStruct(v.shape, BF16), jax.ShapeDtypeStruct(s0.shape, F32)),
        grid=(B, nlt),
        in_specs=[pl.BlockSpec((tl, H * dkh), lambda b, t: (b * nlt + t, 0)),
                  pl.BlockSpec((tl, H * dkh), lambda b, t: (b * nlt + t, 1)),
                  pl.BlockSpec((tl, H * dvh), lambda b, t: (b * nlt + t, 0)),
                  pl.BlockSpec((tl, H * dkh), lambda b, t: (b * nlt + t, 0)),
                  pl.BlockSpec((1, H, dkh, dvh), lambda b, t: (b, 0, 0, 0))],
        out_specs=(pl.BlockSpec((tl, H * dvh), lambda b, t: (b * nlt + t, 0)),
                   pl.BlockSpec((1, H, dkh, dvh), lambda b, t: (b, 0, 0, 0))),
        scratch_shapes=[pltpu.VMEM((H, dvh, dkh), F32)],
        compiler_params=_params("parallel", "arbitrary"),
        name="gla_recurrence",
    )(qk, qk, v, la, s0)


def _mix_mlp_kernel(*refs, gla):
    if gla:
        a_ref, gt_ref, on_ref = refs[:3]
        refs = refs[3:]
    else:
        a_ref = refs[0]
        refs = refs[1:]
    x_ref, wo_ref, g1_ref, g_ref, sh_ref, sc_ref, g2_ref, wu_ref, wd_ref, out_ref, h_ref = refs
    k = pl.program_id(1)

    @pl.when(k == 0)
    def _():
        tm = x_ref.shape[0]
        n_split = 4 if tm % (4 * LANES) == 0 else 1
        for part in range(n_split):
            rs = slice(part * (tm // n_split), (part + 1) * (tm // n_split))
            if gla:
                dvh = a_ref.shape[1] // GLA_HEADS
                mix = jnp.zeros((tm // n_split, out_ref.shape[1]), F32)
                for hd in range(GLA_HEADS):
                    sl = slice(hd * dvh, (hd + 1) * dvh)
                    y = _head_rms(a_ref[rs, sl].astype(F32), on_ref[...])
                    gt = gt_ref[rs, sl].astype(F32)
                    y = y * (gt * jax.nn.sigmoid(gt))
                    mix = mix + jnp.dot(y.astype(BF16), wo_ref[sl, :], preferred_element_type=F32)
            else:
                mix = jnp.dot(a_ref[rs, :].astype(BF16), wo_ref[...], preferred_element_type=F32)
            x1 = x_ref[rs, :] + _mod_rows(g1_ref, rs) * mix
            out_ref[rs, :] = x1
            h_ref[rs, :] = _prenorm(x1, g_ref[...], _mod_rows(sh_ref, rs), _mod_rows(sc_ref, rs)).astype(BF16)

    u = jnp.dot(h_ref[...], wu_ref[...], preferred_element_type=F32)
    u = jnp.square(jnp.maximum(u, 0.0)).astype(BF16)
    out_ref[...] += g2_ref[0] * jnp.dot(u, wd_ref[...], preferred_element_type=F32)


def _mix_mlp(mix_inputs, x, w_out, gate1, g, shift, scale, gate2, w_up, w_down, *, tm, tf):
    rows, D = x.shape
    FF = w_up.shape[1]
    tm = _row_tile(rows, tm)
    gla = len(mix_inputs) == 3
    row = lambda a: pl.BlockSpec((tm, a.shape[1]), lambda i, k: (i, 0))
    full = lambda a: pl.BlockSpec(a.shape, lambda i, k: (0,) * a.ndim)
    mix_specs = [row(mix_inputs[0]), row(mix_inputs[1]), full(mix_inputs[2])] if gla else [row(mix_inputs[0])]
    return pl.pallas_call(
        functools.partial(_mix_mlp_kernel, gla=gla),
        out_shape=jax.ShapeDtypeStruct((rows, D), F32),
        grid=(rows // tm, FF // tf),
        in_specs=mix_specs + [row(x), full(w_out), _mod_spec(gate1, rows, tm, 2), full(g),
                              _mod_spec(shift, rows, tm, 2), _mod_spec(scale, rows, tm, 2),
                              _mod_spec(gate2, rows, tm, 2),
                              pl.BlockSpec((D, tf), lambda i, k: (0, k)),
                              pl.BlockSpec((tf, D), lambda i, k: (k, 0))],
        out_specs=pl.BlockSpec((tm, D), lambda i, k: (i, 0)),
        scratch_shapes=[pltpu.VMEM((tm, D), BF16)],
        compiler_params=pltpu.CompilerParams(dimension_semantics=("parallel", "arbitrary"),
                                             vmem_limit_bytes=MIX_MLP_VMEM_LIMIT),
        name="gla_out_mlp" if gla else "moba_out_mlp",
    )(*mix_inputs, x, w_out, gate1, g, shift, scale, gate2, w_up, w_down)


def _kv_kernel(x_ref, g_ref, sh_ref, sc_ref, w_ref, kn_ref, k_ref, v_ref, *rest, hd, nsum, seq):
    h = _prenorm(x_ref[...], g_ref[...], sh_ref[0], sc_ref[0]).astype(BF16)
    n = k_ref.shape[1]
    tm = k_ref.shape[0]
    kk = jnp.dot(h, w_ref[:, :n], preferred_element_type=F32)
    vv = jnp.dot(h, w_ref[:, n:], preferred_element_type=F32)
    v_ref[...] = vv
    if nsum:
        ka_ref, va_ref, ks_ref = rest
        pos = (pl.program_id(0) * tm) % seq + lax.broadcasted_iota(jnp.int32, (tm, hd), 0)
        lane = lax.broadcasted_iota(jnp.int32, (tm, hd), 1)
        block_onehot = jnp.where(lane == pos // MOBA_BLOCK, 1.0, 0.0).astype(BF16)
        ones = jnp.ones((tm, hd), BF16)
    for a in range(n // hd):
        sl = slice(a * hd, (a + 1) * hd)
        kh = _head_rms(kk[:, sl], kn_ref[...])
        k_ref[:, sl] = kh
        if nsum:
            ka_ref[:, 2 * a * hd:(2 * a + 1) * hd] = kh.astype(BF16)
            ka_ref[:, (2 * a + 1) * hd:(2 * a + 2) * hd] = block_onehot
            va_ref[:, 2 * a * hd:(2 * a + 1) * hd] = vv[:, sl].astype(BF16)
            va_ref[:, (2 * a + 1) * hd:(2 * a + 2) * hd] = ones
            for r in range(nsum):
                ks_ref[r, :, sl] = jnp.sum(kh[r * MOBA_BLOCK:(r + 1) * MOBA_BLOCK], axis=0, keepdims=True)


def _kv(x, g, shift, scale, w_kv, k_norm, *, tm, seq, for_prompt):
    rows, D = x.shape
    n = w_kv.shape[1] // 2
    hd = k_norm.shape[1]
    tm = _row_tile(rows, tm)
    nsum = tm // MOBA_BLOCK if for_prompt else 0
    assert not for_prompt or (tm % MOBA_BLOCK == 0 and seq // MOBA_BLOCK <= hd)
    row = lambda m: pl.BlockSpec((tm, m), lambda i: (i, 0))
    full = lambda a: pl.BlockSpec(a.shape, lambda i: (0,) * a.ndim)
    out_shape = [jax.ShapeDtypeStruct((rows, n), F32), jax.ShapeDtypeStruct((rows, n), F32)]
    out_specs = [row(n), row(n)]
    if for_prompt:
        out_shape += [jax.ShapeDtypeStruct((rows, 2 * n), BF16), jax.ShapeDtypeStruct((rows, 2 * n), BF16),
                      jax.ShapeDtypeStruct((rows // MOBA_BLOCK, 1, n), F32)]
        out_specs += [row(2 * n), row(2 * n), pl.BlockSpec((nsum, 1, n), lambda i: (i, 0, 0))]
    return pl.pallas_call(
        functools.partial(_kv_kernel, hd=hd, nsum=nsum, seq=seq),
        out_shape=tuple(out_shape),
        grid=(rows // tm,),
        in_specs=[row(D), full(g), _mod_spec(shift, rows, tm, 1), _mod_spec(scale, rows, tm, 1),
                  full(w_kv), full(k_norm)],
        out_specs=tuple(out_specs),
        compiler_params=_params("parallel"),
        name="shared_kv",
    )(x, g, shift, scale, w_kv, k_norm)


def _q_kernel(x_ref, g_ref, sh_ref, sc_ref, w_ref, qn_ref, q_ref, *, hd):
    h = _prenorm(x_ref[...], g_ref[...], sh_ref[0], sc_ref[0]).astype(BF16)
    qq = jnp.dot(h, w_ref[...], preferred_element_type=F32)
    for a in range(q_ref.shape[1] // hd):
        sl = slice(a * hd, (a + 1) * hd)
        q_ref[:, sl] = _head_rms(qq[:, sl], qn_ref[...])


def _q_proj(x, g, shift, scale, w_q, q_norm, *, tm):
    rows, D = x.shape
    n = w_q.shape[1]
    tm = _row_tile(rows, tm)
    row = lambda m: pl.BlockSpec((tm, m), lambda i: (i, 0))
    full = lambda a: pl.BlockSpec(a.shape, lambda i: (0,) * a.ndim)
    return pl.pallas_call(
        functools.partial(_q_kernel, hd=q_norm.shape[1]),
        out_shape=jax.ShapeDtypeStruct((rows, n), F32),
        grid=(rows // tm,),
        in_specs=[row(D), full(g), _mod_spec(shift, rows, tm, 1), _mod_spec(scale, rows, tm, 1),
                  full(w_q), full(q_norm)],
        out_specs=row(n),
        compiler_params=_params("parallel"),
        name="moba_q_proj",
    )(x, g, shift, scale, w_q, q_norm)


def _top_blocks(scores, n_past, axis):
    nbk = scores.shape[axis]
    blk_id = lax.broadcasted_iota(jnp.int32, scores.shape, axis)
    sc = jnp.where(blk_id < n_past, scores, -jnp.inf)
    picks = []
    for r in range(MOBA_TOPK):
        m = jnp.max(sc, axis=axis, keepdims=True)
        idx = jnp.min(jnp.where(sc == m, blk_id, nbk), axis=axis, keepdims=True)
        picks.append((idx, (jnp.zeros_like(idx) + r) < n_past))
        sc = jnp.where(blk_id == idx, -jnp.inf, sc)
    return picks


def _q_sel_kernel(x_ref, g_ref, sh_ref, sc_ref, w_ref, qn_ref, ks_ref, qa_ref, *, hd, seq, scale):
    tm = x_ref.shape[0]
    nbk = ks_ref.shape[1]
    h = _prenorm(x_ref[...], g_ref[...], sh_ref[0], sc_ref[0]).astype(BF16)
    qq = jnp.dot(h, w_ref[...], preferred_element_type=F32)
    pos = (pl.program_id(0) * tm) % seq + lax.broadcasted_iota(jnp.int32, (1, tm), 1)
    own = pos // MOBA_BLOCK
    blk_id = lax.broadcasted_iota(jnp.int32, (nbk, tm), 0)
    for a in range(ATT_HEADS):
        sl = slice(a * hd, (a + 1) * hd)
        qh = _head_rms(qq[:, sl], qn_ref[...])
        scores = _dot_nt_3pass(ks_ref[0, :, sl] * (1.0 / MOBA_BLOCK), qh)
        chosen = blk_id < 0
        for idx, ok in _top_blocks(scores, own, 0):
            chosen = chosen | ((blk_id == idx) & ok)
        is_prev = blk_id == own - 1
        far = jnp.where(chosen & jnp.logical_not(is_prev), 0.0, NEG)
        prev = jnp.max(jnp.where(chosen & is_prev, 0.0, NEG), axis=0, keepdims=True)
        mask_t = jnp.concatenate([far, jnp.zeros((hd - nbk - SUBLANES, tm), F32),
                                  jnp.broadcast_to(prev, (SUBLANES, tm))], axis=0)
        qa_ref[:, 2 * a * hd:(2 * a + 1) * hd] = (qh * scale).astype(BF16)
        qa_ref[:, (2 * a + 1) * hd:(2 * a + 2) * hd] = mask_t.T.astype(BF16)


def _q_proj_sel(x, g, shift, scale, w_q, q_norm, ksum, *, tm, seq):
    rows, D = x.shape
    n = w_q.shape[1]
    hd = q_norm.shape[1]
    B, nbk, _ = ksum.shape
    tm = _row_tile(rows, tm)
    tiles_per_seq = seq // tm
    assert seq % tm == 0 and tm % LANES == 0 and nbk % SUBLANES == 0 and nbk + SUBLANES <= hd
    row = lambda m: pl.BlockSpec((tm, m), lambda i: (i, 0))
    full = lambda a: pl.BlockSpec(a.shape, lambda i: (0,) * a.ndim)
    return pl.pallas_call(
        functools.partial(_q_sel_kernel, hd=hd, seq=seq, scale=hd ** -0.5),
        out_shape=jax.ShapeDtypeStruct((rows, 2 * n), BF16),
        grid=(rows // tm,),
        in_specs=[row(D), full(g), _mod_spec(shift, rows, tm, 1), _mod_spec(scale, rows, tm, 1),
                  full(w_q), full(q_norm),
                  pl.BlockSpec((1, nbk, n), lambda i: (i // tiles_per_seq, 0, 0))],
        out_specs=row(2 * n),
        compiler_params=_params("parallel"),
        name="moba_q_proj_select",
    )(x, g, shift, scale, w_q, q_norm, ksum)


def _moba_prompt_kernel(qa_ref, ka_ref, va_ref, bias_ref, o_ref, m_ref, acc_ref,
                        sa_ref, sb_ref, pa_ref, pb_ref, al_ref, pn_ref, m0_ref,
                        *, blk, group, n_groups, halves, heads):
    it = pl.program_id(2)
    hd = qa_ref.shape[1] // (2 * heads)
    last_group = n_groups - 1

    def lanes(hh):
        return slice(hh * 2 * hd, (hh + 1) * 2 * hd)

    def rows_of(g):
        return pl.ds(pl.multiple_of(g * (group * blk), group * blk), group * blk)

    def scores_of(hh, g):
        return lax.dot_general(qa_ref[:, lanes(hh)], ka_ref[rows_of(g), lanes(hh)], NT,
                               preferred_element_type=F32)

    for hh in range(heads):
        sa_ref[hh] = scores_of(hh, 0)

    t_io = lax.broadcasted_iota(jnp.int32, (blk, blk), 0)
    s_io = lax.broadcasted_iota(jnp.int32, (blk, blk), 1)

    def near_rows(hf):
        own = it * halves + hf
        return (pl.ds(pl.multiple_of(jnp.maximum(own - 1, 0) * blk, blk), blk),
                pl.ds(pl.multiple_of(own * blk, blk), blk))

    for hh, hf in [(hh, hf) for hf in range(halves) for hh in range(heads)]:
        rq = slice(hf * blk, (hf + 1) * blk)
        r_prev, r_own = near_rows(hf)
        k_lanes = slice(hh * 2 * hd, hh * 2 * hd + hd)
        qb = qa_ref[rq, k_lanes]
        prev_mask = qa_ref[rq, (hh + 1) * 2 * hd - 1:(hh + 1) * 2 * hd].astype(F32)
        s_own = lax.dot_general(qb, ka_ref[r_own, k_lanes], NT, preferred_element_type=F32)
        s_own = jnp.where(s_io <= t_io, s_own + bias_ref[hh, 0], NEG)
        s_prev = lax.dot_general(qb, ka_ref[r_prev, k_lanes], NT, preferred_element_type=F32)
        s_prev = s_prev + bias_ref[hh, 1] + prev_mask
        m0 = jnp.maximum(jnp.max(s_own, axis=1, keepdims=True), jnp.max(s_prev, axis=1, keepdims=True))
        pn_ref[hh, rq, :blk] = jnp.exp(s_prev - m0).astype(BF16)
        pn_ref[hh, rq, blk:] = jnp.exp(s_own - m0).astype(BF16)
        m_ref[hh, rq, :] = m0
        m0_ref[hh, rq, :] = m0
    acc_ref[...] = jnp.zeros_like(acc_ref)

    def softmax(hh, s_ref, p_ref, slot):
        m_prev = m_ref[hh]
        s = s_ref[hh]
        m_new = jnp.maximum(m_prev, jnp.max(s, axis=1, keepdims=True))
        p_ref[hh] = jnp.exp(s - m_new).astype(BF16)
        al_ref[hh, slot] = jnp.exp(m_prev - m_new)
        m_ref[hh] = m_new

    def flush(hh, p_ref, slot, g):
        acc_ref[hh] = al_ref[hh, slot] * acc_ref[hh] + jnp.dot(p_ref[hh], va_ref[rows_of(g), lanes(hh)],
                                                               preferred_element_type=F32)

    n_far = it * halves + halves - 2
    n_pairs = ((n_far + group - 1) // group + 1) // 2
    pb_ref[...] = jnp.zeros_like(pb_ref)
    for hh in range(heads):
        al_ref[hh, 1] = jnp.ones(al_ref.shape[2:], F32)

    def body(t, carry):
        for hh in range(heads):
            sb_ref[hh] = scores_of(hh, 2 * t + 1)
            flush(hh, pb_ref, 1, jnp.maximum(2 * t - 1, 0))
            softmax(hh, sa_ref, pa_ref, 0)
        for hh in range(heads):
            sa_ref[hh] = scores_of(hh, jnp.minimum(2 * t + 2, last_group))
            flush(hh, pa_ref, 0, 2 * t)
            softmax(hh, sb_ref, pb_ref, 1)
        return carry

    lax.fori_loop(0, n_pairs, body, 0)

    for hh in range(heads):
        flush(hh, pb_ref, 1, jnp.maximum(2 * n_pairs - 1, 0))
    for hh, hf in [(hh, hf) for hf in range(halves) for hh in range(heads)]:
        rq = slice(hf * blk, (hf + 1) * blk)
        r_prev, r_own = near_rows(hf)
        near = jnp.dot(pn_ref[hh, rq, :blk], va_ref[r_prev, lanes(hh)], preferred_element_type=F32)
        near = near + jnp.dot(pn_ref[hh, rq, blk:], va_ref[r_own, lanes(hh)], preferred_element_type=F32)
        acc = acc_ref[hh, rq, :] + jnp.exp(m0_ref[hh, rq, :] - m_ref[hh, rq, :]) * near
        o_ref[rq, hh * hd:(hh + 1) * hd] = (acc[:, :hd] / acc[:, hd:]).astype(o_ref.dtype)


def _moba_prompt(qa, ka, va, bias, *, batch, seq, group=4, halves=2, heads=1):
    rows, n2 = qa.shape
    hd = n2 // (2 * ATT_HEADS)
    blk = MOBA_BLOCK
    nq = seq // blk
    assert nq % (2 * group) == 0 and group >= 2 and nq % halves == 0 and ATT_HEADS % heads == 0
    nt = nq // halves
    tq = halves * blk
    kern = functools.partial(_moba_prompt_kernel, blk=blk, group=group, n_groups=nq // group,
                             halves=halves, heads=heads)
    return pl.pallas_call(
        kern,
        out_shape=jax.ShapeDtypeStruct((rows, ATT_HEADS * hd), BF16),
        grid=(batch, ATT_HEADS // heads, nt),
        in_specs=[pl.BlockSpec((tq, heads * 2 * hd), lambda b, h, i: (b * nt + i, h)),
                  pl.BlockSpec((seq, heads * 2 * hd), lambda b, h, i: (b, h)),
                  pl.BlockSpec((seq, heads * 2 * hd), lambda b, h, i: (b, h)),
                  pl.BlockSpec((heads, 2, blk, blk), lambda b, h, i: (h, 0, 0, 0))],
        out_specs=pl.BlockSpec((tq, heads * hd), lambda b, h, i: (b * nt + i, h)),
        scratch_shapes=[pltpu.VMEM((heads, tq, 1), F32), pltpu.VMEM((heads, tq, 2 * hd), F32),
                        pltpu.VMEM((heads, tq, group * blk), F32), pltpu.VMEM((heads, tq, group * blk), F32),
                        pltpu.VMEM((heads, tq, group * blk), BF16), pltpu.VMEM((heads, tq, group * blk), BF16),
                        pltpu.VMEM((heads, 2, tq, 1), F32), pltpu.VMEM((heads, tq, 2 * blk), BF16),
                        pltpu.VMEM((heads, tq, 1), F32)],
        compiler_params=_params("parallel", "parallel", "arbitrary"),
        name="moba_prompt_attention",
    )(qa, ka, va, bias)


def _page_sum_kernel(pt_ref, *refs, ppb):
    o_ref = refs[-1]
    pages = refs[:-1]
    for r in range(len(pages) // ppb):
        total = jnp.sum(pages[r * ppb][0], axis=0)
        for p in range(1, ppb):
            total = total + jnp.sum(pages[r * ppb + p][0], axis=0)
        o_ref[0, r] = total


def _sample_block_sums(cache_k, pt_flat, *, dec_batch, n_pages, blocks_per_step=4):
    _, page, H, hd = cache_k.shape
    ppb = MOBA_BLOCK // page
    nbk = n_pages // ppb
    bps = math.gcd(blocks_per_step, nbk)
    nper = bps * ppb

    def page_spec(p):
        return pl.BlockSpec((1, page, H, hd), lambda d, j, pt: (pt[d * n_pages + j * nper + p], 0, 0, 0))

    return pl.pallas_call(
        functools.partial(_page_sum_kernel, ppb=ppb),
        out_shape=jax.ShapeDtypeStruct((dec_batch, nbk, H, hd), F32),
        grid_spec=pltpu.PrefetchScalarGridSpec(
            num_scalar_prefetch=1,
            grid=(dec_batch, nbk // bps),
            in_specs=[page_spec(p) for p in range(nper)],
            out_specs=pl.BlockSpec((1, bps, H, hd), lambda d, j, pt: (d, j, 0, 0))),
        compiler_params=_params("parallel", "arbitrary"),
        name="sample_block_sums",
    )(pt_flat, *([cache_k] * nper))


def _sample_select_kernel(q_ref, ks_ref, idx_ref, *, hd, blk):
    nbk = ks_ref.shape[1]
    lane = lax.broadcasted_iota(jnp.int32, idx_ref.shape[1:], 1)
    out = jnp.zeros(idx_ref.shape[1:], jnp.int32)
    for a in range(ATT_HEADS):
        means = ks_ref[0, :, a, :] * (1.0 / blk)
        scores = _dot_nt_3pass(q_ref[0, :, a * hd:(a + 1) * hd], means)
        for r, (idx, _) in enumerate(_top_blocks(scores, nbk, 1)):
            out = jnp.where(lane == a * MOBA_TOPK + r, idx, out)
    idx_ref[0] = out


def _sample_select(q8, ksum):
    DB, R, n = q8.shape
    _, nbk, H, hd = ksum.shape
    assert nbk >= MOBA_TOPK and H == ATT_HEADS
    kern = functools.partial(_sample_select_kernel, hd=hd, blk=MOBA_BLOCK)
    return pl.pallas_call(
        kern,
        out_shape=jax.ShapeDtypeStruct((DB, R, LANES), jnp.int32),
        grid=(DB,),
        in_specs=[pl.BlockSpec((1, R, n), lambda d: (d, 0, 0)),
                  pl.BlockSpec((1, nbk, H, hd), lambda d: (d, 0, 0, 0))],
        out_specs=pl.BlockSpec((1, R, LANES), lambda d: (d, 0, 0)),
        compiler_params=_params("parallel"),
        name="sample_select",
    )(q8, ksum)


def _sample_attn_kernel(pt_ref, sel_ref, q_ref, kn_ref, vn_ref, bias_ref, ck_ref, cv_ref, o_ref,
                        kbuf, vbuf, sem, *, lq_n, ppb, n_pages, page, scale):
    g = pl.program_id(0)
    nslot = lq_n * MOBA_TOPK
    nsl = nslot * ppb

    def copies(step, slot):
        d = step // ATT_HEADS
        a = step % ATT_HEADS
        out = []
        for j in range(nsl):
            phys = pt_ref[d * n_pages + sel_ref[step * nslot + j // ppb] * ppb + j % ppb]
            rows = pl.ds(j * page, page)
            out.append(pltpu.make_async_copy(ck_ref.at[phys, :, a, :], kbuf.at[slot, rows, :], sem.at[slot, 0, j]))
            out.append(pltpu.make_async_copy(cv_ref.at[phys, :, a, :], vbuf.at[slot, rows, :], sem.at[slot, 1, j]))
        return out

    @pl.when(g == 0)
    def _():
        for c in copies(0, 0):
            c.start()

    @pl.when(g + 1 < pl.num_programs(0))
    def _():
        for c in copies(g + 1, (g + 1) % 2):
            c.start()

    slot = g % 2
    q = q_ref[0] * scale
    row = lax.broadcasted_iota(jnp.int32, (q.shape[0], 1), 0)

    new_logits = []
    for s in range(lq_n):
        sc = jnp.sum(q * kn_ref[0, s:s + 1, :], axis=1, keepdims=True) + bias_ref[0, 8:16, s:s + 1]
        new_logits.append(jnp.where(s <= row, sc, NEG))

    for c in copies(g, slot):
        c.wait()

    s_all = lax.dot_general(q.astype(BF16), kbuf[slot].astype(BF16), NT, preferred_element_type=F32)
    parts = []
    for j in range(nsl):
        lp = sel_ref[g * nslot + j // ppb] * ppb + j % ppb
        is_last = (jnp.zeros((q.shape[0], page), jnp.int32) + lp) == n_pages - 1
        sj = s_all[:, j * page:(j + 1) * page] + jnp.where(is_last, bias_ref[0, 0:8, :], 0.0)
        parts.append(jnp.where(row == j // (MOBA_TOPK * ppb), sj, NEG))
    m = jnp.max(functools.reduce(jnp.maximum, parts), axis=1, keepdims=True)
    m = functools.reduce(jnp.maximum, new_logits, m)
    probs = [jnp.exp(p - m) for p in parts]
    den = jnp.sum(functools.reduce(jnp.add, probs), axis=1, keepdims=True)
    pcat = jnp.concatenate([p.astype(BF16) for p in probs], axis=1)
    acc = jnp.dot(pcat, vbuf[slot].astype(BF16), preferred_element_type=F32)
    for s in range(lq_n):
        e = jnp.exp(new_logits[s] - m)
        den = den + e
        acc = acc + e * vn_ref[0, s:s + 1, :]
    o_ref[0] = acc / den


def _sample_attn(q8, kn8, vn8, bias, cache_k, cache_v, pt_flat, sel_flat, *, lq_n, n_pages):
    DB, R, n = q8.shape
    _, page, H, hd = cache_k.shape
    ppb = MOBA_BLOCK // page
    nsl = lq_n * MOBA_TOPK * ppb
    assert page == LANES and R == SUBLANES and H == ATT_HEADS
    kern = functools.partial(_sample_attn_kernel, lq_n=lq_n, ppb=ppb, n_pages=n_pages, page=page,
                             scale=hd ** -0.5)
    head = pl.BlockSpec((1, R, hd), lambda g, pt, sel: (g // ATT_HEADS, 0, g % ATT_HEADS))
    return pl.pallas_call(
        kern,
        out_shape=jax.ShapeDtypeStruct((DB, R, n), F32),
        grid_spec=pltpu.PrefetchScalarGridSpec(
            num_scalar_prefetch=2,
            grid=(DB * ATT_HEADS,),
            in_specs=[head, head, head,
                      pl.BlockSpec((1, 2 * SUBLANES, LANES), lambda g, pt, sel: (g % ATT_HEADS, 0, 0)),
                      pl.BlockSpec(memory_space=pl.ANY), pl.BlockSpec(memory_space=pl.ANY)],
            out_specs=head,
            scratch_shapes=[pltpu.VMEM((2, nsl * page, hd), F32), pltpu.VMEM((2, nsl * page, hd), F32),
                            pltpu.SemaphoreType.DMA((2, 2, nsl))]),
        compiler_params=_params("arbitrary"),
        name="sample_attention",
    )(pt_flat, sel_flat, q8, kn8, vn8, bias, cache_k, cache_v)


def _bucket_of_distance():
    n = np.arange(MAX_DISTANCE)
    max_exact = N_BUCKETS // 2
    nf = np.maximum(n, max_exact).astype(np.float32)
    large = max_exact + (np.log(nf / np.float32(max_exact)) / np.float32(math.log(MAX_DISTANCE / max_exact))
                         * np.float32(N_BUCKETS - max_exact)).astype(np.int32)
    return np.where(n < max_exact, n, np.minimum(large, N_BUCKETS - 1)).astype(np.int32)


def _dist_table(rel_bias, width):
    t = rel_bias.astype(F32)[_bucket_of_distance()].T
    t = t - t[:, MAX_DISTANCE - 1:]
    return jnp.pad(t, ((0, 0), (0, width - MAX_DISTANCE)))


def _toeplitz(first_row_wrapped, n):
    H = first_row_wrapped.shape[0]
    x = jnp.broadcast_to(first_row_wrapped[:, None, :], (H, n, 2 * n)).reshape(H, 2 * n * n)
    return x[:, :n * (2 * n - 1)].reshape(H, n, 2 * n - 1)[:, :, :n]


def _prompt_bias_tiles(rel_bias):
    blk = MOBA_BLOCK
    assert blk >= MAX_DISTANCE
    t = _dist_table(rel_bias, blk)
    rev = jnp.flip(t[:, 1:], axis=1)
    zeros = jnp.zeros_like(t)
    own = _toeplitz(jnp.concatenate([t[:, :1], zeros, rev], axis=1), blk)
    prev = _toeplitz(jnp.concatenate([zeros[:, :1], rev, zeros], axis=1), blk)
    return jnp.stack([own, prev], axis=1)


def _sample_bias_rows(rel_bias, *, page, lq_n):
    assert page == LANES and page >= MAX_DISTANCE and lq_n <= SUBLANES
    t = _dist_table(rel_bias, 2 * page)
    r = np.arange(SUBLANES)[:, None]
    c = np.arange(LANES)[None, :]
    return jnp.concatenate([t[:, page + r - c], t[:, np.clip(r - c, 0, None)]], axis=1)


def _hi_lo(w):
    hi = w.astype(BF16)
    return jnp.stack([hi, (w - hi.astype(F32)).astype(BF16)], axis=1)


def _pad_tokens(a, groups, per, to):
    return jnp.pad(a.reshape(groups, per, a.shape[-1]), ((0, 0), (0, to - per), (0, 0)))


def _trunk(x, mods, kvmod, s0, W, *, batch, seq, tm, attend, make_ctx):
    n_a = W['gla_w_main'].shape[0]
    depth = W['mlp_w_up'].shape[0]
    dk = W['gla_b_gate'].shape[2]
    dv = W['gla_w_out'].shape[1]
    states = []
    k = v = ctx = None
    for i in range(depth):
        sh1, sc1, g1, sh2, sc2, g2 = mods[i]
        if i == n_a:
            k, v, ctx = make_ctx(x, kvmod)
        if i < n_a:
            qk, vv, gt, la = _gla_in(x, W['norm_mix'][i], sh1, sc1, W['gla_w_main'][i], W['gla_w_r'][i],
                                     W['gla_w_g2'][i], W['gla_b_gate'][i], dk=dk, dv=dv, tm=tm)
            chunk = math.gcd(GLA_CHUNK, seq)
            if chunk < 16:
                padded = 16
                qk, vv, la = (_pad_tokens(t, batch, seq, padded).reshape(batch * padded, -1)
                              for t in (qk, vv, la))
                o, s = _gla_rec(qk, vv, la, s0[i], seq=padded, chunk=padded, tl=padded)
                o = o.reshape(batch, padded, dv)[:, :seq].reshape(batch * seq, dv)
            else:
                o, s = _gla_rec(qk, vv, la, s0[i], seq=seq, chunk=chunk, tl=512)
            states.append(s)
            mix_inputs, w_mix = (o, gt, W['gla_onorm'][i]), W['gla_w_out'][i]
        else:
            j = i - n_a
            q_args = (W['norm_mix'][i], sh1, sc1, W['moba_w_q'][j], W['q_norm'][j])
            mix_inputs, w_mix = (attend(x, q_args, ctx),), W['moba_w_o'][j]
        x = _mix_mlp(mix_inputs, x, w_mix, g1, W['norm_mlp'][i], sh2, sc2, g2,
                     W['mlp_w_up'][i], W['mlp_w_down'][i], tm=min(tm * 2, 1024), tf=1024)
    return x, jnp.stack(states), k, v


def kernel(x_prompt, x_sample, state_gla, cache_k, cache_v, page_table, c_prompt, c_sample,
           ada_w, ada_b, norm_mix, norm_mlp, gla_w_in, gla_w_gate2, gla_b_gate, gla_onorm, gla_w_out,
           kv_ada_w, kv_ada_b, kv_norm, w_kv, k_norm, moba_w_q, q_norm, moba_w_o, rel_bias,
           mlp_w_up, mlp_w_down):
    B, L, D = x_prompt.shape
    DB, LQ, _ = x_sample.shape
    depth = ada_w.shape[0]
    n_a = gla_w_in.shape[0]
    rank, dk = gla_w_gate2.shape[1:]
    dv = gla_w_out.shape[1]
    n_pool, page, H, hd = cache_k.shape
    n_pages = page_table.shape[1]
    past = n_pages * page
    assert H == ATT_HEADS and L % MOBA_BLOCK == 0 and MOBA_BLOCK % page == 0
    assert past % MOBA_BLOCK == 0 and LQ <= min(SUBLANES, MOBA_BLOCK) and rank <= LANES

    W = {
        'norm_mix': norm_mix.reshape(depth, 1, D), 'norm_mlp': norm_mlp.reshape(depth, 1, D),
        'gla_w_main': gla_w_in[:, :, :2 * dk + 2 * dv].astype(BF16),
        'gla_w_r': jnp.pad(gla_w_in[:, :, 2 * dk + 2 * dv:], ((0, 0), (0, 0), (0, LANES - rank))).astype(BF16),
        'gla_w_g2': _hi_lo(jnp.pad(gla_w_gate2, ((0, 0), (0, LANES - rank), (0, 0)))),
        'gla_b_gate': gla_b_gate.reshape(n_a, 1, dk), 'gla_onorm': gla_onorm.reshape(n_a, 1, -1),
        'gla_w_out': gla_w_out.astype(BF16),
        'moba_w_q': moba_w_q.astype(BF16), 'q_norm': q_norm.reshape(-1, 1, hd),
        'moba_w_o': moba_w_o.astype(BF16),
        'mlp_w_up': mlp_w_up.astype(BF16), 'mlp_w_down': mlp_w_down.astype(BF16),
    }
    w_kv_b = w_kv.astype(BF16)
    kv_norm2 = kv_norm.reshape(1, D)
    k_norm2 = k_norm.reshape(1, hd)

    n_c = B + DB
    c_all = jnp.pad(jnp.concatenate([c_prompt, c_sample], axis=0), ((0, -n_c % SUBLANES), (0, 0)))
    mod = _ada(c_all, ada_w, ada_b)
    kvm = _ada(c_all, kv_ada_w[None], kv_ada_b[None])[0]

    def prompt_mod(m):
        return m[:B].reshape(B, 1, D)

    def sample_mod(m):
        return jnp.repeat(m[B:n_c], LQ, axis=0).reshape(1, DB * LQ, D)

    def split_mods(pick):
        layers = [[pick(mod[i, :, t * D:(t + 1) * D]) for t in range(6)] for i in range(depth)]
        return layers, [pick(kvm[:, t * D:(t + 1) * D]) for t in range(2)]

    p_mods, p_kvmod = split_mods(prompt_mod)
    p_bias = _prompt_bias_tiles(rel_bias)

    def prompt_ctx(x, kvmod):
        k, v, ka, va, ksum = _kv(x, kv_norm2, kvmod[0], kvmod[1], w_kv_b, k_norm2, tm=512, seq=L,
                                 for_prompt=True)
        return k, v, (ka, va, ksum.reshape(B, L // MOBA_BLOCK, H * hd))

    def prompt_attend(x, q_args, ctx):
        ka, va, ksum = ctx
        qa = _q_proj_sel(x, *q_args, ksum, tm=512, seq=L)
        return _moba_prompt(qa, ka, va, p_bias, batch=B, seq=L)

    s0_p = jnp.zeros((n_a, B) + state_gla.shape[2:], F32)
    y_p, st_p, k_p, v_p = _trunk(x_prompt.reshape(B * L, D), p_mods, p_kvmod, s0_p, W, batch=B, seq=L,
                                 tm=512, attend=prompt_attend, make_ctx=prompt_ctx)

    s_mods, s_kvmod = split_mods(sample_mod)
    s_bias = _sample_bias_rows(rel_bias, page=page, lq_n=LQ)
    pt_flat = page_table.reshape(-1).astype(jnp.int32)
    rows_s = DB * LQ

    def sample_ctx(x, kvmod):
        k, v = _kv(x, kv_norm2, kvmod[0], kvmod[1], w_kv_b, k_norm2, tm=rows_s, seq=LQ, for_prompt=False)
        ksum = _sample_block_sums(cache_k, pt_flat, dec_batch=DB, n_pages=n_pages)
        return k, v, (_pad_tokens(k, DB, LQ, SUBLANES), _pad_tokens(v, DB, LQ, SUBLANES), ksum)

    def sample_attend(x, q_args, ctx):
        kn8, vn8, ksum = ctx
        q8 = _pad_tokens(_q_proj(x, *q_args, tm=rows_s), DB, LQ, SUBLANES)
        picks = _sample_select(q8, ksum)[:, :LQ, :ATT_HEADS * MOBA_TOPK]
        sel_flat = picks.reshape(DB, LQ, ATT_HEADS, MOBA_TOPK).transpose(0, 2, 1, 3).reshape(-1)
        o8 = _sample_attn(q8, kn8, vn8, s_bias, cache_k, cache_v, pt_flat, sel_flat,
                          lq_n=LQ, n_pages=n_pages)
        return o8[:, :LQ].reshape(rows_s, H * hd)

    y_s, st_s, k_s, v_s = _trunk(x_sample.reshape(rows_s, D), s_mods, s_kvmod, state_gla.astype(F32), W,
                                 batch=DB, seq=LQ, tm=rows_s, attend=sample_attend, make_ctx=sample_ctx)

    return (y_p.reshape(B, L, D), y_s.reshape(DB, LQ, D), st_p, st_s,
            k_p.reshape(B, L, H, hd), v_p.reshape(B, L, H, hd),
            k_s.reshape(DB, LQ, H, hd), v_s.reshape(DB, LQ, H, hd))
```

```python
import functools
import math

import numpy as np
import jax
import jax.numpy as jnp
from jax import lax
from jax.experimental import pallas as pl
from jax.experimental.pallas import tpu as pltpu

F32 = jnp.float32
BF16 = jnp.bfloat16

GLA_HEADS = 4
GLA_GATE_NORM = 16.0
GLA_CHUNK = 64
ATT_HEADS = 8
MOBA_BLOCK = 256
MOBA_TOPK = 3
MOBA_Q_TILE_BLOCKS = 2
N_BUCKETS = 32
MAX_DISTANCE = 128
EPS = 1e-6

LANES = 128
SUBLANES = 8
VMEM_LIMIT = 48 * 1024 * 1024
MIX_MLP_VMEM_LIMIT = 56 * 1024 * 1024
NEG = -1e30
NT = (((1,), (1,)), ((), ()))
TN = (((0,), (0,)), ((), ()))


def _params(*sem):
    return pltpu.CompilerParams(dimension_semantics=sem, vmem_limit_bytes=VMEM_LIMIT)


def _prenorm(x, g, shift, scale):
    ms = jnp.mean(x * x, axis=-1, keepdims=True)
    return (x * lax.rsqrt(ms + EPS)) * g * (1.0 + scale) + shift


def _head_rms(y, g):
    return y * lax.rsqrt(jnp.mean(y * y, axis=-1, keepdims=True) + EPS) * g


def _dot_nt_3pass(a, b):
    a_hi = a.astype(BF16)
    b_hi = b.astype(BF16)
    a_lo = (a - a_hi.astype(F32)).astype(BF16)
    b_lo = (b - b_hi.astype(F32)).astype(BF16)
    dot = functools.partial(lax.dot_general, dimension_numbers=NT, preferred_element_type=F32)
    return dot(a_hi, b_hi) + dot(a_lo, b_hi) + dot(a_hi, b_lo)


def _mod_rows(mod_ref, rows):
    return mod_ref[0] if mod_ref.shape[1] == 1 else mod_ref[0, rows, :]


def _row_tile(rows, want):
    tm = min(rows, want)
    assert rows % tm == 0
    return tm


def _mod_spec(mod, rows, tm, ngrid):
    G, R, D = mod.shape
    tiles_per_group = rows // (G * tm)
    assert tiles_per_group * G * tm == rows and R in (1, tm)
    if ngrid == 1:
        return pl.BlockSpec((1, R, D), lambda i: (i // tiles_per_group, 0, 0))
    return pl.BlockSpec((1, R, D), lambda i, k: (i // tiles_per_group, 0, 0))


def _ada_kernel(c_ref, w_ref, b_ref, o_ref):
    c = c_ref[...]
    a = (c * jax.nn.sigmoid(c)).astype(BF16)
    o_ref[...] = jnp.dot(a, w_ref[...].astype(BF16), preferred_element_type=F32) + b_ref[...]


def _ada(c, w, b, tn=1024):
    NL, D, N = w.shape
    M = c.shape[0]
    return pl.pallas_call(
        _ada_kernel,
        out_shape=jax.ShapeDtypeStruct((NL, M, N), F32),
        grid=(NL, N // tn),
        in_specs=[pl.BlockSpec((M, D), lambda l, j: (0, 0)),
                  pl.BlockSpec((None, D, tn), lambda l, j: (l, 0, j)),
                  pl.BlockSpec((None, 1, tn), lambda l, j: (l, 0, j))],
        out_specs=pl.BlockSpec((None, M, tn), lambda l, j: (l, 0, j)),
        compiler_params=_params("parallel", "parallel"),
        name="ada_modulation",
    )(c, w, b.reshape(NL, 1, N))


def _gla_in_kernel(x_ref, g_ref, sh_ref, sc_ref, w_ref, wr_ref, wg2_ref, bg_ref,
                   qk_ref, v_ref, gt_ref, la_ref, *, dk, dv, qscale):
    dot = functools.partial(jnp.dot, preferred_element_type=F32)
    tm = x_ref.shape[0]
    n_split = 2 if tm % (2 * LANES) == 0 else 1
    for part in range(n_split):
        rs = slice(part * (tm // n_split), (part + 1) * (tm // n_split))
        h = _prenorm(x_ref[rs, :], g_ref[...], _mod_rows(sh_ref, rs), _mod_rows(sc_ref, rs)).astype(BF16)
        r = dot(h, wr_ref[...])
        r_hi = r.astype(BF16)
        r_lo = (r - r_hi.astype(F32)).astype(BF16)
        z = dot(r_hi, wg2_ref[0]) + dot(r_lo, wg2_ref[0]) + dot(r_hi, wg2_ref[1]) + bg_ref[...]
        log_sig = jnp.minimum(z, 0.0) - jnp.log1p(jnp.exp(-jnp.abs(z)))
        la_ref[rs, :] = log_sig * (1.0 / GLA_GATE_NORM)
        qk_ref[rs, :dk] = (dot(h, w_ref[:, :dk]) * qscale).astype(qk_ref.dtype)
        qk_ref[rs, dk:] = dot(h, w_ref[:, dk:2 * dk]).astype(qk_ref.dtype)
        v_ref[rs, :] = dot(h, w_ref[:, 2 * dk:2 * dk + dv]).astype(v_ref.dtype)
        gt_ref[rs, :] = dot(h, w_ref[:, 2 * dk + dv:]).astype(gt_ref.dtype)


def _gla_in(x, g, shift, scale, w_main, w_r, w_g2, b_g, *, dk, dv, tm):
    rows, D = x.shape
    tm = _row_tile(rows, tm)
    kern = functools.partial(_gla_in_kernel, dk=dk, dv=dv, qscale=(dk // GLA_HEADS) ** -0.5)
    full = lambda a: pl.BlockSpec(a.shape, lambda i: (0,) * a.ndim)
    row = lambda n: pl.BlockSpec((tm, n), lambda i: (i, 0))
    return pl.pallas_call(
        kern,
        out_shape=(jax.ShapeDtypeStruct((rows, 2 * dk), BF16), jax.ShapeDtypeStruct((rows, dv), BF16),
                   jax.ShapeDtypeStruct((rows, dv), BF16), jax.ShapeDtypeStruct((rows, dk), F32)),
        grid=(rows // tm,),
        in_specs=[row(D), full(g), _mod_spec(shift, rows, tm, 1), _mod_spec(scale, rows, tm, 1),
                  full(w_main), full(w_r), full(w_g2), full(b_g)],
        out_specs=(row(2 * dk), row(dv), row(dv), row(dk)),
        compiler_params=_params("parallel"),
        name="gla_in_proj",
    )(x, g, shift, scale, w_main, w_r, w_g2, b_g)


def _gla_rec_kernel(q_ref, k_ref, v_ref, la_ref, s0_ref, o_ref, s_ref, st_ref, *, chunk, n_inner):
    lt = pl.program_id(1)
    H, dvh, dkh = st_ref.shape

    @pl.when(lt == 0)
    def _():
        for hd in range(H):
            st_ref[hd] = s0_ref[0, hd].T

    r_io = lax.broadcasted_iota(jnp.int32, (chunk, chunk), 0)
    c_io = lax.broadcasted_iota(jnp.int32, (chunk, chunk), 1)
    lower = r_io >= c_io
    tri = jnp.where(lower, 1.0, 0.0).astype(BF16)
    mid = chunk // 2 - 1

    def chunk_step(c):
        rows = pl.ds(pl.multiple_of(c * chunk, chunk), chunk)
        la = la_ref[rows, :]
        la_hi = la.astype(BF16)
        rest = la - la_hi.astype(F32)
        la_mid = rest.astype(BF16)
        la_lo = (rest - la_mid.astype(F32)).astype(BF16)
        b_all = sum(jnp.dot(tri, part, preferred_element_type=F32) for part in (la_hi, la_mid, la_lo))
        for hd in range(H):
            ks = slice(hd * dkh, (hd + 1) * dkh)
            vs = slice(hd * dvh, (hd + 1) * dvh)
            q = q_ref[rows, ks].astype(F32)
            k = k_ref[rows, ks].astype(F32)
            vb = v_ref[rows, vs]
            b = b_all[:, ks]
            b_mid = b[mid:mid + 1, :]
            b_end = b[chunk - 1:chunk, :]
            qm = (q * jnp.exp(b - b_mid)).astype(BF16)
            km = (k * jnp.exp(b_mid - b)).astype(BF16)
            a = lax.dot_general(qm, km, NT, preferred_element_type=F32)
            a = jnp.where(lower, a, 0.0).astype(BF16)
            st = st_ref[hd]
            qd = (q * jnp.exp(b)).astype(BF16)
            o = jnp.dot(a, vb, preferred_element_type=F32)
            o = o + lax.dot_general(qd, st.astype(BF16), NT, preferred_element_type=F32)
            o_ref[rows, vs] = o.astype(o_ref.dtype)
            ke = (k * jnp.exp(b_end - b)).astype(BF16)
            upd = lax.dot_general(vb, ke, TN, preferred_element_type=F32)
            st_ref[hd] = st * jnp.exp(b_end) + upd

    unroll = 2 if n_inner % 2 == 0 else 1

    def body(i, carry):
        for u in range(unroll):
            chunk_step(i * unroll + u)
        return carry

    lax.fori_loop(0, n_inner // unroll, body, 0)

    @pl.when(lt == pl.num_programs(1) - 1)
    def _():
        for hd in range(H):
            s_ref[0, hd] = st_ref[hd].T


def _gla_rec(qk, v, la, s0, *, seq, chunk, tl):
    B, H, dkh, dvh = s0.shape
    tl = min(tl, seq)
    assert seq % tl == 0 and tl % chunk == 0
    nlt = seq // tl
    kern = functools.partial(_gla_rec_kernel, chunk=chunk, n_inner=tl // chunk)
    return pl.pallas_call(
        kern,
        out_shape=(jax.ShapeDtypeStruct(v.shape, BF16), jax.ShapeDtypeStruct(s0.shape, F32)),
        grid=(B, nlt),
        in_specs=[pl.BlockSpec((tl, H * dkh), lambda b, t: (b * nlt + t, 0)),
                  pl.BlockSpec((tl, H * dkh), lambda b, t: (b * nlt + t, 1)),
                  pl.BlockSpec((tl, H * dvh), lambda b, t: (b * nlt + t, 0)),
                  pl.BlockSpec((tl, H * dkh), lambda b, t: (b * nlt + t, 0)),
                  pl.BlockSpec((1, H, dkh, dvh), lambda b, t: (b, 0, 0, 0))],
        out_specs=(pl.BlockSpec((tl, H * dvh), lambda b, t: (b * nlt + t, 0)),
                   pl.BlockSpec((1, H, dkh, dvh), lambda b, t: (b, 0, 0, 0))),
        scratch_shapes=[pltpu.VMEM((H, dvh, dkh), F32)],
        compiler_params=_params("parallel", "arbitrary"),
        name="gla_recurrence",
    )(qk, qk, v, la, s0)


def _mix_mlp_kernel(*refs, gla):
    if gla:
        a_ref, gt_ref, on_ref = refs[:3]
        refs = refs[3:]
    else:
        a_ref = refs[0]
        refs = refs[1:]
    x_ref, wo_ref, g1_ref, g_ref, sh_ref, sc_ref, g2_ref, wu_ref, wd_ref, out_ref, h_ref = refs
    k = pl.program_id(1)

    @pl.when(k == 0)
    def _():
        tm = x_ref.shape[0]
        n_split = 4 if tm % (4 * LANES) == 0 else 1
        for part in range(n_split):
            rs = slice(part * (tm // n_split), (part + 1) * (tm // n_split))
            if gla:
                dvh = a_ref.shape[1] // GLA_HEADS
                mix = jnp.zeros((tm // n_split, out_ref.shape[1]), F32)
                for hd in range(GLA_HEADS):
                    sl = slice(hd * dvh, (hd + 1) * dvh)
                    y = _head_rms(a_ref[rs, sl].astype(F32), on_ref[...])
                    gt = gt_ref[rs, sl].astype(F32)
                    y = y * (gt * jax.nn.sigmoid(gt))
                    mix = mix + jnp.dot(y.astype(BF16), wo_ref[sl, :], preferred_element_type=F32)
            else:
                mix = jnp.dot(a_ref[rs, :].astype(BF16), wo_ref[...], preferred_element_type=F32)
            x1 = x_ref[rs, :] + _mod_rows(g1_ref, rs) * mix
            out_ref[rs, :] = x1
            h_ref[rs, :] = _prenorm(x1, g_ref[...], _mod_rows(sh_ref, rs), _mod_rows(sc_ref, rs)).astype(BF16)

    u = jnp.dot(h_ref[...], wu_ref[...], preferred_element_type=F32)
    u = jnp.square(jnp.maximum(u, 0.0)).astype(BF16)
    out_ref[...] += g2_ref[0] * jnp.dot(u, wd_ref[...], preferred_element_type=F32)


def _mix_mlp(mix_inputs, x, w_out, gate1, g, shift, scale, gate2, w_up, w_down, *, tm, tf):
    rows, D = x.shape
    FF = w_up.shape[1]
    tm = _row_tile(rows, tm)
    gla = len(mix_inputs) == 3
    row = lambda a: pl.BlockSpec((tm, a.shape[1]), lambda i, k: (i, 0))
    full = lambda a: pl.BlockSpec(a.shape, lambda i, k: (0,) * a.ndim)
    mix_specs = [row(mix_inputs[0]), row(mix_inputs[1]), full(mix_inputs[2])] if gla else [row(mix_inputs[0])]
    return pl.pallas_call(
        functools.partial(_mix_mlp_kernel, gla=gla),
        out_shape=jax.ShapeDtypeStruct((rows, D), F32),
        grid=(rows // tm, FF // tf),
        in_specs=mix_specs + [row(x), full(w_out), _mod_spec(gate1, rows, tm, 2), full(g),
                              _mod_spec(shift, rows, tm, 2), _mod_spec(scale, rows, tm, 2),
                              _mod_spec(gate2, rows, tm, 2),
                              pl.BlockSpec((D, tf), lambda i, k: (0, k)),
                              pl.BlockSpec((tf, D), lambda i, k: (k, 0))],
        out_specs=pl.BlockSpec((tm, D), lambda i, k: (i, 0)),
        scratch_shapes=[pltpu.VMEM((tm, D), BF16)],
        compiler_params=pltpu.CompilerParams(dimension_semantics=("parallel", "arbitrary"),
                                             vmem_limit_bytes=MIX_MLP_VMEM_LIMIT),
        name="gla_out_mlp" if gla else "moba_out_mlp",
    )(*mix_inputs, x, w_out, gate1, g, shift, scale, gate2, w_up, w_down)


def _kv_kernel(x_ref, g_ref, sh_ref, sc_ref, w_ref, kn_ref, k_ref, v_ref, *rest, hd, nsum, seq):
    h = _prenorm(x_ref[...], g_ref[...], sh_ref[0], sc_ref[0]).astype(BF16)
    n = k_ref.shape[1]
    tm = k_ref.shape[0]
    kk = jnp.dot(h, w_ref[:, :n], preferred_element_type=F32)
    vv = jnp.dot(h, w_ref[:, n:], preferred_element_type=F32)
    v_ref[...] = vv
    if nsum:
        ka_ref, va_ref, ks_ref = rest
        pos = (pl.program_id(0) * tm) % seq + lax.broadcasted_iota(jnp.int32, (tm, hd), 0)
        lane = lax.broadcasted_iota(jnp.int32, (tm, hd), 1)
        block_onehot = jnp.where(lane == pos // MOBA_BLOCK, 1.0, 0.0).astype(BF16)
        ones = jnp.ones((tm, hd), BF16)
    for a in range(n // hd):
        sl = slice(a * hd, (a + 1) * hd)
        kh = _head_rms(kk[:, sl], kn_ref[...])
        k_ref[:, sl] = kh
        if nsum:
            ka_ref[:, 2 * a * hd:(2 * a + 1) * hd] = kh.astype(BF16)
            ka_ref[:, (2 * a + 1) * hd:(2 * a + 2) * hd] = block_onehot
            va_ref[:, 2 * a * hd:(2 * a + 1) * hd] = vv[:, sl].astype(BF16)
            va_ref[:, (2 * a + 1) * hd:(2 * a + 2) * hd] = ones
            for r in range(nsum):
                ks_ref[r, :, sl] = jnp.sum(kh[r * MOBA_BLOCK:(r + 1) * MOBA_BLOCK], axis=0, keepdims=True)


def _kv(x, g, shift, scale, w_kv, k_norm, *, tm, seq, for_prompt):
    rows, D = x.shape
    n = w_kv.shape[1] // 2
    hd = k_norm.shape[1]
    tm = _row_tile(rows, tm)
    nsum = tm // MOBA_BLOCK if for_prompt else 0
    assert not for_prompt or (tm % MOBA_BLOCK == 0 and seq // MOBA_BLOCK <= hd)
    row = lambda m: pl.BlockSpec((tm, m), lambda i: (i, 0))
    full = lambda a: pl.BlockSpec(a.shape, lambda i: (0,) * a.ndim)
    out_shape = [jax.ShapeDtypeStruct((rows, n), F32), jax.ShapeDtypeStruct((rows, n), F32)]
    out_specs = [row(n), row(n)]
    if for_prompt:
        out_shape += [jax.ShapeDtypeStruct((rows, 2 * n), BF16), jax.ShapeDtypeStruct((rows, 2 * n), BF16),
                      jax.ShapeDtypeStruct((rows // MOBA_BLOCK, 1, n), F32)]
        out_specs += [row(2 * n), row(2 * n), pl.BlockSpec((nsum, 1, n), lambda i: (i, 0, 0))]
    return pl.pallas_call(
        functools.partial(_kv_kernel, hd=hd, nsum=nsum, seq=seq),
        out_shape=tuple(out_shape),
        grid=(rows // tm,),
        in_specs=[row(D), full(g), _mod_spec(shift, rows, tm, 1), _mod_spec(scale, rows, tm, 1),
                  full(w_kv), full(k_norm)],
        out_specs=tuple(out_specs),
        compiler_params=_params("parallel"),
        name="shared_kv",
    )(x, g, shift, scale, w_kv, k_norm)


def _q_kernel(x_ref, g_ref, sh_ref, sc_ref, w_ref, qn_ref, q_ref, *, hd):
    h = _prenorm(x_ref[...], g_ref[...], sh_ref[0], sc_ref[0]).astype(BF16)
    qq = jnp.dot(h, w_ref[...], preferred_element_type=F32)
    for a in range(q_ref.shape[1] // hd):
        sl = slice(a * hd, (a + 1) * hd)
        q_ref[:, sl] = _head_rms(qq[:, sl], qn_ref[...])


def _q_proj(x, g, shift, scale, w_q, q_norm, *, tm):
    rows, D = x.shape
    n = w_q.shape[1]
    tm = _row_tile(rows, tm)
    row = lambda m: pl.BlockSpec((tm, m), lambda i: (i, 0))
    full = lambda a: pl.BlockSpec(a.shape, lambda i: (0,) * a.ndim)
    return pl.pallas_call(
        functools.partial(_q_kernel, hd=q_norm.shape[1]),
        out_shape=jax.ShapeDtypeStruct((rows, n), F32),
        grid=(rows // tm,),
        in_specs=[row(D), full(g), _mod_spec(shift, rows, tm, 1), _mod_spec(scale, rows, tm, 1),
                  full(w_q), full(q_norm)],
        out_specs=row(n),
        compiler_params=_params("parallel"),
        name="moba_q_proj",
    )(x, g, shift, scale, w_q, q_norm)


def _top_blocks(scores, n_past, axis):
    nbk = scores.shape[axis]
    blk_id = lax.broadcasted_iota(jnp.int32, scores.shape, axis)
    sc = jnp.where(blk_id < n_past, scores, -jnp.inf)
    picks = []
    for r in range(MOBA_TOPK):
        m = jnp.max(sc, axis=axis, keepdims=True)
        idx = jnp.min(jnp.where(sc == m, blk_id, nbk), axis=axis, keepdims=True)
        picks.append((idx, (jnp.zeros_like(idx) + r) < n_past))
        sc = jnp.where(blk_id == idx, -jnp.inf, sc)
    return picks


def _q_sel_kernel(x_ref, g_ref, sh_ref, sc_ref, w_ref, qn_ref, ks_ref, qa_ref, *, hd, seq, scale):
    tm = x_ref.shape[0]
    nbk = ks_ref.shape[1]
    h = _prenorm(x_ref[...], g_ref[...], sh_ref[0], sc_ref[0]).astype(BF16)
    qq = jnp.dot(h, w_ref[...], preferred_element_type=F32)
    pos = (pl.program_id(0) * tm) % seq + lax.broadcasted_iota(jnp.int32, (1, tm), 1)
    own = pos // MOBA_BLOCK
    blk_id = lax.broadcasted_iota(jnp.int32, (nbk, tm), 0)
    for a in range(ATT_HEADS):
        sl = slice(a * hd, (a + 1) * hd)
        qh = _head_rms(qq[:, sl], qn_ref[...])
        scores = _dot_nt_3pass(ks_ref[0, :, sl] * (1.0 / MOBA_BLOCK), qh)
        chosen = blk_id < 0
        for idx, ok in _top_blocks(scores, own, 0):
            chosen = chosen | ((blk_id == idx) & ok)
        is_prev = blk_id == own - 1
        far = jnp.where(chosen & jnp.logical_not(is_prev), 0.0, NEG)
        prev = jnp.max(jnp.where(chosen & is_prev, 0.0, NEG), axis=0, keepdims=True)
        mask_t = jnp.concatenate([far, jnp.zeros((hd - nbk - SUBLANES, tm), F32),
                                  jnp.broadcast_to(prev, (SUBLANES, tm))], axis=0)
        qa_ref[:, 2 * a * hd:(2 * a + 1) * hd] = (qh * scale).astype(BF16)
        qa_ref[:, (2 * a + 1) * hd:(2 * a + 2) * hd] = mask_t.T.astype(BF16)


def _q_proj_sel(x, g, shift, scale, w_q, q_norm, ksum, *, tm, seq):
    rows, D = x.shape
    n = w_q.shape[1]
    hd = q_norm.shape[1]
    B, nbk, _ = ksum.shape
    tm = _row_tile(rows, tm)
    tiles_per_seq = seq // tm
    assert seq % tm == 0 and tm % LANES == 0 and nbk % SUBLANES == 0 and nbk + SUBLANES <= hd
    row = lambda m: pl.BlockSpec((tm, m), lambda i: (i, 0))
    full = lambda a: pl.BlockSpec(a.shape, lambda i: (0,) * a.ndim)
    return pl.pallas_call(
        functools.partial(_q_sel_kernel, hd=hd, seq=seq, scale=hd ** -0.5),
        out_shape=jax.ShapeDtypeStruct((rows, 2 * n), BF16),
        grid=(rows // tm,),
        in_specs=[row(D), full(g), _mod_spec(shift, rows, tm, 1), _mod_spec(scale, rows, tm, 1),
                  full(w_q), full(q_norm),
                  pl.BlockSpec((1, nbk, n), lambda i: (i // tiles_per_seq, 0, 0))],
        out_specs=row(2 * n),
        compiler_params=_params("parallel"),
        name="moba_q_proj_select",
    )(x, g, shift, scale, w_q, q_norm, ksum)


def _moba_prompt_kernel(pt_ref, qa_ref, ka_ref, va_ref, bias_ref, *rest,
                        blk, group, n_groups, halves, heads, n_page_refs, ppb):
    page_refs = rest[:n_page_refs]
    o_ref, ps_ref = rest[n_page_refs:n_page_refs + 2]
    m_ref, acc_ref, sa_ref, sb_ref, pa_ref, pb_ref, al_ref, pn_ref, m0_ref = rest[n_page_refs + 2:]
    for r in range(n_page_refs // ppb):
        total = jnp.sum(page_refs[r * ppb][0], axis=0)
        for p in range(1, ppb):
            total = total + jnp.sum(page_refs[r * ppb + p][0], axis=0)
        ps_ref[r] = total

    it = pl.program_id(2)
    hd = qa_ref.shape[1] // (2 * heads)
    last_group = n_groups - 1

    def lanes(hh):
        return slice(hh * 2 * hd, (hh + 1) * 2 * hd)

    def rows_of(g):
        return pl.ds(pl.multiple_of(g * (group * blk), group * blk), group * blk)

    def scores_of(hh, g):
        return lax.dot_general(qa_ref[:, lanes(hh)], ka_ref[rows_of(g), lanes(hh)], NT,
                               preferred_element_type=F32)

    for hh in range(heads):
        sa_ref[hh] = scores_of(hh, 0)

    t_io = lax.broadcasted_iota(jnp.int32, (blk, blk), 0)
    s_io = lax.broadcasted_iota(jnp.int32, (blk, blk), 1)

    def near_rows(hf):
        own = it * halves + hf
        return (pl.ds(pl.multiple_of(jnp.maximum(own - 1, 0) * blk, blk), blk),
                pl.ds(pl.multiple_of(own * blk, blk), blk))

    for hh, hf in [(hh, hf) for hf in range(halves) for hh in range(heads)]:
        rq = slice(hf * blk, (hf + 1) * blk)
        r_prev, r_own = near_rows(hf)
        k_lanes = slice(hh * 2 * hd, hh * 2 * hd + hd)
        qb = qa_ref[rq, k_lanes]
        prev_mask = qa_ref[rq, (hh + 1) * 2 * hd - 1:(hh + 1) * 2 * hd].astype(F32)
        s_own = lax.dot_general(qb, ka_ref[r_own, k_lanes], NT, preferred_element_type=F32)
        s_own = jnp.where(s_io <= t_io, s_own + bias_ref[hh, 0], NEG)
        s_prev = lax.dot_general(qb, ka_ref[r_prev, k_lanes], NT, preferred_element_type=F32)
        s_prev = s_prev + bias_ref[hh, 1] + prev_mask
        m0 = jnp.maximum(jnp.max(s_own, axis=1, keepdims=True), jnp.max(s_prev, axis=1, keepdims=True))
        pn_ref[hh, rq, :blk] = jnp.exp(s_prev - m0).astype(BF16)
        pn_ref[hh, rq, blk:] = jnp.exp(s_own - m0).astype(BF16)
        m_ref[hh, rq, :] = m0
        m0_ref[hh, rq, :] = m0
    acc_ref[...] = jnp.zeros_like(acc_ref)

    def softmax(hh, s_ref, p_ref, slot):
        m_prev = m_ref[hh]
        s = s_ref[hh]
        m_new = jnp.maximum(m_prev, jnp.max(s, axis=1, keepdims=True))
        p_ref[hh] = jnp.exp(s - m_new).astype(BF16)
        al_ref[hh, slot] = jnp.exp(m_prev - m_new)
        m_ref[hh] = m_new

    def flush(hh, p_ref, slot, g):
        acc_ref[hh] = al_ref[hh, slot] * acc_ref[hh] + jnp.dot(p_ref[hh], va_ref[rows_of(g), lanes(hh)],
                                                               preferred_element_type=F32)

    n_far = it * halves + halves - 2
    n_pairs = ((n_far + group - 1) // group + 1) // 2
    pb_ref[...] = jnp.zeros_like(pb_ref)
    for hh in range(heads):
        al_ref[hh, 1] = jnp.ones(al_ref.shape[2:], F32)

    def body(t, carry):
        for hh in range(heads):
            sb_ref[hh] = scores_of(hh, 2 * t + 1)
            flush(hh, pb_ref, 1, jnp.maximum(2 * t - 1, 0))
            softmax(hh, sa_ref, pa_ref, 0)
        for hh in range(heads):
            sa_ref[hh] = scores_of(hh, jnp.minimum(2 * t + 2, last_group))
            flush(hh, pa_ref, 0, 2 * t)
            softmax(hh, sb_ref, pb_ref, 1)
        return carry

    lax.fori_loop(0, n_pairs, body, 0)

    for hh in range(heads):
        flush(hh, pb_ref, 1, jnp.maximum(2 * n_pairs - 1, 0))
    for hh, hf in [(hh, hf) for hf in range(halves) for hh in range(heads)]:
        rq = slice(hf * blk, (hf + 1) * blk)
        r_prev, r_own = near_rows(hf)
        near = jnp.dot(pn_ref[hh, rq, :blk], va_ref[r_prev, lanes(hh)], preferred_element_type=F32)
        near = near + jnp.dot(pn_ref[hh, rq, blk:], va_ref[r_own, lanes(hh)], preferred_element_type=F32)
        acc = acc_ref[hh, rq, :] + jnp.exp(m0_ref[hh, rq, :] - m_ref[hh, rq, :]) * near
        o_ref[rq, hh * hd:(hh + 1) * hd] = (acc[:, :hd] / acc[:, hd:]).astype(o_ref.dtype)


def _moba_prompt(qa, ka, va, bias, cache_k, pt_flat, *, batch, seq, n_pages, first_block, blocks_per_step,
                 group=4, halves=MOBA_Q_TILE_BLOCKS, heads=1):
    rows, n2 = qa.shape
    hd = n2 // (2 * ATT_HEADS)
    blk = MOBA_BLOCK
    nq = seq // blk
    assert nq % (2 * group) == 0 and group >= 2 and nq % halves == 0 and ATT_HEADS % heads == 0
    nt = nq // halves
    tq = halves * blk
    hg = ATT_HEADS // heads
    steps = batch * hg * nt
    _, page, H, _ = cache_k.shape
    ppb = blk // page
    nbs = n_pages // ppb
    last_block = (pt_flat.shape[0] // n_pages) * nbs - 1
    bps = blocks_per_step

    def step_of(b, h, i):
        return (b * hg + h) * nt + i

    def page_spec(r, p):
        def index_map(b, h, i, pt):
            f = jnp.minimum(first_block + step_of(b, h, i) * bps + r, last_block)
            return (pt[(f // nbs) * n_pages + (f % nbs) * ppb + p], 0, 0, 0)
        return pl.BlockSpec((1, page, H, hd), index_map)

    page_specs = [page_spec(r, p) for r in range(bps) for p in range(ppb)]
    kern = functools.partial(_moba_prompt_kernel, blk=blk, group=group, n_groups=nq // group,
                             halves=halves, heads=heads, n_page_refs=len(page_specs), ppb=ppb)
    return pl.pallas_call(
        kern,
        out_shape=(jax.ShapeDtypeStruct((rows, ATT_HEADS * hd), BF16),
                   jax.ShapeDtypeStruct((steps * bps, H, hd), F32)),
        grid_spec=pltpu.PrefetchScalarGridSpec(
            num_scalar_prefetch=1,
            grid=(batch, hg, nt),
            in_specs=[pl.BlockSpec((tq, heads * 2 * hd), lambda b, h, i, pt: (b * nt + i, h)),
                      pl.BlockSpec((seq, heads * 2 * hd), lambda b, h, i, pt: (b, h)),
                      pl.BlockSpec((seq, heads * 2 * hd), lambda b, h, i, pt: (b, h)),
                      pl.BlockSpec((heads, 2, blk, blk), lambda b, h, i, pt: (h, 0, 0, 0))] + page_specs,
            out_specs=(pl.BlockSpec((tq, heads * hd), lambda b, h, i, pt: (b * nt + i, h)),
                       pl.BlockSpec((bps, H, hd), lambda b, h, i, pt: (step_of(b, h, i), 0, 0))),
            scratch_shapes=[pltpu.VMEM((heads, tq, 1), F32), pltpu.VMEM((heads, tq, 2 * hd), F32),
                            pltpu.VMEM((heads, tq, group * blk), F32), pltpu.VMEM((heads, tq, group * blk), F32),
                            pltpu.VMEM((heads, tq, group * blk), BF16), pltpu.VMEM((heads, tq, group * blk), BF16),
                            pltpu.VMEM((heads, 2, tq, 1), F32), pltpu.VMEM((heads, tq, 2 * blk), BF16),
                            pltpu.VMEM((heads, tq, 1), F32)]),
        compiler_params=_params("parallel", "parallel", "arbitrary"),
        name="moba_prompt_attention",
    )(pt_flat, qa, ka, va, bias, *([cache_k] * len(page_specs)))


def _sample_select_kernel(q_ref, ks_ref, idx_ref, *, hd, blk):
    nbk = ks_ref.shape[1]
    lane = lax.broadcasted_iota(jnp.int32, idx_ref.shape[1:], 1)
    out = jnp.zeros(idx_ref.shape[1:], jnp.int32)
    for a in range(ATT_HEADS):
        means = ks_ref[0, :, a, :] * (1.0 / blk)
        scores = _dot_nt_3pass(q_ref[0, :, a * hd:(a + 1) * hd], means)
        for r, (idx, _) in enumerate(_top_blocks(scores, nbk, 1)):
            out = jnp.where(lane == a * MOBA_TOPK + r, idx, out)
    idx_ref[0] = out


def _sample_select(q8, ksum):
    DB, R, n = q8.shape
    _, nbk, H, hd = ksum.shape
    assert nbk >= MOBA_TOPK and H == ATT_HEADS
    kern = functools.partial(_sample_select_kernel, hd=hd, blk=MOBA_BLOCK)
    return pl.pallas_call(
        kern,
        out_shape=jax.ShapeDtypeStruct((DB, R, LANES), jnp.int32),
        grid=(DB,),
        in_specs=[pl.BlockSpec((1, R, n), lambda d: (d, 0, 0)),
                  pl.BlockSpec((1, nbk, H, hd), lambda d: (d, 0, 0, 0))],
        out_specs=pl.BlockSpec((1, R, LANES), lambda d: (d, 0, 0)),
        compiler_params=_params("parallel"),
        name="sample_select",
    )(q8, ksum)


def _sample_attn_kernel(pt_ref, sel_ref, q_ref, kn_ref, vn_ref, bias_ref, ck_ref, cv_ref, o_ref,
                        kbuf, vbuf, sem, *, lq_n, ppb, n_pages, page, scale):
    g = pl.program_id(0)
    nslot = lq_n * MOBA_TOPK
    nsl = nslot * ppb

    def copies(step, slot):
        d = step // ATT_HEADS
        a = step % ATT_HEADS
        out = []
        for j in range(nsl):
            phys = pt_ref[d * n_pages + sel_ref[step * nslot + j // ppb] * ppb + j % ppb]
            rows = pl.ds(j * page, page)
            out.append(pltpu.make_async_copy(ck_ref.at[phys, :, a, :], kbuf.at[slot, rows, :], sem.at[slot, 0, j]))
            out.append(pltpu.make_async_copy(cv_ref.at[phys, :, a, :], vbuf.at[slot, rows, :], sem.at[slot, 1, j]))
        return out

    @pl.when(g == 0)
    def _():
        for c in copies(0, 0):
            c.start()

    @pl.when(g + 1 < pl.num_programs(0))
    def _():
        for c in copies(g + 1, (g + 1) % 2):
            c.start()

    slot = g % 2
    q = q_ref[0] * scale
    row = lax.broadcasted_iota(jnp.int32, (q.shape[0], 1), 0)

    new_logits = []
    for s in range(lq_n):
        sc = jnp.sum(q * kn_ref[0, s:s + 1, :], axis=1, keepdims=True) + bias_ref[0, 8:16, s:s + 1]
        new_logits.append(jnp.where(s <= row, sc, NEG))

    for c in copies(g, slot):
        c.wait()

    s_all = lax.dot_general(q.astype(BF16), kbuf[slot].astype(BF16), NT, preferred_element_type=F32)
    parts = []
    for j in range(nsl):
        lp = sel_ref[g * nslot + j // ppb] * ppb + j % ppb
        is_last = (jnp.zeros((q.shape[0], page), jnp.int32) + lp) == n_pages - 1
        sj = s_all[:, j * page:(j + 1) * page] + jnp.where(is_last, bias_ref[0, 0:8, :], 0.0)
        parts.append(jnp.where(row == j // (MOBA_TOPK * ppb), sj, NEG))
    m = jnp.max(functools.reduce(jnp.maximum, parts), axis=1, keepdims=True)
    m = functools.reduce(jnp.maximum, new_logits, m)
    probs = [jnp.exp(p - m) for p in parts]
    den = jnp.sum(functools.reduce(jnp.add, probs), axis=1, keepdims=True)
    pcat = jnp.concatenate([p.astype(BF16) for p in probs], axis=1)
    acc = jnp.dot(pcat, vbuf[slot].astype(BF16), preferred_element_type=F32)
    for s in range(lq_n):
        e = jnp.exp(new_logits[s] - m)
        den = den + e
        acc = acc + e * vn_ref[0, s:s + 1, :]
    o_ref[0] = acc / den


def _sample_attn(q8, kn8, vn8, bias, cache_k, cache_v, pt_flat, sel_flat, *, lq_n, n_pages):
    DB, R, n = q8.shape
    _, page, H, hd = cache_k.shape
    ppb = MOBA_BLOCK // page
    nsl = lq_n * MOBA_TOPK * ppb
    assert page == LANES and R == SUBLANES and H == ATT_HEADS
    kern = functools.partial(_sample_attn_kernel, lq_n=lq_n, ppb=ppb, n_pages=n_pages, page=page,
                             scale=hd ** -0.5)
    head = pl.BlockSpec((1, R, hd), lambda g, pt, sel: (g // ATT_HEADS, 0, g % ATT_HEADS))
    return pl.pallas_call(
        kern,
        out_shape=jax.ShapeDtypeStruct((DB, R, n), F32),
        grid_spec=pltpu.PrefetchScalarGridSpec(
            num_scalar_prefetch=2,
            grid=(DB * ATT_HEADS,),
            in_specs=[head, head, head,
                      pl.BlockSpec((1, 2 * SUBLANES, LANES), lambda g, pt, sel: (g % ATT_HEADS, 0, 0)),
                      pl.BlockSpec(memory_space=pl.ANY), pl.BlockSpec(memory_space=pl.ANY)],
            out_specs=head,
            scratch_shapes=[pltpu.VMEM((2, nsl * page, hd), F32), pltpu.VMEM((2, nsl * page, hd), F32),
                            pltpu.SemaphoreType.DMA((2, 2, nsl))]),
        compiler_params=_params("arbitrary"),
        name="sample_attention",
    )(pt_flat, sel_flat, q8, kn8, vn8, bias, cache_k, cache_v)


def _bucket_of_distance():
    n = np.arange(MAX_DISTANCE)
    max_exact = N_BUCKETS // 2
    nf = np.maximum(n, max_exact).astype(np.float32)
    large = max_exact + (np.log(nf / np.float32(max_exact)) / np.float32(math.log(MAX_DISTANCE / max_exact))
                         * np.float32(N_BUCKETS - max_exact)).astype(np.int32)
    return np.where(n < max_exact, n, np.minimum(large, N_BUCKETS - 1)).astype(np.int32)


def _dist_table(rel_bias, width):
    t = rel_bias.astype(F32)[_bucket_of_distance()].T
    t = t - t[:, MAX_DISTANCE - 1:]
    return jnp.pad(t, ((0, 0), (0, width - MAX_DISTANCE)))


def _toeplitz(first_row_wrapped, n):
    H = first_row_wrapped.shape[0]
    x = jnp.broadcast_to(first_row_wrapped[:, None, :], (H, n, 2 * n)).reshape(H, 2 * n * n)
    return x[:, :n * (2 * n - 1)].reshape(H, n, 2 * n - 1)[:, :, :n]


def _prompt_bias_tiles(rel_bias):
    blk = MOBA_BLOCK
    assert blk >= MAX_DISTANCE
    t = _dist_table(rel_bias, blk)
    rev = jnp.flip(t[:, 1:], axis=1)
    zeros = jnp.zeros_like(t)
    own = _toeplitz(jnp.concatenate([t[:, :1], zeros, rev], axis=1), blk)
    prev = _toeplitz(jnp.concatenate([zeros[:, :1], rev, zeros], axis=1), blk)
    return jnp.stack([own, prev], axis=1)


def _sample_bias_rows(rel_bias, *, page, lq_n):
    assert page == LANES and page >= MAX_DISTANCE and lq_n <= SUBLANES
    t = _dist_table(rel_bias, 2 * page)
    r = np.arange(SUBLANES)[:, None]
    c = np.arange(LANES)[None, :]
    return jnp.concatenate([t[:, page + r - c], t[:, np.clip(r - c, 0, None)]], axis=1)


def _hi_lo(w):
    hi = w.astype(BF16)
    return jnp.stack([hi, (w - hi.astype(F32)).astype(BF16)], axis=1)


def _pad_tokens(a, groups, per, to):
    return jnp.pad(a.reshape(groups, per, a.shape[-1]), ((0, 0), (0, to - per), (0, 0)))


def _trunk(x, mods, kvmod, s0, W, *, batch, seq, tm, attend, make_ctx):
    n_a = W['gla_w_main'].shape[0]
    depth = W['mlp_w_up'].shape[0]
    dk = W['gla_b_gate'].shape[2]
    dv = W['gla_w_out'].shape[1]
    states = []
    k = v = ctx = None
    for i in range(depth):
        sh1, sc1, g1, sh2, sc2, g2 = mods[i]
        if i == n_a:
            k, v, ctx = make_ctx(x, kvmod)
        if i < n_a:
            qk, vv, gt, la = _gla_in(x, W['norm_mix'][i], sh1, sc1, W['gla_w_main'][i], W['gla_w_r'][i],
                                     W['gla_w_g2'][i], W['gla_b_gate'][i], dk=dk, dv=dv, tm=tm)
            chunk = math.gcd(GLA_CHUNK, seq)
            if chunk < 16:
                padded = 16
                qk, vv, la = (_pad_tokens(t, batch, seq, padded).reshape(batch * padded, -1)
                              for t in (qk, vv, la))
                o, s = _gla_rec(qk, vv, la, s0[i], seq=padded, chunk=padded, tl=padded)
                o = o.reshape(batch, padded, dv)[:, :seq].reshape(batch * seq, dv)
            else:
                o, s = _gla_rec(qk, vv, la, s0[i], seq=seq, chunk=chunk, tl=1024)
            states.append(s)
            mix_inputs, w_mix = (o, gt, W['gla_onorm'][i]), W['gla_w_out'][i]
        else:
            j = i - n_a
            q_args = (W['norm_mix'][i], sh1, sc1, W['moba_w_q'][j], W['q_norm'][j])
            mix_inputs, w_mix = (attend(x, q_args, ctx, j),), W['moba_w_o'][j]
        x = _mix_mlp(mix_inputs, x, w_mix, g1, W['norm_mlp'][i], sh2, sc2, g2,
                     W['mlp_w_up'][i], W['mlp_w_down'][i], tm=min(tm * 2, 1024), tf=1024)
    return x, jnp.stack(states), k, v


def kernel(x_prompt, x_sample, state_gla, cache_k, cache_v, page_table, c_prompt, c_sample,
           ada_w, ada_b, norm_mix, norm_mlp, gla_w_in, gla_w_gate2, gla_b_gate, gla_onorm, gla_w_out,
           kv_ada_w, kv_ada_b, kv_norm, w_kv, k_norm, moba_w_q, q_norm, moba_w_o, rel_bias,
           mlp_w_up, mlp_w_down):
    B, L, D = x_prompt.shape
    DB, LQ, _ = x_sample.shape
    depth = ada_w.shape[0]
    n_a = gla_w_in.shape[0]
    rank, dk = gla_w_gate2.shape[1:]
    dv = gla_w_out.shape[1]
    n_pool, page, H, hd = cache_k.shape
    n_pages = page_table.shape[1]
    past = n_pages * page
    assert H == ATT_HEADS and L % MOBA_BLOCK == 0 and MOBA_BLOCK % page == 0
    assert past % MOBA_BLOCK == 0 and LQ <= min(SUBLANES, MOBA_BLOCK) and rank <= LANES

    W = {
        'norm_mix': norm_mix.reshape(depth, 1, D), 'norm_mlp': norm_mlp.reshape(depth, 1, D),
        'gla_w_main': gla_w_in[:, :, :2 * dk + 2 * dv].astype(BF16),
        'gla_w_r': jnp.pad(gla_w_in[:, :, 2 * dk + 2 * dv:], ((0, 0), (0, 0), (0, LANES - rank))).astype(BF16),
        'gla_w_g2': _hi_lo(jnp.pad(gla_w_gate2, ((0, 0), (0, LANES - rank), (0, 0)))),
        'gla_b_gate': gla_b_gate.reshape(n_a, 1, dk), 'gla_onorm': gla_onorm.reshape(n_a, 1, -1),
        'gla_w_out': gla_w_out.astype(BF16),
        'moba_w_q': moba_w_q.astype(BF16), 'q_norm': q_norm.reshape(-1, 1, hd),
        'moba_w_o': moba_w_o.astype(BF16),
        'mlp_w_up': mlp_w_up.astype(BF16), 'mlp_w_down': mlp_w_down.astype(BF16),
    }
    w_kv_b = w_kv.astype(BF16)
    kv_norm2 = kv_norm.reshape(1, D)
    k_norm2 = k_norm.reshape(1, hd)

    n_c = B + DB
    c_all = jnp.pad(jnp.concatenate([c_prompt, c_sample], axis=0), ((0, -n_c % SUBLANES), (0, 0)))
    mod = _ada(c_all, ada_w, ada_b)
    kvm = _ada(c_all, kv_ada_w[None], kv_ada_b[None])[0]

    def prompt_mod(m):
        return m[:B].reshape(B, 1, D)

    def sample_mod(m):
        return jnp.repeat(m[B:n_c], LQ, axis=0).reshape(1, DB * LQ, D)

    def split_mods(pick):
        layers = [[pick(mod[i, :, t * D:(t + 1) * D]) for t in range(6)] for i in range(depth)]
        return layers, [pick(kvm[:, t * D:(t + 1) * D]) for t in range(2)]

    p_mods, p_kvmod = split_mods(prompt_mod)
    p_bias = _prompt_bias_tiles(rel_bias)

    def prompt_ctx(x, kvmod):
        k, v, ka, va, ksum = _kv(x, kv_norm2, kvmod[0], kvmod[1], w_kv_b, k_norm2, tm=512, seq=L,
                                 for_prompt=True)
        return k, v, (ka, va, ksum.reshape(B, L // MOBA_BLOCK, H * hd))

    pt_flat = page_table.reshape(-1).astype(jnp.int32)
    n_sample_blocks = DB * (n_pages // (MOBA_BLOCK // page))
    moba_steps = B * ATT_HEADS * (L // (MOBA_Q_TILE_BLOCKS * MOBA_BLOCK))
    blocks_per_step = -(-n_sample_blocks // ((depth - n_a) * moba_steps))
    sample_block_sums = []

    def prompt_attend(x, q_args, ctx, j):
        ka, va, ksum = ctx
        qa = _q_proj_sel(x, *q_args, ksum, tm=512, seq=L)
        o, sums = _moba_prompt(qa, ka, va, p_bias, cache_k, pt_flat, batch=B, seq=L, n_pages=n_pages,
                               first_block=j * moba_steps * blocks_per_step, blocks_per_step=blocks_per_step)
        sample_block_sums.append(sums)
        return o

    s0_p = jnp.zeros((n_a, B) + state_gla.shape[2:], F32)
    y_p, st_p, k_p, v_p = _trunk(x_prompt.reshape(B * L, D), p_mods, p_kvmod, s0_p, W, batch=B, seq=L,
                                 tm=512, attend=prompt_attend, make_ctx=prompt_ctx)

    s_mods, s_kvmod = split_mods(sample_mod)
    s_bias = _sample_bias_rows(rel_bias, page=page, lq_n=LQ)
    rows_s = DB * LQ

    def sample_ctx(x, kvmod):
        k, v = _kv(x, kv_norm2, kvmod[0], kvmod[1], w_kv_b, k_norm2, tm=rows_s, seq=LQ, for_prompt=False)
        ksum = jnp.concatenate(sample_block_sums, axis=0)[:n_sample_blocks].reshape(DB, -1, H, hd)
        return k, v, (_pad_tokens(k, DB, LQ, SUBLANES), _pad_tokens(v, DB, LQ, SUBLANES), ksum)

    def sample_attend(x, q_args, ctx, j):
        kn8, vn8, ksum = ctx
        q8 = _pad_tokens(_q_proj(x, *q_args, tm=rows_s), DB, LQ, SUBLANES)
        picks = _sample_select(q8, ksum)[:, :LQ, :ATT_HEADS * MOBA_TOPK]
        sel_flat = picks.reshape(DB, LQ, ATT_HEADS, MOBA_TOPK).transpose(0, 2, 1, 3).reshape(-1)
        o8 = _sample_attn(q8, kn8, vn8, s_bias, cache_k, cache_v, pt_flat, sel_flat,
                          lq_n=LQ, n_pages=n_pages)
        return o8[:, :LQ].reshape(rows_s, H * hd)

    y_s, st_s, k_s, v_s = _trunk(x_sample.reshape(rows_s, D), s_mods, s_kvmod, state_gla.astype(F32), W,
                                 batch=DB, seq=LQ, tm=rows_s, attend=sample_attend, make_ctx=sample_ctx)

    return (y_p.reshape(B, L, D), y_s.reshape(DB, LQ, D), st_p, st_s,
            k_p.reshape(B, L, H, hd), v_p.reshape(B, L, H, hd),
            k_s.reshape(DB, LQ, H, hd), v_s.reshape(DB, LQ, H, hd))
```

```python
import functools
import math

import numpy as np
import jax
import jax.numpy as jnp
from jax import lax
from jax.experimental import pallas as pl
from jax.experimental.pallas import tpu as pltpu

F32 = jnp.float32
BF16 = jnp.bfloat16

GLA_HEADS = 4
GLA_GATE_NORM = 16.0
GLA_CHUNK = 64
ATT_HEADS = 8
MOBA_BLOCK = 256
MOBA_TOPK = 3
MOBA_Q_TILE_BLOCKS = 2
N_BUCKETS = 32
MAX_DISTANCE = 128
EPS = 1e-6

LANES = 128
SUBLANES = 8
VMEM_LIMIT = 48 * 1024 * 1024
MIX_MLP_VMEM_LIMIT = 56 * 1024 * 1024
NEG = -1e30
NT = (((1,), (1,)), ((), ()))
TN = (((0,), (0,)), ((), ()))


def _params(*sem):
    return pltpu.CompilerParams(dimension_semantics=sem, vmem_limit_bytes=VMEM_LIMIT)


def _prenorm(x, g, shift, scale):
    ms = jnp.mean(x * x, axis=-1, keepdims=True)
    return (x * lax.rsqrt(ms + EPS)) * g * (1.0 + scale) + shift


def _head_rms(y, g):
    return y * lax.rsqrt(jnp.mean(y * y, axis=-1, keepdims=True) + EPS) * g


def _dot_nt_3pass(a, b):
    a_hi = a.astype(BF16)
    b_hi = b.astype(BF16)
    a_lo = (a - a_hi.astype(F32)).astype(BF16)
    b_lo = (b - b_hi.astype(F32)).astype(BF16)
    dot = functools.partial(lax.dot_general, dimension_numbers=NT, preferred_element_type=F32)
    return dot(a_hi, b_hi) + dot(a_lo, b_hi) + dot(a_hi, b_lo)


def _mod_rows(mod_ref, rows):
    return mod_ref[0] if mod_ref.shape[1] == 1 else mod_ref[0, rows, :]


def _row_tile(rows, want):
    tm = min(rows, want)
    assert rows % tm == 0
    return tm


def _mod_spec(mod, rows, tm, ngrid):
    G, R, D = mod.shape
    tiles_per_group = rows // (G * tm)
    assert tiles_per_group * G * tm == rows and R in (1, tm)
    if ngrid == 1:
        return pl.BlockSpec((1, R, D), lambda i: (i // tiles_per_group, 0, 0))
    return pl.BlockSpec((1, R, D), lambda i, k: (i // tiles_per_group, 0, 0))


def _ada_kernel(c_ref, w_ref, b_ref, o_ref):
    c = c_ref[...]
    a = (c * jax.nn.sigmoid(c)).astype(BF16)
    o_ref[...] = jnp.dot(a, w_ref[...].astype(BF16), preferred_element_type=F32) + b_ref[...]


def _ada(c, w, b, tn=1024):
    NL, D, N = w.shape
    M = c.shape[0]
    return pl.pallas_call(
        _ada_kernel,
        out_shape=jax.ShapeDtypeStruct((NL, M, N), F32),
        grid=(NL, N // tn),
        in_specs=[pl.BlockSpec((M, D), lambda l, j: (0, 0)),
                  pl.BlockSpec((None, D, tn), lambda l, j: (l, 0, j)),
                  pl.BlockSpec((None, 1, tn), lambda l, j: (l, 0, j))],
        out_specs=pl.BlockSpec((None, M, tn), lambda l, j: (l, 0, j)),
        compiler_params=_params("parallel", "parallel"),
        name="ada_modulation",
    )(c, w, b.reshape(NL, 1, N))


def _gla_in_kernel(x_ref, g_ref, sh_ref, sc_ref, w_ref, wr_ref, wg2_ref, bg_ref,
                   qk_ref, v_ref, gt_ref, la_ref, *, dk, dv, qscale):
    dot = functools.partial(jnp.dot, preferred_element_type=F32)
    tm = x_ref.shape[0]
    n_split = 2 if tm % (2 * LANES) == 0 else 1
    for part in range(n_split):
        rs = slice(part * (tm // n_split), (part + 1) * (tm // n_split))
        h = _prenorm(x_ref[rs, :], g_ref[...], _mod_rows(sh_ref, rs), _mod_rows(sc_ref, rs)).astype(BF16)
        r = dot(h, wr_ref[...])
        r_hi = r.astype(BF16)
        r_lo = (r - r_hi.astype(F32)).astype(BF16)
        z = dot(r_hi, wg2_ref[0]) + dot(r_lo, wg2_ref[0]) + dot(r_hi, wg2_ref[1]) + bg_ref[...]
        log_sig = jnp.minimum(z, 0.0) - jnp.log1p(jnp.exp(-jnp.abs(z)))
        la_ref[rs, :] = log_sig * (1.0 / GLA_GATE_NORM)
        qk_ref[rs, :dk] = (dot(h, w_ref[:, :dk]) * qscale).astype(qk_ref.dtype)
        qk_ref[rs, dk:] = dot(h, w_ref[:, dk:2 * dk]).astype(qk_ref.dtype)
        v_ref[rs, :] = dot(h, w_ref[:, 2 * dk:2 * dk + dv]).astype(v_ref.dtype)
        gt_ref[rs, :] = dot(h, w_ref[:, 2 * dk + dv:]).astype(gt_ref.dtype)


def _gla_in(x, g, shift, scale, w_main, w_r, w_g2, b_g, *, dk, dv, tm):
    rows, D = x.shape
    tm = _row_tile(rows, tm)
    kern = functools.partial(_gla_in_kernel, dk=dk, dv=dv, qscale=(dk // GLA_HEADS) ** -0.5)
    full = lambda a: pl.BlockSpec(a.shape, lambda i: (0,) * a.ndim)
    row = lambda n: pl.BlockSpec((tm, n), lambda i: (i, 0))
    return pl.pallas_call(
        kern,
        out_shape=(jax.ShapeDtypeStruct((rows, 2 * dk), BF16), jax.ShapeDtypeStruct((rows, dv), BF16),
                   jax.ShapeDtypeStruct((rows, dv), BF16), jax.ShapeDtypeStruct((rows, dk), F32)),
        grid=(rows // tm,),
        in_specs=[row(D), full(g), _mod_spec(shift, rows, tm, 1), _mod_spec(scale, rows, tm, 1),
                  full(w_main), full(w_r), full(w_g2), full(b_g)],
        out_specs=(row(2 * dk), row(dv), row(dv), row(dk)),
        compiler_params=_params("parallel"),
        name="gla_in_proj",
    )(x, g, shift, scale, w_main, w_r, w_g2, b_g)


def _gla_rec_kernel(q_ref, k_ref, v_ref, la_ref, s0_ref, *rest, chunk, n_inner, n_cast):
    cast_in = rest[:n_cast]
    o_ref, s_ref = rest[n_cast:n_cast + 2]
    cast_out = rest[n_cast + 2:2 * n_cast + 2]
    st_ref = rest[2 * n_cast + 2]
    for src, dst in zip(cast_in, cast_out):
        dst[...] = src[...].astype(dst.dtype)

    lt = pl.program_id(1)
    H, dvh, dkh = st_ref.shape

    @pl.when(lt == 0)
    def _():
        for hd in range(H):
            st_ref[hd] = s0_ref[0, hd].T

    r_io = lax.broadcasted_iota(jnp.int32, (chunk, chunk), 0)
    c_io = lax.broadcasted_iota(jnp.int32, (chunk, chunk), 1)
    lower = r_io >= c_io
    tri = jnp.where(lower, 1.0, 0.0).astype(BF16)
    mid = chunk // 2 - 1

    def chunk_step(c):
        rows = pl.ds(pl.multiple_of(c * chunk, chunk), chunk)
        la = la_ref[rows, :]
        la_hi = la.astype(BF16)
        rest = la - la_hi.astype(F32)
        la_mid = rest.astype(BF16)
        la_lo = (rest - la_mid.astype(F32)).astype(BF16)
        b_all = sum(jnp.dot(tri, part, preferred_element_type=F32) for part in (la_hi, la_mid, la_lo))
        for hd in range(H):
            ks = slice(hd * dkh, (hd + 1) * dkh)
            vs = slice(hd * dvh, (hd + 1) * dvh)
            q = q_ref[rows, ks].astype(F32)
            k = k_ref[rows, ks].astype(F32)
            vb = v_ref[rows, vs]
            b = b_all[:, ks]
            b_mid = b[mid:mid + 1, :]
            b_end = b[chunk - 1:chunk, :]
            qm = (q * jnp.exp(b - b_mid)).astype(BF16)
            km = (k * jnp.exp(b_mid - b)).astype(BF16)
            a = lax.dot_general(qm, km, NT, preferred_element_type=F32)
            a = jnp.where(lower, a, 0.0).astype(BF16)
            st = st_ref[hd]
            qd = (q * jnp.exp(b)).astype(BF16)
            o = jnp.dot(a, vb, preferred_element_type=F32)
            o = o + lax.dot_general(qd, st.astype(BF16), NT, preferred_element_type=F32)
            o_ref[rows, vs] = o.astype(o_ref.dtype)
            ke = (k * jnp.exp(b_end - b)).astype(BF16)
            upd = lax.dot_general(vb, ke, TN, preferred_element_type=F32)
            st_ref[hd] = st * jnp.exp(b_end) + upd

    unroll = 2 if n_inner % 2 == 0 else 1

    def body(i, carry):
        for u in range(unroll):
            chunk_step(i * unroll + u)
        return carry

    lax.fori_loop(0, n_inner // unroll, body, 0)

    @pl.when(lt == pl.num_programs(1) - 1)
    def _():
        for hd in range(H):
            s_ref[0, hd] = st_ref[hd].T


def _cast_blocks(shape):
    n, R, C = shape
    side = min(R, C)
    return n * (R // side) * (C // side)


def _cast_block_map(shape, nlt):
    _, R, C = shape
    side = min(R, C)
    rb, cb = R // side, C // side
    last = _cast_blocks(shape) - 1

    def index_map(b, t):
        f = jnp.minimum(b * nlt + t, last)
        return (f // (rb * cb), (f // cb) % rb, f % cb)
    return index_map


def _gla_rec(qk, v, la, s0, *, seq, chunk, tl, cast_weights=()):
    B, H, dkh, dvh = s0.shape
    tl = min(tl, seq)
    assert seq % tl == 0 and tl % chunk == 0
    nlt = seq // tl
    cast_specs = []
    for w in cast_weights:
        assert _cast_blocks(w.shape) <= B * nlt
        side = min(w.shape[1:])
        cast_specs.append(pl.BlockSpec((1, side, side), _cast_block_map(w.shape, nlt)))
    kern = functools.partial(_gla_rec_kernel, chunk=chunk, n_inner=tl // chunk, n_cast=len(cast_weights))
    out_shape = [jax.ShapeDtypeStruct(v.shape, BF16), jax.ShapeDtypeStruct(s0.shape, F32)]
    out_shape += [jax.ShapeDtypeStruct(w.shape, BF16) for w in cast_weights]
    cast_in_specs = cast_out_specs = cast_specs
    cast_in = list(cast_weights)
    return pl.pallas_call(
        kern,
        out_shape=tuple(out_shape),
        grid=(B, nlt),
        in_specs=[pl.BlockSpec((tl, H * dkh), lambda b, t: (b * nlt + t, 0)),
                  pl.BlockSpec((tl, H * dkh), lambda b, t: (b * nlt + t, 1)),
                  pl.BlockSpec((tl, H * dvh), lambda b, t: (b * nlt + t, 0)),
                  pl.BlockSpec((tl, H * dkh), lambda b, t: (b * nlt + t, 0)),
                  pl.BlockSpec((1, H, dkh, dvh), lambda b, t: (b, 0, 0, 0))] + cast_in_specs,
        out_specs=tuple([pl.BlockSpec((tl, H * dvh), lambda b, t: (b * nlt + t, 0)),
                         pl.BlockSpec((1, H, dkh, dvh), lambda b, t: (b, 0, 0, 0))] + cast_out_specs),
        scratch_shapes=[pltpu.VMEM((H, dvh, dkh), F32)],
        compiler_params=_params("parallel", "arbitrary"),
        name="gla_recurrence",
    )(qk, qk, v, la, s0, *cast_in)


def _mix_mlp_kernel(*refs, gla):
    if gla:
        a_ref, gt_ref, on_ref = refs[:3]
        refs = refs[3:]
    else:
        a_ref = refs[0]
        refs = refs[1:]
    x_ref, wo_ref, g1_ref, g_ref, sh_ref, sc_ref, g2_ref, wu_ref, wd_ref, out_ref, h_ref = refs
    k = pl.program_id(1)

    @pl.when(k == 0)
    def _():
        tm = x_ref.shape[0]
        n_split = 4 if tm % (4 * LANES) == 0 else 1
        for part in range(n_split):
            rs = slice(part * (tm // n_split), (part + 1) * (tm // n_split))
            if gla:
                dvh = a_ref.shape[1] // GLA_HEADS
                mix = jnp.zeros((tm // n_split, out_ref.shape[1]), F32)
                for hd in range(GLA_HEADS):
                    sl = slice(hd * dvh, (hd + 1) * dvh)
                    y = _head_rms(a_ref[rs, sl].astype(F32), on_ref[...])
                    gt = gt_ref[rs, sl].astype(F32)
                    y = y * (gt * jax.nn.sigmoid(gt))
                    mix = mix + jnp.dot(y.astype(BF16), wo_ref[sl, :], preferred_element_type=F32)
            else:
                mix = jnp.dot(a_ref[rs, :].astype(BF16), wo_ref[...], preferred_element_type=F32)
            x1 = x_ref[rs, :] + _mod_rows(g1_ref, rs) * mix
            out_ref[rs, :] = x1
            h_ref[rs, :] = _prenorm(x1, g_ref[...], _mod_rows(sh_ref, rs), _mod_rows(sc_ref, rs)).astype(BF16)

    u = jnp.dot(h_ref[...], wu_ref[...], preferred_element_type=F32)
    u = jnp.square(jnp.maximum(u, 0.0)).astype(BF16)
    out_ref[...] += g2_ref[0] * jnp.dot(u, wd_ref[...], preferred_element_type=F32)


def _mix_mlp(mix_inputs, x, w_out, gate1, g, shift, scale, gate2, w_up, w_down, *, tm, tf):
    rows, D = x.shape
    FF = w_up.shape[1]
    tm = _row_tile(rows, tm)
    gla = len(mix_inputs) == 3
    row = lambda a: pl.BlockSpec((tm, a.shape[1]), lambda i, k: (i, 0))
    full = lambda a: pl.BlockSpec(a.shape, lambda i, k: (0,) * a.ndim)
    mix_specs = [row(mix_inputs[0]), row(mix_inputs[1]), full(mix_inputs[2])] if gla else [row(mix_inputs[0])]
    return pl.pallas_call(
        functools.partial(_mix_mlp_kernel, gla=gla),
        out_shape=jax.ShapeDtypeStruct((rows, D), F32),
        grid=(rows // tm, FF // tf),
        in_specs=mix_specs + [row(x), full(w_out), _mod_spec(gate1, rows, tm, 2), full(g),
                              _mod_spec(shift, rows, tm, 2), _mod_spec(scale, rows, tm, 2),
                              _mod_spec(gate2, rows, tm, 2),
                              pl.BlockSpec((D, tf), lambda i, k: (0, k)),
                              pl.BlockSpec((tf, D), lambda i, k: (k, 0))],
        out_specs=pl.BlockSpec((tm, D), lambda i, k: (i, 0)),
        scratch_shapes=[pltpu.VMEM((tm, D), BF16)],
        compiler_params=pltpu.CompilerParams(dimension_semantics=("parallel", "arbitrary"),
                                             vmem_limit_bytes=MIX_MLP_VMEM_LIMIT),
        name="gla_out_mlp" if gla else "moba_out_mlp",
    )(*mix_inputs, x, w_out, gate1, g, shift, scale, gate2, w_up, w_down)


def _kv_kernel(x_ref, g_ref, sh_ref, sc_ref, w_ref, kn_ref, k_ref, v_ref, *rest, hd, nsum, seq):
    h = _prenorm(x_ref[...], g_ref[...], sh_ref[0], sc_ref[0]).astype(BF16)
    n = k_ref.shape[1]
    tm = k_ref.shape[0]
    kk = jnp.dot(h, w_ref[:, :n], preferred_element_type=F32)
    vv = jnp.dot(h, w_ref[:, n:], preferred_element_type=F32)
    v_ref[...] = vv
    if nsum:
        ka_ref, va_ref, ks_ref = rest
        pos = (pl.program_id(0) * tm) % seq + lax.broadcasted_iota(jnp.int32, (tm, hd), 0)
        lane = lax.broadcasted_iota(jnp.int32, (tm, hd), 1)
        block_onehot = jnp.where(lane == pos // MOBA_BLOCK, 1.0, 0.0).astype(BF16)
        ones = jnp.ones((tm, hd), BF16)
    for a in range(n // hd):
        sl = slice(a * hd, (a + 1) * hd)
        kh = _head_rms(kk[:, sl], kn_ref[...])
        k_ref[:, sl] = kh
        if nsum:
            ka_ref[:, 2 * a * hd:(2 * a + 1) * hd] = kh.astype(BF16)
            ka_ref[:, (2 * a + 1) * hd:(2 * a + 2) * hd] = block_onehot
            va_ref[:, 2 * a * hd:(2 * a + 1) * hd] = vv[:, sl].astype(BF16)
            va_ref[:, (2 * a + 1) * hd:(2 * a + 2) * hd] = ones
            for r in range(nsum):
                ks_ref[r, :, sl] = jnp.sum(kh[r * MOBA_BLOCK:(r + 1) * MOBA_BLOCK], axis=0, keepdims=True)


def _kv(x, g, shift, scale, w_kv, k_norm, *, tm, seq, for_prompt):
    rows, D = x.shape
    n = w_kv.shape[1] // 2
    hd = k_norm.shape[1]
    tm = _row_tile(rows, tm)
    nsum = tm // MOBA_BLOCK if for_prompt else 0
    assert not for_prompt or (tm % MOBA_BLOCK == 0 and seq // MOBA_BLOCK <= hd)
    row = lambda m: pl.BlockSpec((tm, m), lambda i: (i, 0))
    full = lambda a: pl.BlockSpec(a.shape, lambda i: (0,) * a.ndim)
    out_shape = [jax.ShapeDtypeStruct((rows, n), F32), jax.ShapeDtypeStruct((rows, n), F32)]
    out_specs = [row(n), row(n)]
    if for_prompt:
        out_shape += [jax.ShapeDtypeStruct((rows, 2 * n), BF16), jax.ShapeDtypeStruct((rows, 2 * n), BF16),
                      jax.ShapeDtypeStruct((rows // MOBA_BLOCK, 1, n), F32)]
        out_specs += [row(2 * n), row(2 * n), pl.BlockSpec((nsum, 1, n), lambda i: (i, 0, 0))]
    return pl.pallas_call(
        functools.partial(_kv_kernel, hd=hd, nsum=nsum, seq=seq),
        out_shape=tuple(out_shape),
        grid=(rows // tm,),
        in_specs=[row(D), full(g), _mod_spec(shift, rows, tm, 1), _mod_spec(scale, rows, tm, 1),
                  full(w_kv), full(k_norm)],
        out_specs=tuple(out_specs),
        compiler_params=_params("parallel"),
        name="shared_kv",
    )(x, g, shift, scale, w_kv, k_norm)


def _q_kernel(x_ref, g_ref, sh_ref, sc_ref, w_ref, qn_ref, q_ref, *, hd):
    h = _prenorm(x_ref[...], g_ref[...], sh_ref[0], sc_ref[0]).astype(BF16)
    qq = jnp.dot(h, w_ref[...], preferred_element_type=F32)
    for a in range(q_ref.shape[1] // hd):
        sl = slice(a * hd, (a + 1) * hd)
        q_ref[:, sl] = _head_rms(qq[:, sl], qn_ref[...])


def _q_proj(x, g, shift, scale, w_q, q_norm, *, tm):
    rows, D = x.shape
    n = w_q.shape[1]
    tm = _row_tile(rows, tm)
    row = lambda m: pl.BlockSpec((tm, m), lambda i: (i, 0))
    full = lambda a: pl.BlockSpec(a.shape, lambda i: (0,) * a.ndim)
    return pl.pallas_call(
        functools.partial(_q_kernel, hd=q_norm.shape[1]),
        out_shape=jax.ShapeDtypeStruct((rows, n), F32),
        grid=(rows // tm,),
        in_specs=[row(D), full(g), _mod_spec(shift, rows, tm, 1), _mod_spec(scale, rows, tm, 1),
                  full(w_q), full(q_norm)],
        out_specs=row(n),
        compiler_params=_params("parallel"),
        name="moba_q_proj",
    )(x, g, shift, scale, w_q, q_norm)


def _top_blocks(scores, n_past, axis):
    nbk = scores.shape[axis]
    blk_id = lax.broadcasted_iota(jnp.int32, scores.shape, axis)
    sc = jnp.where(blk_id < n_past, scores, -jnp.inf)
    picks = []
    for r in range(MOBA_TOPK):
        m = jnp.max(sc, axis=axis, keepdims=True)
        idx = jnp.min(jnp.where(sc == m, blk_id, nbk), axis=axis, keepdims=True)
        picks.append((idx, (jnp.zeros_like(idx) + r) < n_past))
        sc = jnp.where(blk_id == idx, -jnp.inf, sc)
    return picks


def _q_sel_kernel(x_ref, g_ref, sh_ref, sc_ref, w_ref, qn_ref, ks_ref, qa_ref, *, hd, seq, scale):
    tm = x_ref.shape[0]
    nbk = ks_ref.shape[1]
    h = _prenorm(x_ref[...], g_ref[...], sh_ref[0], sc_ref[0]).astype(BF16)
    qq = jnp.dot(h, w_ref[...], preferred_element_type=F32)
    pos = (pl.program_id(0) * tm) % seq + lax.broadcasted_iota(jnp.int32, (1, tm), 1)
    own = pos // MOBA_BLOCK
    blk_id = lax.broadcasted_iota(jnp.int32, (nbk, tm), 0)
    for a in range(ATT_HEADS):
        sl = slice(a * hd, (a + 1) * hd)
        qh = _head_rms(qq[:, sl], qn_ref[...])
        scores = _dot_nt_3pass(ks_ref[0, :, sl] * (1.0 / MOBA_BLOCK), qh)
        chosen = blk_id < 0
        for idx, ok in _top_blocks(scores, own, 0):
            chosen = chosen | ((blk_id == idx) & ok)
        is_prev = blk_id == own - 1
        far = jnp.where(chosen & jnp.logical_not(is_prev), 0.0, NEG)
        prev = jnp.max(jnp.where(chosen & is_prev, 0.0, NEG), axis=0, keepdims=True)
        mask_t = jnp.concatenate([far, jnp.zeros((hd - nbk - SUBLANES, tm), F32),
                                  jnp.broadcast_to(prev, (SUBLANES, tm))], axis=0)
        qa_ref[:, 2 * a * hd:(2 * a + 1) * hd] = (qh * scale).astype(BF16)
        qa_ref[:, (2 * a + 1) * hd:(2 * a + 2) * hd] = mask_t.T.astype(BF16)


def _q_proj_sel(x, g, shift, scale, w_q, q_norm, ksum, *, tm, seq):
    rows, D = x.shape
    n = w_q.shape[1]
    hd = q_norm.shape[1]
    B, nbk, _ = ksum.shape
    tm = _row_tile(rows, tm)
    tiles_per_seq = seq // tm
    assert seq % tm == 0 and tm % LANES == 0 and nbk % SUBLANES == 0 and nbk + SUBLANES <= hd
    row = lambda m: pl.BlockSpec((tm, m), lambda i: (i, 0))
    full = lambda a: pl.BlockSpec(a.shape, lambda i: (0,) * a.ndim)
    return pl.pallas_call(
        functools.partial(_q_sel_kernel, hd=hd, seq=seq, scale=hd ** -0.5),
        out_shape=jax.ShapeDtypeStruct((rows, 2 * n), BF16),
        grid=(rows // tm,),
        in_specs=[row(D), full(g), _mod_spec(shift, rows, tm, 1), _mod_spec(scale, rows, tm, 1),
                  full(w_q), full(q_norm),
                  pl.BlockSpec((1, nbk, n), lambda i: (i // tiles_per_seq, 0, 0))],
        out_specs=row(2 * n),
        compiler_params=_params("parallel"),
        name="moba_q_proj_select",
    )(x, g, shift, scale, w_q, q_norm, ksum)


def _moba_prompt_kernel(pt_ref, qa_ref, ka_ref, va_ref, bias_ref, *rest,
                        blk, group, n_groups, halves, heads, n_page_refs, ppb):
    page_refs = rest[:n_page_refs]
    o_ref, ps_ref = rest[n_page_refs:n_page_refs + 2]
    m_ref, acc_ref, sa_ref, sb_ref, pa_ref, pb_ref, al_ref, pn_ref, m0_ref = rest[n_page_refs + 2:]
    for r in range(n_page_refs // ppb):
        total = jnp.sum(page_refs[r * ppb][0], axis=0)
        for p in range(1, ppb):
            total = total + jnp.sum(page_refs[r * ppb + p][0], axis=0)
        ps_ref[r] = total

    it = pl.program_id(2)
    hd = qa_ref.shape[1] // (2 * heads)
    last_group = n_groups - 1

    def lanes(hh):
        return slice(hh * 2 * hd, (hh + 1) * 2 * hd)

    def rows_of(g):
        return pl.ds(pl.multiple_of(g * (group * blk), group * blk), group * blk)

    def scores_of(hh, g):
        return lax.dot_general(qa_ref[:, lanes(hh)], ka_ref[rows_of(g), lanes(hh)], NT,
                               preferred_element_type=F32)

    for hh in range(heads):
        sa_ref[hh] = scores_of(hh, 0)

    t_io = lax.broadcasted_iota(jnp.int32, (blk, blk), 0)
    s_io = lax.broadcasted_iota(jnp.int32, (blk, blk), 1)

    def near_rows(hf):
        own = it * halves + hf
        return (pl.ds(pl.multiple_of(jnp.maximum(own - 1, 0) * blk, blk), blk),
                pl.ds(pl.multiple_of(own * blk, blk), blk))

    for hh, hf in [(hh, hf) for hf in range(halves) for hh in range(heads)]:
        rq = slice(hf * blk, (hf + 1) * blk)
        r_prev, r_own = near_rows(hf)
        k_lanes = slice(hh * 2 * hd, hh * 2 * hd + hd)
        qb = qa_ref[rq, k_lanes]
        prev_mask = qa_ref[rq, (hh + 1) * 2 * hd - 1:(hh + 1) * 2 * hd].astype(F32)
        s_own = lax.dot_general(qb, ka_ref[r_own, k_lanes], NT, preferred_element_type=F32)
        s_own = jnp.where(s_io <= t_io, s_own + bias_ref[hh, 0], NEG)
        s_prev = lax.dot_general(qb, ka_ref[r_prev, k_lanes], NT, preferred_element_type=F32)
        s_prev = s_prev + bias_ref[hh, 1] + prev_mask
        m0 = jnp.maximum(jnp.max(s_own, axis=1, keepdims=True), jnp.max(s_prev, axis=1, keepdims=True))
        pn_ref[hh, rq, :blk] = jnp.exp(s_prev - m0).astype(BF16)
        pn_ref[hh, rq, blk:] = jnp.exp(s_own - m0).astype(BF16)
        m_ref[hh, rq, :] = m0
        m0_ref[hh, rq, :] = m0
    acc_ref[...] = jnp.zeros_like(acc_ref)

    def softmax(hh, s_ref, p_ref, slot):
        m_prev = m_ref[hh]
        s = s_ref[hh]
        m_new = jnp.maximum(m_prev, jnp.max(s, axis=1, keepdims=True))
        p_ref[hh] = jnp.exp(s - m_new).astype(BF16)
        al_ref[hh, slot] = jnp.exp(m_prev - m_new)
        m_ref[hh] = m_new

    def flush(hh, p_ref, slot, g):
        acc_ref[hh] = al_ref[hh, slot] * acc_ref[hh] + jnp.dot(p_ref[hh], va_ref[rows_of(g), lanes(hh)],
                                                               preferred_element_type=F32)

    n_far = it * halves + halves - 2
    n_pairs = ((n_far + group - 1) // group + 1) // 2
    pb_ref[...] = jnp.zeros_like(pb_ref)
    for hh in range(heads):
        al_ref[hh, 1] = jnp.ones(al_ref.shape[2:], F32)

    def body(t, carry):
        for hh in range(heads):
            sb_ref[hh] = scores_of(hh, 2 * t + 1)
            flush(hh, pb_ref, 1, jnp.maximum(2 * t - 1, 0))
            softmax(hh, sa_ref, pa_ref, 0)
        for hh in range(heads):
            sa_ref[hh] = scores_of(hh, jnp.minimum(2 * t + 2, last_group))
            flush(hh, pa_ref, 0, 2 * t)
            softmax(hh, sb_ref, pb_ref, 1)
        return carry

    lax.fori_loop(0, n_pairs, body, 0)

    for hh in range(heads):
        flush(hh, pb_ref, 1, jnp.maximum(2 * n_pairs - 1, 0))
    for hh, hf in [(hh, hf) for hf in range(halves) for hh in range(heads)]:
        rq = slice(hf * blk, (hf + 1) * blk)
        r_prev, r_own = near_rows(hf)
        near = jnp.dot(pn_ref[hh, rq, :blk], va_ref[r_prev, lanes(hh)], preferred_element_type=F32)
        near = near + jnp.dot(pn_ref[hh, rq, blk:], va_ref[r_own, lanes(hh)], preferred_element_type=F32)
        acc = acc_ref[hh, rq, :] + jnp.exp(m0_ref[hh, rq, :] - m_ref[hh, rq, :]) * near
        o_ref[rq, hh * hd:(hh + 1) * hd] = (acc[:, :hd] / acc[:, hd:]).astype(o_ref.dtype)


def _moba_prompt(qa, ka, va, bias, cache_k, pt_flat, *, batch, seq, n_pages, first_block, blocks_per_step,
                 group=4, halves=MOBA_Q_TILE_BLOCKS, heads=1):
    rows, n2 = qa.shape
    hd = n2 // (2 * ATT_HEADS)
    blk = MOBA_BLOCK
    nq = seq // blk
    assert nq % (2 * group) == 0 and group >= 2 and nq % halves == 0 and ATT_HEADS % heads == 0
    nt = nq // halves
    tq = halves * blk
    hg = ATT_HEADS // heads
    steps = batch * hg * nt
    _, page, H, _ = cache_k.shape
    ppb = blk // page
    nbs = n_pages // ppb
    last_block = (pt_flat.shape[0] // n_pages) * nbs - 1
    bps = blocks_per_step

    def step_of(b, h, i):
        return (b * hg + h) * nt + i

    def page_spec(r, p):
        def index_map(b, h, i, pt):
            f = jnp.minimum(first_block + step_of(b, h, i) * bps + r, last_block)
            return (pt[(f // nbs) * n_pages + (f % nbs) * ppb + p], 0, 0, 0)
        return pl.BlockSpec((1, page, H, hd), index_map)

    page_specs = [page_spec(r, p) for r in range(bps) for p in range(ppb)]
    kern = functools.partial(_moba_prompt_kernel, blk=blk, group=group, n_groups=nq // group,
                             halves=halves, heads=heads, n_page_refs=len(page_specs), ppb=ppb)
    return pl.pallas_call(
        kern,
        out_shape=(jax.ShapeDtypeStruct((rows, ATT_HEADS * hd), BF16),
                   jax.ShapeDtypeStruct((steps * bps, H, hd), F32)),
        grid_spec=pltpu.PrefetchScalarGridSpec(
            num_scalar_prefetch=1,
            grid=(batch, hg, nt),
            in_specs=[pl.BlockSpec((tq, heads * 2 * hd), lambda b, h, i, pt: (b * nt + i, h)),
                      pl.BlockSpec((seq, heads * 2 * hd), lambda b, h, i, pt: (b, h)),
                      pl.BlockSpec((seq, heads * 2 * hd), lambda b, h, i, pt: (b, h)),
                      pl.BlockSpec((heads, 2, blk, blk), lambda b, h, i, pt: (h, 0, 0, 0))] + page_specs,
            out_specs=(pl.BlockSpec((tq, heads * hd), lambda b, h, i, pt: (b * nt + i, h)),
                       pl.BlockSpec((bps, H, hd), lambda b, h, i, pt: (step_of(b, h, i), 0, 0))),
            scratch_shapes=[pltpu.VMEM((heads, tq, 1), F32), pltpu.VMEM((heads, tq, 2 * hd), F32),
                            pltpu.VMEM((heads, tq, group * blk), F32), pltpu.VMEM((heads, tq, group * blk), F32),
                            pltpu.VMEM((heads, tq, group * blk), BF16), pltpu.VMEM((heads, tq, group * blk), BF16),
                            pltpu.VMEM((heads, 2, tq, 1), F32), pltpu.VMEM((heads, tq, 2 * blk), BF16),
                            pltpu.VMEM((heads, tq, 1), F32)]),
        compiler_params=_params("parallel", "parallel", "arbitrary"),
        name="moba_prompt_attention",
    )(pt_flat, qa, ka, va, bias, *([cache_k] * len(page_specs)))


def _sample_select_kernel(q_ref, ks_ref, idx_ref, *, hd, blk):
    nbk = ks_ref.shape[1]
    lane = lax.broadcasted_iota(jnp.int32, idx_ref.shape[1:], 1)
    out = jnp.zeros(idx_ref.shape[1:], jnp.int32)
    for a in range(ATT_HEADS):
        means = ks_ref[0, :, a, :] * (1.0 / blk)
        scores = _dot_nt_3pass(q_ref[0, :, a * hd:(a + 1) * hd], means)
        for r, (idx, _) in enumerate(_top_blocks(scores, nbk, 1)):
            out = jnp.where(lane == a * MOBA_TOPK + r, idx, out)
    idx_ref[0] = out


def _sample_select(q8, ksum):
    DB, R, n = q8.shape
    _, nbk, H, hd = ksum.shape
    assert nbk >= MOBA_TOPK and H == ATT_HEADS
    kern = functools.partial(_sample_select_kernel, hd=hd, blk=MOBA_BLOCK)
    return pl.pallas_call(
        kern,
        out_shape=jax.ShapeDtypeStruct((DB, R, LANES), jnp.int32),
        grid=(DB,),
        in_specs=[pl.BlockSpec((1, R, n), lambda d: (d, 0, 0)),
                  pl.BlockSpec((1, nbk, H, hd), lambda d: (d, 0, 0, 0))],
        out_specs=pl.BlockSpec((1, R, LANES), lambda d: (d, 0, 0)),
        compiler_params=_params("parallel"),
        name="sample_select",
    )(q8, ksum)


def _sample_attn_kernel(pt_ref, sel_ref, q_ref, kn_ref, vn_ref, bias_ref, ck_ref, cv_ref, o_ref,
                        kbuf, vbuf, sem, *, lq_n, ppb, n_pages, page, scale):
    g = pl.program_id(0)
    nslot = lq_n * MOBA_TOPK
    nsl = nslot * ppb

    def copies(step, slot):
        d = step // ATT_HEADS
        a = step % ATT_HEADS
        out = []
        for j in range(nsl):
            phys = pt_ref[d * n_pages + sel_ref[step * nslot + j // ppb] * ppb + j % ppb]
            rows = pl.ds(j * page, page)
            out.append(pltpu.make_async_copy(ck_ref.at[phys, :, a, :], kbuf.at[slot, rows, :], sem.at[slot, 0, j]))
            out.append(pltpu.make_async_copy(cv_ref.at[phys, :, a, :], vbuf.at[slot, rows, :], sem.at[slot, 1, j]))
        return out

    @pl.when(g == 0)
    def _():
        for c in copies(0, 0):
            c.start()

    @pl.when(g + 1 < pl.num_programs(0))
    def _():
        for c in copies(g + 1, (g + 1) % 2):
            c.start()

    slot = g % 2
    q = q_ref[0] * scale
    row = lax.broadcasted_iota(jnp.int32, (q.shape[0], 1), 0)

    new_logits = []
    for s in range(lq_n):
        sc = jnp.sum(q * kn_ref[0, s:s + 1, :], axis=1, keepdims=True) + bias_ref[0, 8:16, s:s + 1]
        new_logits.append(jnp.where(s <= row, sc, NEG))

    for c in copies(g, slot):
        c.wait()

    s_all = lax.dot_general(q.astype(BF16), kbuf[slot].astype(BF16), NT, preferred_element_type=F32)
    parts = []
    for j in range(nsl):
        lp = sel_ref[g * nslot + j // ppb] * ppb + j % ppb
        is_last = (jnp.zeros((q.shape[0], page), jnp.int32) + lp) == n_pages - 1
        sj = s_all[:, j * page:(j + 1) * page] + jnp.where(is_last, bias_ref[0, 0:8, :], 0.0)
        parts.append(jnp.where(row == j // (MOBA_TOPK * ppb), sj, NEG))
    m = jnp.max(functools.reduce(jnp.maximum, parts), axis=1, keepdims=True)
    m = functools.reduce(jnp.maximum, new_logits, m)
    probs = [jnp.exp(p - m) for p in parts]
    den = jnp.sum(functools.reduce(jnp.add, probs), axis=1, keepdims=True)
    pcat = jnp.concatenate([p.astype(BF16) for p in probs], axis=1)
    acc = jnp.dot(pcat, vbuf[slot].astype(BF16), preferred_element_type=F32)
    for s in range(lq_n):
        e = jnp.exp(new_logits[s] - m)
        den = den + e
        acc = acc + e * vn_ref[0, s:s + 1, :]
    o_ref[0] = acc / den


def _sample_attn(q8, kn8, vn8, bias, cache_k, cache_v, pt_flat, sel_flat, *, lq_n, n_pages):
    DB, R, n = q8.shape
    _, page, H, hd = cache_k.shape
    ppb = MOBA_BLOCK // page
    nsl = lq_n * MOBA_TOPK * ppb
    assert page == LANES and R == SUBLANES and H == ATT_HEADS
    kern = functools.partial(_sample_attn_kernel, lq_n=lq_n, ppb=ppb, n_pages=n_pages, page=page,
                             scale=hd ** -0.5)
    head = pl.BlockSpec((1, R, hd), lambda g, pt, sel: (g // ATT_HEADS, 0, g % ATT_HEADS))
    return pl.pallas_call(
        kern,
        out_shape=jax.ShapeDtypeStruct((DB, R, n), F32),
        grid_spec=pltpu.PrefetchScalarGridSpec(
            num_scalar_prefetch=2,
            grid=(DB * ATT_HEADS,),
            in_specs=[head, head, head,
                      pl.BlockSpec((1, 2 * SUBLANES, LANES), lambda g, pt, sel: (g % ATT_HEADS, 0, 0)),
                      pl.BlockSpec(memory_space=pl.ANY), pl.BlockSpec(memory_space=pl.ANY)],
            out_specs=head,
            scratch_shapes=[pltpu.VMEM((2, nsl * page, hd), F32), pltpu.VMEM((2, nsl * page, hd), F32),
                            pltpu.SemaphoreType.DMA((2, 2, nsl))]),
        compiler_params=_params("arbitrary"),
        name="sample_attention",
    )(pt_flat, sel_flat, q8, kn8, vn8, bias, cache_k, cache_v)


def _bucket_of_distance():
    n = np.arange(MAX_DISTANCE)
    max_exact = N_BUCKETS // 2
    nf = np.maximum(n, max_exact).astype(np.float32)
    large = max_exact + (np.log(nf / np.float32(max_exact)) / np.float32(math.log(MAX_DISTANCE / max_exact))
                         * np.float32(N_BUCKETS - max_exact)).astype(np.int32)
    return np.where(n < max_exact, n, np.minimum(large, N_BUCKETS - 1)).astype(np.int32)


def _dist_table(rel_bias, width):
    t = rel_bias.astype(F32)[_bucket_of_distance()].T
    t = t - t[:, MAX_DISTANCE - 1:]
    return jnp.pad(t, ((0, 0), (0, width - MAX_DISTANCE)))


def _toeplitz(first_row_wrapped, n):
    H = first_row_wrapped.shape[0]
    x = jnp.broadcast_to(first_row_wrapped[:, None, :], (H, n, 2 * n)).reshape(H, 2 * n * n)
    return x[:, :n * (2 * n - 1)].reshape(H, n, 2 * n - 1)[:, :, :n]


def _prompt_bias_tiles(rel_bias):
    blk = MOBA_BLOCK
    assert blk >= MAX_DISTANCE
    t = _dist_table(rel_bias, blk)
    rev = jnp.flip(t[:, 1:], axis=1)
    zeros = jnp.zeros_like(t)
    own = _toeplitz(jnp.concatenate([t[:, :1], zeros, rev], axis=1), blk)
    prev = _toeplitz(jnp.concatenate([zeros[:, :1], rev, zeros], axis=1), blk)
    return jnp.stack([own, prev], axis=1)


def _sample_bias_rows(rel_bias, *, page, lq_n):
    assert page == LANES and page >= MAX_DISTANCE and lq_n <= SUBLANES
    t = _dist_table(rel_bias, 2 * page)
    r = np.arange(SUBLANES)[:, None]
    c = np.arange(LANES)[None, :]
    return jnp.concatenate([t[:, page + r - c], t[:, np.clip(r - c, 0, None)]], axis=1)


def _hi_lo(w):
    hi = w.astype(BF16)
    return jnp.stack([hi, (w - hi.astype(F32)).astype(BF16)], axis=1)


def _pad_tokens(a, groups, per, to):
    return jnp.pad(a.reshape(groups, per, a.shape[-1]), ((0, 0), (0, to - per), (0, 0)))


def _trunk(x, mods, kvmod, s0, W, *, batch, seq, tm, attend, make_ctx):
    n_a = W['gla_w_main'].shape[0]
    depth = W['mlp_w_up_f32'].shape[0]
    dk = W['gla_b_gate'].shape[2]
    dv = W['gla_w_out'].shape[1]
    states = []
    k = v = ctx = None
    for i in range(depth):
        sh1, sc1, g1, sh2, sc2, g2 = mods[i]
        if i == n_a:
            k, v, ctx = make_ctx(x, kvmod)
        if i < n_a:
            qk, vv, gt, la = _gla_in(x, W['norm_mix'][i], sh1, sc1, W['gla_w_main'][i], W['gla_w_r'][i],
                                     W['gla_w_g2'][i], W['gla_b_gate'][i], dk=dk, dv=dv, tm=tm)
            chunk = math.gcd(GLA_CHUNK, seq)
            if chunk < 16:
                padded = 16
                qk, vv, la = (_pad_tokens(t, batch, seq, padded).reshape(batch * padded, -1)
                              for t in (qk, vv, la))
                o, s = _gla_rec(qk, vv, la, s0[i], seq=padded, chunk=padded, tl=padded)
                o = o.reshape(batch, padded, dv)[:, :seq].reshape(batch * seq, dv)
            elif 'mlp_w_up' not in W:
                f32_weights = (W['mlp_w_up_f32'], W['mlp_w_down_f32'])
                steps = batch * (seq // min(1024, seq))
                if all(_cast_blocks(w.shape) <= steps for w in f32_weights):
                    o, s, W['mlp_w_up'], W['mlp_w_down'] = _gla_rec(qk, vv, la, s0[i], seq=seq, chunk=chunk,
                                                                    tl=1024, cast_weights=f32_weights)
                else:
                    W['mlp_w_up'], W['mlp_w_down'] = (w.astype(BF16) for w in f32_weights)
                    o, s = _gla_rec(qk, vv, la, s0[i], seq=seq, chunk=chunk, tl=1024)
            else:
                o, s = _gla_rec(qk, vv, la, s0[i], seq=seq, chunk=chunk, tl=1024)
            states.append(s)
            mix_inputs, w_mix = (o, gt, W['gla_onorm'][i]), W['gla_w_out'][i]
        else:
            j = i - n_a
            q_args = (W['norm_mix'][i], sh1, sc1, W['moba_w_q'][j], W['q_norm'][j])
            mix_inputs, w_mix = (attend(x, q_args, ctx, j),), W['moba_w_o'][j]
        x = _mix_mlp(mix_inputs, x, w_mix, g1, W['norm_mlp'][i], sh2, sc2, g2,
                     W['mlp_w_up'][i], W['mlp_w_down'][i], tm=min(tm * 2, 1024), tf=1024)
    return x, jnp.stack(states), k, v


def kernel(x_prompt, x_sample, state_gla, cache_k, cache_v, page_table, c_prompt, c_sample,
           ada_w, ada_b, norm_mix, norm_mlp, gla_w_in, gla_w_gate2, gla_b_gate, gla_onorm, gla_w_out,
           kv_ada_w, kv_ada_b, kv_norm, w_kv, k_norm, moba_w_q, q_norm, moba_w_o, rel_bias,
           mlp_w_up, mlp_w_down):
    B, L, D = x_prompt.shape
    DB, LQ, _ = x_sample.shape
    depth = ada_w.shape[0]
    n_a = gla_w_in.shape[0]
    rank, dk = gla_w_gate2.shape[1:]
    dv = gla_w_out.shape[1]
    n_pool, page, H, hd = cache_k.shape
    n_pages = page_table.shape[1]
    past = n_pages * page
    assert H == ATT_HEADS and L % MOBA_BLOCK == 0 and MOBA_BLOCK % page == 0
    assert past % MOBA_BLOCK == 0 and LQ <= min(SUBLANES, MOBA_BLOCK) and rank <= LANES

    W = {
        'norm_mix': norm_mix.reshape(depth, 1, D), 'norm_mlp': norm_mlp.reshape(depth, 1, D),
        'gla_w_main': gla_w_in[:, :, :2 * dk + 2 * dv].astype(BF16),
        'gla_w_r': jnp.pad(gla_w_in[:, :, 2 * dk + 2 * dv:], ((0, 0), (0, 0), (0, LANES - rank))).astype(BF16),
        'gla_w_g2': _hi_lo(jnp.pad(gla_w_gate2, ((0, 0), (0, LANES - rank), (0, 0)))),
        'gla_b_gate': gla_b_gate.reshape(n_a, 1, dk), 'gla_onorm': gla_onorm.reshape(n_a, 1, -1),
        'gla_w_out': gla_w_out.astype(BF16),
        'moba_w_q': moba_w_q.astype(BF16), 'q_norm': q_norm.reshape(-1, 1, hd),
        'moba_w_o': moba_w_o.astype(BF16),
        'mlp_w_up_f32': mlp_w_up, 'mlp_w_down_f32': mlp_w_down,
    }
    w_kv_b = w_kv.astype(BF16)
    kv_norm2 = kv_norm.reshape(1, D)
    k_norm2 = k_norm.reshape(1, hd)

    n_c = B + DB
    c_all = jnp.pad(jnp.concatenate([c_prompt, c_sample], axis=0), ((0, -n_c % SUBLANES), (0, 0)))
    mod = _ada(c_all, ada_w, ada_b)
    kvm = _ada(c_all, kv_ada_w[None], kv_ada_b[None])[0]

    def prompt_mod(m):
        return m[:B].reshape(B, 1, D)

    def sample_mod(m):
        return jnp.repeat(m[B:n_c], LQ, axis=0).reshape(1, DB * LQ, D)

    def split_mods(pick):
        layers = [[pick(mod[i, :, t * D:(t + 1) * D]) for t in range(6)] for i in range(depth)]
        return layers, [pick(kvm[:, t * D:(t + 1) * D]) for t in range(2)]

    p_mods, p_kvmod = split_mods(prompt_mod)
    p_bias = _prompt_bias_tiles(rel_bias)

    def prompt_ctx(x, kvmod):
        k, v, ka, va, ksum = _kv(x, kv_norm2, kvmod[0], kvmod[1], w_kv_b, k_norm2, tm=512, seq=L,
                                 for_prompt=True)
        return k, v, (ka, va, ksum.reshape(B, L // MOBA_BLOCK, H * hd))

    pt_flat = page_table.reshape(-1).astype(jnp.int32)
    n_sample_blocks = DB * (n_pages // (MOBA_BLOCK // page))
    moba_steps = B * ATT_HEADS * (L // (MOBA_Q_TILE_BLOCKS * MOBA_BLOCK))
    blocks_per_step = -(-n_sample_blocks // ((depth - n_a) * moba_steps))
    sample_block_sums = []

    def prompt_attend(x, q_args, ctx, j):
        ka, va, ksum = ctx
        qa = _q_proj_sel(x, *q_args, ksum, tm=512, seq=L)
        o, sums = _moba_prompt(qa, ka, va, p_bias, cache_k, pt_flat, batch=B, seq=L, n_pages=n_pages,
                               first_block=j * moba_steps * blocks_per_step, blocks_per_step=blocks_per_step)
        sample_block_sums.append(sums)
        return o

    s0_p = jnp.zeros((n_a, B) + state_gla.shape[2:], F32)
    y_p, st_p, k_p, v_p = _trunk(x_prompt.reshape(B * L, D), p_mods, p_kvmod, s0_p, W, batch=B, seq=L,
                                 tm=512, attend=prompt_attend, make_ctx=prompt_ctx)

    s_mods, s_kvmod = split_mods(sample_mod)
    s_bias = _sample_bias_rows(rel_bias, page=page, lq_n=LQ)
    rows_s = DB * LQ

    def sample_ctx(x, kvmod):
        k, v = _kv(x, kv_norm2, kvmod[0], kvmod[1], w_kv_b, k_norm2, tm=rows_s, seq=LQ, for_prompt=False)
        ksum = jnp.concatenate(sample_block_sums, axis=0)[:n_sample_blocks].reshape(DB, -1, H, hd)
        return k, v, (_pad_tokens(k, DB, LQ, SUBLANES), _pad_tokens(v, DB, LQ, SUBLANES), ksum)

    def sample_attend(x, q_args, ctx, j):
        kn8, vn8, ksum = ctx
        q8 = _pad_tokens(_q_proj(x, *q_args, tm=rows_s), DB, LQ, SUBLANES)
        picks = _sample_select(q8, ksum)[:, :LQ, :ATT_HEADS * MOBA_TOPK]
        sel_flat = picks.reshape(DB, LQ, ATT_HEADS, MOBA_TOPK).transpose(0, 2, 1, 3).reshape(-1)
        o8 = _sample_attn(q8, kn8, vn8, s_bias, cache_k, cache_v, pt_flat, sel_flat,
                          lq_n=LQ, n_pages=n_pages)
        return o8[:, :LQ].reshape(rows_s, H * hd)

    y_s, st_s, k_s, v_s = _trunk(x_sample.reshape(rows_s, D), s_mods, s_kvmod, state_gla.astype(F32), W,
                                 batch=DB, seq=LQ, tm=rows_s, attend=sample_attend, make_ctx=sample_ctx)

    return (y_p.reshape(B, L, D), y_s.reshape(DB, LQ, D), st_p, st_s,
            k_p.reshape(B, L, H, hd), v_p.reshape(B, L, H, hd),
            k_s.reshape(DB, LQ, H, hd), v_s.reshape(DB, LQ, H, hd))
```

```python
import functools
import math

import numpy as np
import jax
import jax.numpy as jnp
from jax import lax
from jax.experimental import pallas as pl
from jax.experimental.pallas import tpu as pltpu

F32 = jnp.float32
BF16 = jnp.bfloat16

GLA_HEADS = 4
GLA_GATE_NORM = 16.0
GLA_CHUNK = 64
ATT_HEADS = 8
MOBA_BLOCK = 256
MOBA_TOPK = 3
MOBA_Q_TILE_BLOCKS = 2
N_BUCKETS = 32
MAX_DISTANCE = 128
EPS = 1e-6

LANES = 128
SUBLANES = 8
VMEM_LIMIT = 48 * 1024 * 1024
MIX_MLP_VMEM_LIMIT = 56 * 1024 * 1024
NEG = -1e30
NT = (((1,), (1,)), ((), ()))
TN = (((0,), (0,)), ((), ()))


def _params(*sem):
    return pltpu.CompilerParams(dimension_semantics=sem, vmem_limit_bytes=VMEM_LIMIT)


def _prenorm(x, g, shift, scale):
    ms = jnp.mean(x * x, axis=-1, keepdims=True)
    return (x * lax.rsqrt(ms + EPS)) * g * (1.0 + scale) + shift


def _head_rms(y, g):
    return y * lax.rsqrt(jnp.mean(y * y, axis=-1, keepdims=True) + EPS) * g


def _dot_nt_3pass(a, b):
    a_hi = a.astype(BF16)
    b_hi = b.astype(BF16)
    a_lo = (a - a_hi.astype(F32)).astype(BF16)
    b_lo = (b - b_hi.astype(F32)).astype(BF16)
    dot = functools.partial(lax.dot_general, dimension_numbers=NT, preferred_element_type=F32)
    return dot(a_hi, b_hi) + dot(a_lo, b_hi) + dot(a_hi, b_lo)


def _mod_rows(mod_ref, rows):
    return mod_ref[0] if mod_ref.shape[1] == 1 else mod_ref[0, rows, :]


def _row_tile(rows, want):
    tm = min(rows, want)
    assert rows % tm == 0
    return tm


def _mod_spec(mod, rows, tm, ngrid):
    G, R, D = mod.shape
    tiles_per_group = rows // (G * tm)
    assert tiles_per_group * G * tm == rows and R in (1, tm)
    if ngrid == 1:
        return pl.BlockSpec((1, R, D), lambda i: (i // tiles_per_group, 0, 0))
    return pl.BlockSpec((1, R, D), lambda i, k: (i // tiles_per_group, 0, 0))


def _ada_kernel(c_ref, w_ref, b_ref, o_ref):
    c = c_ref[...]
    a = (c * jax.nn.sigmoid(c)).astype(BF16)
    o_ref[...] = jnp.dot(a, w_ref[...].astype(BF16), preferred_element_type=F32) + b_ref[...]


def _ada(c, w, b, tn=1024):
    NL, D, N = w.shape
    M = c.shape[0]
    return pl.pallas_call(
        _ada_kernel,
        out_shape=jax.ShapeDtypeStruct((NL, M, N), F32),
        grid=(NL, N // tn),
        in_specs=[pl.BlockSpec((M, D), lambda l, j: (0, 0)),
                  pl.BlockSpec((None, D, tn), lambda l, j: (l, 0, j)),
                  pl.BlockSpec((None, 1, tn), lambda l, j: (l, 0, j))],
        out_specs=pl.BlockSpec((None, M, tn), lambda l, j: (l, 0, j)),
        compiler_params=_params("parallel", "parallel"),
        name="ada_modulation",
    )(c, w, b.reshape(NL, 1, N))


def _gla_in_kernel(x_ref, g_ref, sh_ref, sc_ref, w_ref, wr_ref, wg2_ref, bg_ref,
                   qk_ref, v_ref, gt_ref, la_ref, *, dk, dv, qscale):
    dot = functools.partial(jnp.dot, preferred_element_type=F32)
    tm = x_ref.shape[0]
    n_split = 2 if tm % (2 * LANES) == 0 else 1
    for part in range(n_split):
        rs = slice(part * (tm // n_split), (part + 1) * (tm // n_split))
        h = _prenorm(x_ref[rs, :], g_ref[...], _mod_rows(sh_ref, rs), _mod_rows(sc_ref, rs)).astype(BF16)
        r = dot(h, wr_ref[...])
        r_hi = r.astype(BF16)
        r_lo = (r - r_hi.astype(F32)).astype(BF16)
        z = dot(r_hi, wg2_ref[0]) + dot(r_lo, wg2_ref[0]) + dot(r_hi, wg2_ref[1]) + bg_ref[...]
        log_sig = jnp.minimum(z, 0.0) - jnp.log1p(jnp.exp(-jnp.abs(z)))
        la_ref[rs, :] = log_sig * (1.0 / GLA_GATE_NORM)
        qk_ref[rs, :dk] = (dot(h, w_ref[:, :dk]) * qscale).astype(qk_ref.dtype)
        qk_ref[rs, dk:] = dot(h, w_ref[:, dk:2 * dk]).astype(qk_ref.dtype)
        v_ref[rs, :] = dot(h, w_ref[:, 2 * dk:2 * dk + dv]).astype(v_ref.dtype)
        gt_ref[rs, :] = dot(h, w_ref[:, 2 * dk + dv:]).astype(gt_ref.dtype)


def _gla_in(x, g, shift, scale, w_main, w_r, w_g2, b_g, *, dk, dv, tm):
    rows, D = x.shape
    tm = _row_tile(rows, tm)
    kern = functools.partial(_gla_in_kernel, dk=dk, dv=dv, qscale=(dk // GLA_HEADS) ** -0.5)
    full = lambda a: pl.BlockSpec(a.shape, lambda i: (0,) * a.ndim)
    row = lambda n: pl.BlockSpec((tm, n), lambda i: (i, 0))
    return pl.pallas_call(
        kern,
        out_shape=(jax.ShapeDtypeStruct((rows, 2 * dk), BF16), jax.ShapeDtypeStruct((rows, dv), BF16),
                   jax.ShapeDtypeStruct((rows, dv), BF16), jax.ShapeDtypeStruct((rows, dk), F32)),
        grid=(rows // tm,),
        in_specs=[row(D), full(g), _mod_spec(shift, rows, tm, 1), _mod_spec(scale, rows, tm, 1),
                  full(w_main), full(w_r), full(w_g2), full(b_g)],
        out_specs=(row(2 * dk), row(dv), row(dv), row(dk)),
        compiler_params=_params("parallel"),
        name="gla_in_proj",
    )(x, g, shift, scale, w_main, w_r, w_g2, b_g)


def _gla_rec_kernel(q_ref, k_ref, v_ref, la_ref, s0_ref, *rest, chunk, n_inner, n_cast):
    cast_in = rest[:n_cast]
    o_ref, s_ref = rest[n_cast:n_cast + 2]
    cast_out = rest[n_cast + 2:2 * n_cast + 2]
    st_ref = rest[2 * n_cast + 2]
    for src, dst in zip(cast_in, cast_out):
        dst[...] = src[...].astype(dst.dtype)

    lt = pl.program_id(1)
    H, dvh, dkh = st_ref.shape

    @pl.when(lt == 0)
    def _():
        for hd in range(H):
            st_ref[hd] = s0_ref[0, hd].T

    r_io = lax.broadcasted_iota(jnp.int32, (chunk, chunk), 0)
    c_io = lax.broadcasted_iota(jnp.int32, (chunk, chunk), 1)
    lower = r_io >= c_io
    tri = jnp.where(lower, 1.0, 0.0).astype(BF16)
    mid = chunk // 2 - 1

    def chunk_step(c):
        rows = pl.ds(pl.multiple_of(c * chunk, chunk), chunk)
        la = la_ref[rows, :]
        la_hi = la.astype(BF16)
        rest = la - la_hi.astype(F32)
        la_mid = rest.astype(BF16)
        la_lo = (rest - la_mid.astype(F32)).astype(BF16)
        b_all = sum(jnp.dot(tri, part, preferred_element_type=F32) for part in (la_hi, la_mid, la_lo))
        for hd in range(H):
            ks = slice(hd * dkh, (hd + 1) * dkh)
            vs = slice(hd * dvh, (hd + 1) * dvh)
            q = q_ref[rows, ks].astype(F32)
            k = k_ref[rows, ks].astype(F32)
            vb = v_ref[rows, vs]
            b = b_all[:, ks]
            b_mid = b[mid:mid + 1, :]
            b_end = b[chunk - 1:chunk, :]
            qm = (q * jnp.exp(b - b_mid)).astype(BF16)
            km = (k * jnp.exp(b_mid - b)).astype(BF16)
            a = lax.dot_general(qm, km, NT, preferred_element_type=F32)
            a = jnp.where(lower, a, 0.0).astype(BF16)
            st = st_ref[hd]
            qd = (q * jnp.exp(b)).astype(BF16)
            o = jnp.dot(a, vb, preferred_element_type=F32)
            o = o + lax.dot_general(qd, st.astype(BF16), NT, preferred_element_type=F32)
            o_ref[rows, vs] = o.astype(o_ref.dtype)
            ke = (k * jnp.exp(b_end - b)).astype(BF16)
            upd = lax.dot_general(vb, ke, TN, preferred_element_type=F32)
            st_ref[hd] = st * jnp.exp(b_end) + upd

    unroll = 2 if n_inner % 2 == 0 else 1

    def body(i, carry):
        for u in range(unroll):
            chunk_step(i * unroll + u)
        return carry

    lax.fori_loop(0, n_inner // unroll, body, 0)

    @pl.when(lt == pl.num_programs(1) - 1)
    def _():
        for hd in range(H):
            s_ref[0, hd] = st_ref[hd].T


def _cast_blocks(shape):
    n, R, C = shape
    side = min(R, C)
    return n * (R // side) * (C // side)


def _cast_block_map(shape, nlt):
    _, R, C = shape
    side = min(R, C)
    rb, cb = R // side, C // side
    last = _cast_blocks(shape) - 1

    def index_map(b, t):
        f = jnp.minimum(b * nlt + t, last)
        return (f // (rb * cb), (f // cb) % rb, f % cb)
    return index_map


def _gla_rec(qk, v, la, s0, *, seq, chunk, tl, cast_weights=()):
    B, H, dkh, dvh = s0.shape
    tl = min(tl, seq)
    assert seq % tl == 0 and tl % chunk == 0
    nlt = seq // tl
    cast_specs = []
    for w in cast_weights:
        assert _cast_blocks(w.shape) <= B * nlt
        side = min(w.shape[1:])
        cast_specs.append(pl.BlockSpec((1, side, side), _cast_block_map(w.shape, nlt)))
    kern = functools.partial(_gla_rec_kernel, chunk=chunk, n_inner=tl // chunk, n_cast=len(cast_weights))
    out_shape = [jax.ShapeDtypeStruct(v.shape, BF16), jax.ShapeDtypeStruct(s0.shape, F32)]
    out_shape += [jax.ShapeDtypeStruct(w.shape, BF16) for w in cast_weights]
    cast_in_specs = cast_out_specs = cast_specs
    cast_in = list(cast_weights)
    return pl.pallas_call(
        kern,
        out_shape=tuple(out_shape),
        grid=(B, nlt),
        in_specs=[pl.BlockSpec((tl, H * dkh), lambda b, t: (b * nlt + t, 0)),
                  pl.BlockSpec((tl, H * dkh), lambda b, t: (b * nlt + t, 1)),
                  pl.BlockSpec((tl, H * dvh), lambda b, t: (b * nlt + t, 0)),
                  pl.BlockSpec((tl, H * dkh), lambda b, t: (b * nlt + t, 0)),
                  pl.BlockSpec((1, H, dkh, dvh), lambda b, t: (b, 0, 0, 0))] + cast_in_specs,
        out_specs=tuple([pl.BlockSpec((tl, H * dvh), lambda b, t: (b * nlt + t, 0)),
                         pl.BlockSpec((1, H, dkh, dvh), lambda b, t: (b, 0, 0, 0))] + cast_out_specs),
        scratch_shapes=[pltpu.VMEM((H, dvh, dkh), F32)],
        compiler_params=_params("parallel", "arbitrary"),
        name="gla_recurrence",
    )(qk, qk, v, la, s0, *cast_in)


def _mix_mlp_kernel(*refs, gla):
    if gla:
        a_ref, gt_ref, on_ref = refs[:3]
        refs = refs[3:]
    else:
        a_ref = refs[0]
        refs = refs[1:]
    x_ref, wo_ref, g1_ref, g_ref, sh_ref, sc_ref, g2_ref, wu_ref, wd_ref, out_ref, h_ref = refs
    k = pl.program_id(1)

    @pl.when(k == 0)
    def _():
        tm = x_ref.shape[0]
        n_split = 4 if tm % (4 * LANES) == 0 else 1
        for part in range(n_split):
            rs = slice(part * (tm // n_split), (part + 1) * (tm // n_split))
            if gla:
                dvh = a_ref.shape[1] // GLA_HEADS
                mix = jnp.zeros((tm // n_split, out_ref.shape[1]), F32)
                for hd in range(GLA_HEADS):
                    sl = slice(hd * dvh, (hd + 1) * dvh)
                    y = _head_rms(a_ref[rs, sl].astype(F32), on_ref[...])
                    gt = gt_ref[rs, sl].astype(F32)
                    y = y * (gt * jax.nn.sigmoid(gt))
                    mix = mix + jnp.dot(y.astype(BF16), wo_ref[sl, :], preferred_element_type=F32)
            else:
                mix = jnp.dot(a_ref[rs, :].astype(BF16), wo_ref[...], preferred_element_type=F32)
            x1 = x_ref[rs, :] + _mod_rows(g1_ref, rs) * mix
            out_ref[rs, :] = x1
            h_ref[rs, :] = _prenorm(x1, g_ref[...], _mod_rows(sh_ref, rs), _mod_rows(sc_ref, rs)).astype(BF16)

    u = jnp.dot(h_ref[...], wu_ref[...], preferred_element_type=F32)
    u = jnp.square(jnp.maximum(u, 0.0)).astype(BF16)
    out_ref[...] += g2_ref[0] * jnp.dot(u, wd_ref[...], preferred_element_type=F32)


def _mix_mlp(mix_inputs, x, w_out, gate1, g, shift, scale, gate2, w_up, w_down, *, tm, tf):
    rows, D = x.shape
    FF = w_up.shape[1]
    tm = _row_tile(rows, tm)
    gla = len(mix_inputs) == 3
    row = lambda a: pl.BlockSpec((tm, a.shape[1]), lambda i, k: (i, 0))
    full = lambda a: pl.BlockSpec(a.shape, lambda i, k: (0,) * a.ndim)
    mix_specs = [row(mix_inputs[0]), row(mix_inputs[1]), full(mix_inputs[2])] if gla else [row(mix_inputs[0])]
    return pl.pallas_call(
        functools.partial(_mix_mlp_kernel, gla=gla),
        out_shape=jax.ShapeDtypeStruct((rows, D), F32),
        grid=(rows // tm, FF // tf),
        in_specs=mix_specs + [row(x), full(w_out), _mod_spec(gate1, rows, tm, 2), full(g),
                              _mod_spec(shift, rows, tm, 2), _mod_spec(scale, rows, tm, 2),
                              _mod_spec(gate2, rows, tm, 2),
                              pl.BlockSpec((D, tf), lambda i, k: (0, k)),
                              pl.BlockSpec((tf, D), lambda i, k: (k, 0))],
        out_specs=pl.BlockSpec((tm, D), lambda i, k: (i, 0)),
        scratch_shapes=[pltpu.VMEM((tm, D), BF16)],
        compiler_params=pltpu.CompilerParams(dimension_semantics=("parallel", "arbitrary"),
                                             vmem_limit_bytes=MIX_MLP_VMEM_LIMIT),
        name="gla_out_mlp" if gla else "moba_out_mlp",
    )(*mix_inputs, x, w_out, gate1, g, shift, scale, gate2, w_up, w_down)


def _kv_kernel(x_ref, g_ref, sh_ref, sc_ref, w_ref, kn_ref, k_ref, v_ref, *rest, hd, nsum, seq):
    h = _prenorm(x_ref[...], g_ref[...], sh_ref[0], sc_ref[0]).astype(BF16)
    n = k_ref.shape[1]
    tm = k_ref.shape[0]
    kk = jnp.dot(h, w_ref[:, :n], preferred_element_type=F32)
    vv = jnp.dot(h, w_ref[:, n:], preferred_element_type=F32)
    v_ref[...] = vv
    if nsum:
        ka_ref, va_ref, ks_ref = rest
        pos = (pl.program_id(0) * tm) % seq + lax.broadcasted_iota(jnp.int32, (tm, hd), 0)
        lane = lax.broadcasted_iota(jnp.int32, (tm, hd), 1)
        block_onehot = jnp.where(lane == pos // MOBA_BLOCK, 1.0, 0.0).astype(BF16)
        ones = jnp.ones((tm, hd), BF16)
    for a in range(n // hd):
        sl = slice(a * hd, (a + 1) * hd)
        kh = _head_rms(kk[:, sl], kn_ref[...])
        k_ref[:, sl] = kh
        if nsum:
            ka_ref[:, 2 * a * hd:(2 * a + 1) * hd] = kh.astype(BF16)
            ka_ref[:, (2 * a + 1) * hd:(2 * a + 2) * hd] = block_onehot
            va_ref[:, 2 * a * hd:(2 * a + 1) * hd] = vv[:, sl].astype(BF16)
            va_ref[:, (2 * a + 1) * hd:(2 * a + 2) * hd] = ones
            for r in range(nsum):
                ks_ref[r, :, sl] = jnp.sum(kh[r * MOBA_BLOCK:(r + 1) * MOBA_BLOCK], axis=0, keepdims=True)


def _kv(x, g, shift, scale, w_kv, k_norm, *, tm, seq, for_prompt):
    rows, D = x.shape
    n = w_kv.shape[1] // 2
    hd = k_norm.shape[1]
    tm = _row_tile(rows, tm)
    nsum = tm // MOBA_BLOCK if for_prompt else 0
    assert not for_prompt or (tm % MOBA_BLOCK == 0 and seq // MOBA_BLOCK <= hd)
    row = lambda m: pl.BlockSpec((tm, m), lambda i: (i, 0))
    full = lambda a: pl.BlockSpec(a.shape, lambda i: (0,) * a.ndim)
    out_shape = [jax.ShapeDtypeStruct((rows, n), F32), jax.ShapeDtypeStruct((rows, n), F32)]
    out_specs = [row(n), row(n)]
    if for_prompt:
        out_shape += [jax.ShapeDtypeStruct((rows, 2 * n), BF16), jax.ShapeDtypeStruct((rows, 2 * n), BF16),
                      jax.ShapeDtypeStruct((rows // MOBA_BLOCK, 1, n), F32)]
        out_specs += [row(2 * n), row(2 * n), pl.BlockSpec((nsum, 1, n), lambda i: (i, 0, 0))]
    return pl.pallas_call(
        functools.partial(_kv_kernel, hd=hd, nsum=nsum, seq=seq),
        out_shape=tuple(out_shape),
        grid=(rows // tm,),
        in_specs=[row(D), full(g), _mod_spec(shift, rows, tm, 1), _mod_spec(scale, rows, tm, 1),
                  full(w_kv), full(k_norm)],
        out_specs=tuple(out_specs),
        compiler_params=_params("parallel"),
        name="shared_kv",
    )(x, g, shift, scale, w_kv, k_norm)


def _q_kernel(x_ref, g_ref, sh_ref, sc_ref, w_ref, qn_ref, q_ref, *, hd):
    h = _prenorm(x_ref[...], g_ref[...], sh_ref[0], sc_ref[0]).astype(BF16)
    qq = jnp.dot(h, w_ref[...], preferred_element_type=F32)
    for a in range(q_ref.shape[1] // hd):
        sl = slice(a * hd, (a + 1) * hd)
        q_ref[:, sl] = _head_rms(qq[:, sl], qn_ref[...])


def _q_proj(x, g, shift, scale, w_q, q_norm, *, tm):
    rows, D = x.shape
    n = w_q.shape[1]
    tm = _row_tile(rows, tm)
    row = lambda m: pl.BlockSpec((tm, m), lambda i: (i, 0))
    full = lambda a: pl.BlockSpec(a.shape, lambda i: (0,) * a.ndim)
    return pl.pallas_call(
        functools.partial(_q_kernel, hd=q_norm.shape[1]),
        out_shape=jax.ShapeDtypeStruct((rows, n), F32),
        grid=(rows // tm,),
        in_specs=[row(D), full(g), _mod_spec(shift, rows, tm, 1), _mod_spec(scale, rows, tm, 1),
                  full(w_q), full(q_norm)],
        out_specs=row(n),
        compiler_params=_params("parallel"),
        name="moba_q_proj",
    )(x, g, shift, scale, w_q, q_norm)


def _top_blocks(scores, n_past, axis):
    nbk = scores.shape[axis]
    blk_id = lax.broadcasted_iota(jnp.int32, scores.shape, axis)
    sc = jnp.where(blk_id < n_past, scores, -jnp.inf)
    picks = []
    for r in range(MOBA_TOPK):
        m = jnp.max(sc, axis=axis, keepdims=True)
        idx = jnp.min(jnp.where(sc == m, blk_id, nbk), axis=axis, keepdims=True)
        picks.append((idx, (jnp.zeros_like(idx) + r) < n_past))
        sc = jnp.where(blk_id == idx, -jnp.inf, sc)
    return picks


def _q_sel_kernel(x_ref, g_ref, sh_ref, sc_ref, w_ref, qn_ref, ks_ref, qa_ref, *, hd, seq, scale):
    tm = x_ref.shape[0]
    nbk = ks_ref.shape[1]
    h = _prenorm(x_ref[...], g_ref[...], sh_ref[0], sc_ref[0]).astype(BF16)
    qq = jnp.dot(h, w_ref[...], preferred_element_type=F32)
    pos = (pl.program_id(0) * tm) % seq + lax.broadcasted_iota(jnp.int32, (1, tm), 1)
    own = pos // MOBA_BLOCK
    blk_id = lax.broadcasted_iota(jnp.int32, (nbk, tm), 0)
    for a in range(ATT_HEADS):
        sl = slice(a * hd, (a + 1) * hd)
        qh = _head_rms(qq[:, sl], qn_ref[...])
        scores = _dot_nt_3pass(ks_ref[0, :, sl] * (1.0 / MOBA_BLOCK), qh)
        chosen = blk_id < 0
        for idx, ok in _top_blocks(scores, own, 0):
            chosen = chosen | ((blk_id == idx) & ok)
        is_prev = blk_id == own - 1
        far = jnp.where(chosen & jnp.logical_not(is_prev), 0.0, NEG)
        prev = jnp.max(jnp.where(chosen & is_prev, 0.0, NEG), axis=0, keepdims=True)
        mask_t = jnp.concatenate([far, jnp.zeros((hd - nbk - SUBLANES, tm), F32),
                                  jnp.broadcast_to(prev, (SUBLANES, tm))], axis=0)
        qa_ref[:, 2 * a * hd:(2 * a + 1) * hd] = (qh * scale).astype(BF16)
        qa_ref[:, (2 * a + 1) * hd:(2 * a + 2) * hd] = mask_t.T.astype(BF16)


def _q_proj_sel(x, g, shift, scale, w_q, q_norm, ksum, *, tm, seq):
    rows, D = x.shape
    n = w_q.shape[1]
    hd = q_norm.shape[1]
    B, nbk, _ = ksum.shape
    tm = _row_tile(rows, tm)
    tiles_per_seq = seq // tm
    assert seq % tm == 0 and tm % LANES == 0 and nbk % SUBLANES == 0 and nbk + SUBLANES <= hd
    row = lambda m: pl.BlockSpec((tm, m), lambda i: (i, 0))
    full = lambda a: pl.BlockSpec(a.shape, lambda i: (0,) * a.ndim)
    return pl.pallas_call(
        functools.partial(_q_sel_kernel, hd=hd, seq=seq, scale=hd ** -0.5),
        out_shape=jax.ShapeDtypeStruct((rows, 2 * n), BF16),
        grid=(rows // tm,),
        in_specs=[row(D), full(g), _mod_spec(shift, rows, tm, 1), _mod_spec(scale, rows, tm, 1),
                  full(w_q), full(q_norm),
                  pl.BlockSpec((1, nbk, n), lambda i: (i // tiles_per_seq, 0, 0))],
        out_specs=row(2 * n),
        compiler_params=_params("parallel"),
        name="moba_q_proj_select",
    )(x, g, shift, scale, w_q, q_norm, ksum)


def _moba_prompt_kernel(pt_ref, qa_ref, ka_ref, va_ref, bias_ref, *rest,
                        blk, group, n_groups, halves, heads, n_page_refs, ppb):
    page_refs = rest[:n_page_refs]
    o_ref, ps_ref = rest[n_page_refs:n_page_refs + 2]
    m_ref, acc_ref, sa_ref, sb_ref, pa_ref, pb_ref, al_ref, pn_ref, m0_ref = rest[n_page_refs + 2:]
    for r in range(n_page_refs // ppb):
        total = jnp.sum(page_refs[r * ppb][0], axis=0)
        for p in range(1, ppb):
            total = total + jnp.sum(page_refs[r * ppb + p][0], axis=0)
        ps_ref[r] = total

    it = pl.program_id(2)
    hd = qa_ref.shape[1] // (2 * heads)
    last_group = n_groups - 1

    def lanes(hh):
        return slice(hh * 2 * hd, (hh + 1) * 2 * hd)

    def rows_of(g):
        return pl.ds(pl.multiple_of(g * (group * blk), group * blk), group * blk)

    def scores_of(hh, g):
        return lax.dot_general(qa_ref[:, lanes(hh)], ka_ref[rows_of(g), lanes(hh)], NT,
                               preferred_element_type=F32)

    for hh in range(heads):
        sa_ref[hh] = scores_of(hh, 0)

    t_io = lax.broadcasted_iota(jnp.int32, (blk, blk), 0)
    s_io = lax.broadcasted_iota(jnp.int32, (blk, blk), 1)

    def near_rows(hf):
        own = it * halves + hf
        return (pl.ds(pl.multiple_of(jnp.maximum(own - 1, 0) * blk, blk), blk),
                pl.ds(pl.multiple_of(own * blk, blk), blk))

    for hh, hf in [(hh, hf) for hf in range(halves) for hh in range(heads)]:
        rq = slice(hf * blk, (hf + 1) * blk)
        r_prev, r_own = near_rows(hf)
        k_lanes = slice(hh * 2 * hd, hh * 2 * hd + hd)
        qb = qa_ref[rq, k_lanes]
        prev_mask = qa_ref[rq, (hh + 1) * 2 * hd - 1:(hh + 1) * 2 * hd].astype(F32)
        s_own = lax.dot_general(qb, ka_ref[r_own, k_lanes], NT, preferred_element_type=F32)
        s_own = jnp.where(s_io <= t_io, s_own + bias_ref[hh, 0], NEG)
        s_prev = lax.dot_general(qb, ka_ref[r_prev, k_lanes], NT, preferred_element_type=F32)
        s_prev = s_prev + bias_ref[hh, 1] + prev_mask
        m0 = jnp.maximum(jnp.max(s_own, axis=1, keepdims=True), jnp.max(s_prev, axis=1, keepdims=True))
        pn_ref[hh, rq, :blk] = jnp.exp(s_prev - m0).astype(BF16)
        pn_ref[hh, rq, blk:] = jnp.exp(s_own - m0).astype(BF16)
        m_ref[hh, rq, :] = m0
        m0_ref[hh, rq, :] = m0
    acc_ref[...] = jnp.zeros_like(acc_ref)

    def softmax(hh, s_ref, p_ref, slot):
        m_prev = m_ref[hh]
        s = s_ref[hh]
        m_new = jnp.maximum(m_prev, jnp.max(s, axis=1, keepdims=True))
        p_ref[hh] = jnp.exp(s - m_new).astype(BF16)
        al_ref[hh, slot] = jnp.exp(m_prev - m_new)
        m_ref[hh] = m_new

    def flush(hh, p_ref, slot, g):
        acc_ref[hh] = al_ref[hh, slot] * acc_ref[hh] + jnp.dot(p_ref[hh], va_ref[rows_of(g), lanes(hh)],
                                                               preferred_element_type=F32)

    n_far = it * halves + halves - 2
    n_pairs = ((n_far + group - 1) // group + 1) // 2
    pb_ref[...] = jnp.zeros_like(pb_ref)
    for hh in range(heads):
        al_ref[hh, 1] = jnp.ones(al_ref.shape[2:], F32)

    def body(t, carry):
        for hh in range(heads):
            sb_ref[hh] = scores_of(hh, 2 * t + 1)
            flush(hh, pb_ref, 1, jnp.maximum(2 * t - 1, 0))
            softmax(hh, sa_ref, pa_ref, 0)
        for hh in range(heads):
            sa_ref[hh] = scores_of(hh, jnp.minimum(2 * t + 2, last_group))
            flush(hh, pa_ref, 0, 2 * t)
            softmax(hh, sb_ref, pb_ref, 1)
        return carry

    lax.fori_loop(0, n_pairs, body, 0)

    for hh in range(heads):
        flush(hh, pb_ref, 1, jnp.maximum(2 * n_pairs - 1, 0))
    for hh, hf in [(hh, hf) for hf in range(halves) for hh in range(heads)]:
        rq = slice(hf * blk, (hf + 1) * blk)
        r_prev, r_own = near_rows(hf)
        near = jnp.dot(pn_ref[hh, rq, :blk], va_ref[r_prev, lanes(hh)], preferred_element_type=F32)
        near = near + jnp.dot(pn_ref[hh, rq, blk:], va_ref[r_own, lanes(hh)], preferred_element_type=F32)
        acc = acc_ref[hh, rq, :] + jnp.exp(m0_ref[hh, rq, :] - m_ref[hh, rq, :]) * near
        o_ref[rq, hh * hd:(hh + 1) * hd] = (acc[:, :hd] / acc[:, hd:]).astype(o_ref.dtype)


def _moba_prompt(qa, ka, va, bias, cache_k, pt_flat, *, batch, seq, n_pages, first_block, blocks_per_step,
                 group=4, halves=MOBA_Q_TILE_BLOCKS, heads=1):
    rows, n2 = qa.shape
    hd = n2 // (2 * ATT_HEADS)
    blk = MOBA_BLOCK
    nq = seq // blk
    assert nq % (2 * group) == 0 and group >= 2 and nq % halves == 0 and ATT_HEADS % heads == 0
    nt = nq // halves
    tq = halves * blk
    hg = ATT_HEADS // heads
    steps = batch * hg * nt
    _, page, H, _ = cache_k.shape
    ppb = blk // page
    nbs = n_pages // ppb
    last_block = (pt_flat.shape[0] // n_pages) * nbs - 1
    bps = blocks_per_step

    def step_of(b, h, i):
        return (b * hg + h) * nt + i

    def page_spec(r, p):
        def index_map(b, h, i, pt):
            f = jnp.minimum(first_block + step_of(b, h, i) * bps + r, last_block)
            return (pt[(f // nbs) * n_pages + (f % nbs) * ppb + p], 0, 0, 0)
        return pl.BlockSpec((1, page, H, hd), index_map)

    page_specs = [page_spec(r, p) for r in range(bps) for p in range(ppb)]
    kern = functools.partial(_moba_prompt_kernel, blk=blk, group=group, n_groups=nq // group,
                             halves=halves, heads=heads, n_page_refs=len(page_specs), ppb=ppb)
    return pl.pallas_call(
        kern,
        out_shape=(jax.ShapeDtypeStruct((rows, ATT_HEADS * hd), BF16),
                   jax.ShapeDtypeStruct((steps * bps, H, hd), F32)),
        grid_spec=pltpu.PrefetchScalarGridSpec(
            num_scalar_prefetch=1,
            grid=(batch, hg, nt),
            in_specs=[pl.BlockSpec((tq, heads * 2 * hd), lambda b, h, i, pt: (b * nt + i, h)),
                      pl.BlockSpec((seq, heads * 2 * hd), lambda b, h, i, pt: (b, h)),
                      pl.BlockSpec((seq, heads * 2 * hd), lambda b, h, i, pt: (b, h)),
                      pl.BlockSpec((heads, 2, blk, blk), lambda b, h, i, pt: (h, 0, 0, 0))] + page_specs,
            out_specs=(pl.BlockSpec((tq, heads * hd), lambda b, h, i, pt: (b * nt + i, h)),
                       pl.BlockSpec((bps, H, hd), lambda b, h, i, pt: (step_of(b, h, i), 0, 0))),
            scratch_shapes=[pltpu.VMEM((heads, tq, 1), F32), pltpu.VMEM((heads, tq, 2 * hd), F32),
                            pltpu.VMEM((heads, tq, group * blk), F32), pltpu.VMEM((heads, tq, group * blk), F32),
                            pltpu.VMEM((heads, tq, group * blk), BF16), pltpu.VMEM((heads, tq, group * blk), BF16),
                            pltpu.VMEM((heads, 2, tq, 1), F32), pltpu.VMEM((heads, tq, 2 * blk), BF16),
                            pltpu.VMEM((heads, tq, 1), F32)]),
        compiler_params=_params("parallel", "parallel", "arbitrary"),
        name="moba_prompt_attention",
    )(pt_flat, qa, ka, va, bias, *([cache_k] * len(page_specs)))


def _sample_select_kernel(q_ref, ks_ref, idx_ref, *, hd, blk):
    nbk = ks_ref.shape[1]
    lane = lax.broadcasted_iota(jnp.int32, idx_ref.shape[1:], 1)
    out = jnp.zeros(idx_ref.shape[1:], jnp.int32)
    for a in range(ATT_HEADS):
        means = ks_ref[0, :, a, :] * (1.0 / blk)
        scores = _dot_nt_3pass(q_ref[0, :, a * hd:(a + 1) * hd], means)
        for r, (idx, _) in enumerate(_top_blocks(scores, nbk, 1)):
            out = jnp.where(lane == a * MOBA_TOPK + r, idx, out)
    idx_ref[0] = out


def _sample_select(q8, ksum):
    DB, R, n = q8.shape
    _, nbk, H, hd = ksum.shape
    assert nbk >= MOBA_TOPK and H == ATT_HEADS
    kern = functools.partial(_sample_select_kernel, hd=hd, blk=MOBA_BLOCK)
    return pl.pallas_call(
        kern,
        out_shape=jax.ShapeDtypeStruct((DB, R, LANES), jnp.int32),
        grid=(DB,),
        in_specs=[pl.BlockSpec((1, R, n), lambda d: (d, 0, 0)),
                  pl.BlockSpec((1, nbk, H, hd), lambda d: (d, 0, 0, 0))],
        out_specs=pl.BlockSpec((1, R, LANES), lambda d: (d, 0, 0)),
        compiler_params=_params("parallel"),
        name="sample_select",
    )(q8, ksum)


def _sample_attn_kernel(pt_ref, sel_ref, q_ref, kn_ref, vn_ref, bias_ref, ck_ref, cv_ref, o_ref,
                        kbuf, vbuf, sem, *, lq_n, ppb, n_pages, page, scale):
    g = pl.program_id(0)
    nslot = lq_n * MOBA_TOPK
    nsl = nslot * ppb

    def copies(step, slot):
        d = step // ATT_HEADS
        a = step % ATT_HEADS
        out = []
        for j in range(nsl):
            phys = pt_ref[d * n_pages + sel_ref[step * nslot + j // ppb] * ppb + j % ppb]
            rows = pl.ds(j * page, page)
            out.append(pltpu.make_async_copy(ck_ref.at[phys, :, a, :], kbuf.at[slot, rows, :], sem.at[slot, 0, j]))
            out.append(pltpu.make_async_copy(cv_ref.at[phys, :, a, :], vbuf.at[slot, rows, :], sem.at[slot, 1, j]))
        return out

    @pl.when(g == 0)
    def _():
        for c in copies(0, 0):
            c.start()

    @pl.when(g + 1 < pl.num_programs(0))
    def _():
        for c in copies(g + 1, (g + 1) % 2):
            c.start()

    slot = g % 2
    q = q_ref[0] * scale
    row = lax.broadcasted_iota(jnp.int32, (q.shape[0], 1), 0)

    new_logits = []
    for s in range(lq_n):
        sc = jnp.sum(q * kn_ref[0, s:s + 1, :], axis=1, keepdims=True) + bias_ref[0, 8:16, s:s + 1]
        new_logits.append(jnp.where(s <= row, sc, NEG))

    for c in copies(g, slot):
        c.wait()

    s_all = lax.dot_general(q.astype(BF16), kbuf[slot].astype(BF16), NT, preferred_element_type=F32)
    parts = []
    for j in range(nsl):
        lp = sel_ref[g * nslot + j // ppb] * ppb + j % ppb
        is_last = (jnp.zeros((q.shape[0], page), jnp.int32) + lp) == n_pages - 1
        sj = s_all[:, j * page:(j + 1) * page] + jnp.where(is_last, bias_ref[0, 0:8, :], 0.0)
        parts.append(jnp.where(row == j // (MOBA_TOPK * ppb), sj, NEG))
    m = jnp.max(functools.reduce(jnp.maximum, parts), axis=1, keepdims=True)
    m = functools.reduce(jnp.maximum, new_logits, m)
    probs = [jnp.exp(p - m) for p in parts]
    den = jnp.sum(functools.reduce(jnp.add, probs), axis=1, keepdims=True)
    pcat = jnp.concatenate([p.astype(BF16) for p in probs], axis=1)
    acc = jnp.dot(pcat, vbuf[slot].astype(BF16), preferred_element_type=F32)
    for s in range(lq_n):
        e = jnp.exp(new_logits[s] - m)
        den = den + e
        acc = acc + e * vn_ref[0, s:s + 1, :]
    o_ref[0] = acc / den


def _sample_attn(q8, kn8, vn8, bias, cache_k, cache_v, pt_flat, sel_flat, *, lq_n, n_pages):
    DB, R, n = q8.shape
    _, page, H, hd = cache_k.shape
    ppb = MOBA_BLOCK // page
    nsl = lq_n * MOBA_TOPK * ppb
    assert page == LANES and R == SUBLANES and H == ATT_HEADS
    kern = functools.partial(_sample_attn_kernel, lq_n=lq_n, ppb=ppb, n_pages=n_pages, page=page,
                             scale=hd ** -0.5)
    head = pl.BlockSpec((1, R, hd), lambda g, pt, sel: (g // ATT_HEADS, 0, g % ATT_HEADS))
    return pl.pallas_call(
        kern,
        out_shape=jax.ShapeDtypeStruct((DB, R, n), F32),
        grid_spec=pltpu.PrefetchScalarGridSpec(
            num_scalar_prefetch=2,
            grid=(DB * ATT_HEADS,),
            in_specs=[head, head, head,
                      pl.BlockSpec((1, 2 * SUBLANES, LANES), lambda g, pt, sel: (g % ATT_HEADS, 0, 0)),
                      pl.BlockSpec(memory_space=pl.ANY), pl.BlockSpec(memory_space=pl.ANY)],
            out_specs=head,
            scratch_shapes=[pltpu.VMEM((2, nsl * page, hd), F32), pltpu.VMEM((2, nsl * page, hd), F32),
                            pltpu.SemaphoreType.DMA((2, 2, nsl))]),
        compiler_params=_params("arbitrary"),
        name="sample_attention",
    )(pt_flat, sel_flat, q8, kn8, vn8, bias, cache_k, cache_v)


def _bucket_of_distance():
    n = np.arange(MAX_DISTANCE)
    max_exact = N_BUCKETS // 2
    nf = np.maximum(n, max_exact).astype(np.float32)
    large = max_exact + (np.log(nf / np.float32(max_exact)) / np.float32(math.log(MAX_DISTANCE / max_exact))
                         * np.float32(N_BUCKETS - max_exact)).astype(np.int32)
    return np.where(n < max_exact, n, np.minimum(large, N_BUCKETS - 1)).astype(np.int32)


def _dist_table(rel_bias, width):
    t = rel_bias.astype(F32)[_bucket_of_distance()].T
    t = t - t[:, MAX_DISTANCE - 1:]
    return jnp.pad(t, ((0, 0), (0, width - MAX_DISTANCE)))


def _toeplitz(first_row_wrapped, n):
    H = first_row_wrapped.shape[0]
    x = jnp.broadcast_to(first_row_wrapped[:, None, :], (H, n, 2 * n)).reshape(H, 2 * n * n)
    return x[:, :n * (2 * n - 1)].reshape(H, n, 2 * n - 1)[:, :, :n]


def _prompt_bias_tiles(rel_bias):
    blk = MOBA_BLOCK
    assert blk >= MAX_DISTANCE
    t = _dist_table(rel_bias, blk)
    rev = jnp.flip(t[:, 1:], axis=1)
    zeros = jnp.zeros_like(t)
    own = _toeplitz(jnp.concatenate([t[:, :1], zeros, rev], axis=1), blk)
    prev = _toeplitz(jnp.concatenate([zeros[:, :1], rev, zeros], axis=1), blk)
    return jnp.stack([own, prev], axis=1)


def _sample_bias_rows(rel_bias, *, page, lq_n):
    assert page == LANES and page >= MAX_DISTANCE and lq_n <= SUBLANES
    t = _dist_table(rel_bias, 2 * page)
    r = np.arange(SUBLANES)[:, None]
    c = np.arange(LANES)[None, :]
    return jnp.concatenate([t[:, page + r - c], t[:, np.clip(r - c, 0, None)]], axis=1)


def _hi_lo(w):
    hi = w.astype(BF16)
    return jnp.stack([hi, (w - hi.astype(F32)).astype(BF16)], axis=1)


def _pad_tokens(a, groups, per, to):
    return jnp.pad(a.reshape(groups, per, a.shape[-1]), ((0, 0), (0, to - per), (0, 0)))


def _trunk(x, mods, kvmod, s0, W, *, batch, seq, tm, attend, make_ctx):
    n_a = W['gla_w_main'].shape[0]
    depth = W['mlp_w_up_f32'].shape[0]
    dk = W['gla_b_gate'].shape[2]
    dv = W['gla_w_out'].shape[1]
    states = []
    k = v = ctx = None
    for i in range(depth):
        sh1, sc1, g1, sh2, sc2, g2 = mods[i]
        if i == n_a:
            k, v, ctx = make_ctx(x, kvmod)
        if i < n_a:
            qk, vv, gt, la = _gla_in(x, W['norm_mix'][i], sh1, sc1, W['gla_w_main'][i], W['gla_w_r'][i],
                                     W['gla_w_g2'][i], W['gla_b_gate'][i], dk=dk, dv=dv, tm=tm)
            chunk = math.gcd(GLA_CHUNK, seq)
            if chunk < 16:
                padded = 16
                qk, vv, la = (_pad_tokens(t, batch, seq, padded).reshape(batch * padded, -1)
                              for t in (qk, vv, la))
                o, s = _gla_rec(qk, vv, la, s0[i], seq=padded, chunk=padded, tl=padded)
                o = o.reshape(batch, padded, dv)[:, :seq].reshape(batch * seq, dv)
            else:
                names = W['cast_queue'].pop(0) if W['cast_queue'] else ()
                f32_weights = tuple(W[name + '_f32'] for name in names)
                steps = batch * (seq // min(1024, seq))
                if all(_cast_blocks(w.shape) <= steps for w in f32_weights):
                    o, s, *done = _gla_rec(qk, vv, la, s0[i], seq=seq, chunk=chunk, tl=1024,
                                           cast_weights=f32_weights)
                else:
                    done = [w.astype(BF16) for w in f32_weights]
                    o, s = _gla_rec(qk, vv, la, s0[i], seq=seq, chunk=chunk, tl=1024)
                W.update(zip(names, done))
            states.append(s)
            mix_inputs, w_mix = (o, gt, W['gla_onorm'][i]), W['gla_w_out'][i]
        while W['cast_queue'] and (i >= n_a or 'mlp_w_up' not in W):
            for name in W['cast_queue'].pop(0):
                W[name] = W[name + '_f32'].astype(BF16)
        if i >= n_a:
            j = i - n_a
            q_args = (W['norm_mix'][i], sh1, sc1, W['moba_w_q'][j], W['q_norm'][j])
            mix_inputs, w_mix = (attend(x, q_args, ctx, j),), W['moba_w_o'][j]
        x = _mix_mlp(mix_inputs, x, w_mix, g1, W['norm_mlp'][i], sh2, sc2, g2,
                     W['mlp_w_up'][i], W['mlp_w_down'][i], tm=min(tm * 2, 1024), tf=1024)
    return x, jnp.stack(states), k, v


def kernel(x_prompt, x_sample, state_gla, cache_k, cache_v, page_table, c_prompt, c_sample,
           ada_w, ada_b, norm_mix, norm_mlp, gla_w_in, gla_w_gate2, gla_b_gate, gla_onorm, gla_w_out,
           kv_ada_w, kv_ada_b, kv_norm, w_kv, k_norm, moba_w_q, q_norm, moba_w_o, rel_bias,
           mlp_w_up, mlp_w_down):
    B, L, D = x_prompt.shape
    DB, LQ, _ = x_sample.shape
    depth = ada_w.shape[0]
    n_a = gla_w_in.shape[0]
    rank, dk = gla_w_gate2.shape[1:]
    dv = gla_w_out.shape[1]
    n_pool, page, H, hd = cache_k.shape
    n_pages = page_table.shape[1]
    past = n_pages * page
    assert H == ATT_HEADS and L % MOBA_BLOCK == 0 and MOBA_BLOCK % page == 0
    assert past % MOBA_BLOCK == 0 and LQ <= min(SUBLANES, MOBA_BLOCK) and rank <= LANES

    W = {
        'norm_mix': norm_mix.reshape(depth, 1, D), 'norm_mlp': norm_mlp.reshape(depth, 1, D),
        'gla_w_main': gla_w_in[:, :, :2 * dk + 2 * dv].astype(BF16),
        'gla_w_r': jnp.pad(gla_w_in[:, :, 2 * dk + 2 * dv:], ((0, 0), (0, 0), (0, LANES - rank))).astype(BF16),
        'gla_w_g2': _hi_lo(jnp.pad(gla_w_gate2, ((0, 0), (0, LANES - rank), (0, 0)))),
        'gla_b_gate': gla_b_gate.reshape(n_a, 1, dk), 'gla_onorm': gla_onorm.reshape(n_a, 1, -1),
        'gla_w_out': gla_w_out.astype(BF16),
        'q_norm': q_norm.reshape(-1, 1, hd),
        'mlp_w_up_f32': mlp_w_up, 'mlp_w_down_f32': mlp_w_down,
        'moba_w_q_f32': moba_w_q, 'moba_w_o_f32': moba_w_o,
        'cast_queue': [('mlp_w_up', 'mlp_w_down'), ('moba_w_q', 'moba_w_o')],
    }
    w_kv_b = w_kv.astype(BF16)
    kv_norm2 = kv_norm.reshape(1, D)
    k_norm2 = k_norm.reshape(1, hd)

    n_c = B + DB
    c_all = jnp.pad(jnp.concatenate([c_prompt, c_sample], axis=0), ((0, -n_c % SUBLANES), (0, 0)))
    mod = _ada(c_all, ada_w, ada_b)
    kvm = _ada(c_all, kv_ada_w[None], kv_ada_b[None])[0]

    def prompt_mod(m):
        return m[:B].reshape(B, 1, D)

    def sample_mod(m):
        return jnp.repeat(m[B:n_c], LQ, axis=0).reshape(1, DB * LQ, D)

    def split_mods(pick):
        layers = [[pick(mod[i, :, t * D:(t + 1) * D]) for t in range(6)] for i in range(depth)]
        return layers, [pick(kvm[:, t * D:(t + 1) * D]) for t in range(2)]

    p_mods, p_kvmod = split_mods(prompt_mod)
    p_bias = _prompt_bias_tiles(rel_bias)

    def prompt_ctx(x, kvmod):
        k, v, ka, va, ksum = _kv(x, kv_norm2, kvmod[0], kvmod[1], w_kv_b, k_norm2, tm=512, seq=L,
                                 for_prompt=True)
        return k, v, (ka, va, ksum.reshape(B, L // MOBA_BLOCK, H * hd))

    pt_flat = page_table.reshape(-1).astype(jnp.int32)
    n_sample_blocks = DB * (n_pages // (MOBA_BLOCK // page))
    moba_steps = B * ATT_HEADS * (L // (MOBA_Q_TILE_BLOCKS * MOBA_BLOCK))
    blocks_per_step = -(-n_sample_blocks // ((depth - n_a) * moba_steps))
    sample_block_sums = []

    def prompt_attend(x, q_args, ctx, j):
        ka, va, ksum = ctx
        qa = _q_proj_sel(x, *q_args, ksum, tm=512, seq=L)
        o, sums = _moba_prompt(qa, ka, va, p_bias, cache_k, pt_flat, batch=B, seq=L, n_pages=n_pages,
                               first_block=j * moba_steps * blocks_per_step, blocks_per_step=blocks_per_step)
        sample_block_sums.append(sums)
        return o

    s0_p = jnp.zeros((n_a, B) + state_gla.shape[2:], F32)
    y_p, st_p, k_p, v_p = _trunk(x_prompt.reshape(B * L, D), p_mods, p_kvmod, s0_p, W, batch=B, seq=L,
                                 tm=512, attend=prompt_attend, make_ctx=prompt_ctx)

    s_mods, s_kvmod = split_mods(sample_mod)
    s_bias = _sample_bias_rows(rel_bias, page=page, lq_n=LQ)
    rows_s = DB * LQ

    def sample_ctx(x, kvmod):
        k, v = _kv(x, kv_norm2, kvmod[0], kvmod[1], w_kv_b, k_norm2, tm=rows_s, seq=LQ, for_prompt=False)
        ksum = jnp.concatenate(sample_block_sums, axis=0)[:n_sample_blocks].reshape(DB, -1, H, hd)
        return k, v, (_pad_tokens(k, DB, LQ, SUBLANES), _pad_tokens(v, DB, LQ, SUBLANES), ksum)

    def sample_attend(x, q_args, ctx, j):
        kn8, vn8, ksum = ctx
        q8 = _pad_tokens(_q_proj(x, *q_args, tm=rows_s), DB, LQ, SUBLANES)
        picks = _sample_select(q8, ksum)[:, :LQ, :ATT_HEADS * MOBA_TOPK]
        sel_flat = picks.reshape(DB, LQ, ATT_HEADS, MOBA_TOPK).transpose(0, 2, 1, 3).reshape(-1)
        o8 = _sample_attn(q8, kn8, vn8, s_bias, cache_k, cache_v, pt_flat, sel_flat,
                          lq_n=LQ, n_pages=n_pages)
        return o8[:, :LQ].reshape(rows_s, H * hd)

    y_s, st_s, k_s, v_s = _trunk(x_sample.reshape(rows_s, D), s_mods, s_kvmod, state_gla.astype(F32), W,
                                 batch=DB, seq=LQ, tm=rows_s, attend=sample_attend, make_ctx=sample_ctx)

    return (y_p.reshape(B, L, D), y_s.reshape(DB, LQ, D), st_p, st_s,
            k_p.reshape(B, L, H, hd), v_p.reshape(B, L, H, hd),
            k_s.reshape(DB, LQ, H, hd), v_s.reshape(DB, LQ, H, hd))
```
